```python
import jax, jax.numpy as jnp
from jax import lax
import numpy as np

D_MODEL = 1024
BATCH = 8
SEQ = 4096
DEPTH = 1
DEC_BATCH = 32
DEC_SEQ = 64
PAST_LEN = 2048

CHUNK = 64
Q_BLOCK = 128
N_MEM = 256
EPS = 1e-6
BRANCH_W = D_MODEL // 2
N_BRANCH = 3
SB_HEADS = 8
SB_DIM = BRANCH_W // SB_HEADS
SB_W = SB_HEADS * SB_DIM
GLA_HEADS = 4
GLA_DK = 64
GLA_DV = BRANCH_W // GLA_HEADS
GLA_K = GLA_HEADS * GLA_DK
GLA_V = GLA_HEADS * GLA_DV
GLA_RANK = 16
GLA_TAU = 16.0
MEM_HEADS = 4
MEM_DIM = BRANCH_W // MEM_HEADS
MEM_W = MEM_HEADS * MEM_DIM
SPLITS = (SB_W, SB_W, SB_W, GLA_K, GLA_K, GLA_V, GLA_RANK, GLA_V, MEM_W, N_BRANCH * D_MODEL)
IN_W = SB_W * 3 + GLA_K * 2 + GLA_V * 2 + GLA_RANK + MEM_W + N_BRANCH * D_MODEL
N_GROUPS = 4
EXPERTS_PER_GROUP = 8
N_EXPERTS = N_GROUPS * EXPERTS_PER_GROUP
TOP_K = 2
D_EXPERT = D_MODEL // 2
MOE_BLOCK = 128

kernel_name = 'hybrid_stream_sb_gla_hmoe_step'


def rmsnorm(x, g):
    x32 = x.astype(jnp.float32)
    y = x32 * lax.rsqrt(jnp.mean(x32 * x32, axis=-1, keepdims=True) + EPS)
    return (y * g.astype(jnp.float32)).astype(x.dtype)


def mix_projections(x, g, w_in, w_a2, b_a):
    B, T, _ = x.shape
    u = rmsnorm(x, g) @ w_in
    sq, sk, sv, gq, gk, gv, glr, gr, mq, gt = jnp.split(u, np.cumsum(SPLITS)[:-1].tolist(), axis=-1)
    log_a = jax.nn.log_sigmoid((glr @ w_a2 + b_a).astype(jnp.float32)) / GLA_TAU
    hd = lambda a, h: a.reshape(B, T, h, -1)
    return (hd(sq, SB_HEADS), hd(sk, SB_HEADS), hd(sv, SB_HEADS),
            hd(gq, GLA_HEADS) * GLA_DK ** -0.5, hd(gk, GLA_HEADS), hd(gv, GLA_HEADS),
            hd(log_a, GLA_HEADS), gr, hd(mq, MEM_HEADS), gt.reshape(B, T, N_BRANCH, D_MODEL))


def sb_attend(q, k, v, q_pos, k_pos):
    f32 = jnp.float32
    z = jnp.einsum('bqhd,bkhd->bhqk', q.astype(f32), k.astype(f32)) * SB_DIM ** -0.5
    mask = k_pos[None, :] < q_pos[:, None]
    log_1mb = jnp.where(mask, jax.nn.log_sigmoid(-z), 0.0)
    after = lax.cumsum(log_1mb, axis=3, reverse=True) - log_1mb
    att = jnp.where(mask, jnp.exp(jax.nn.log_sigmoid(z) + after), 0.0)
    return jnp.einsum('bhqk,bkhd->bqhd', att, v.astype(f32)).astype(v.dtype)


def sb_prompt(q, k, v):
    T = q.shape[1]
    outs = []
    for i0 in range(0, T, Q_BLOCK):
        i1 = min(i0 + Q_BLOCK, T)
        outs.append(sb_attend(q[:, i0:i1], k[:, :i1], v[:, :i1], jnp.arange(i0, i1), jnp.arange(i1)))
    return jnp.concatenate(outs, axis=1)


def sb_sample(q, k, v, ck, cv):
    P, T = ck.shape[1], q.shape[1]
    kk = jnp.concatenate([ck.astype(k.dtype), k], axis=1)
    vv = jnp.concatenate([cv.astype(v.dtype), v], axis=1)
    return sb_attend(q, kk, vv, P + jnp.arange(T), jnp.arange(P + T))


def gla_chunked(q, k, v, log_a, S0, chunk):
    B, T, H, _ = q.shape
    n = T // chunk
    f32 = jnp.float32
    def blk(a):
        return a.astype(f32).reshape(B, n, chunk, H, a.shape[-1]).transpose(1, 0, 3, 2, 4)
    q, k, v, la = blk(q), blk(k), blk(v), blk(log_a)
    b = jnp.cumsum(la, axis=3)
    b_last = b[:, :, :, -1:, :]
    q_d = q * jnp.exp(b)
    k_d = k * jnp.exp(-b)
    k_e = k * jnp.exp(b_last - b)
    causal = jnp.tril(jnp.ones((chunk, chunk), bool))
    att = jnp.where(causal, jnp.einsum('nbhid,nbhjd->nbhij', q_d, k_d), 0.0)
    o_intra = jnp.einsum('nbhij,nbhjv->nbhiv', att, v)
    def step(S, xs):
        qd, ke, vv, bl = xs
        o = jnp.einsum('bhid,bhdv->bhiv', qd, S)
        S = jnp.exp(bl[:, :, 0, :])[..., None] * S + jnp.einsum('bhjd,bhjv->bhdv', ke, vv)
        return S, o
    S_fin, o_inter = lax.scan(step, S0.astype(f32), (q_d, k_e, v, b_last))
    o = (o_intra + o_inter).transpose(1, 0, 3, 2, 4).reshape(B, T, H, v.shape[-1])
    return o, S_fin


def gla_out(o, gr, g):
    B, T, H, DV = o.shape
    on = o * lax.rsqrt(jnp.mean(o * o, axis=-1, keepdims=True) + EPS) * g.astype(jnp.float32).reshape(H, DV)
    return (on.reshape(B, T, H * DV) * jax.nn.silu(gr.astype(jnp.float32))).astype(gr.dtype)


def memory_kv(mem, g, w):
    B, M, _ = mem.shape
    mk, mv = jnp.split(rmsnorm(mem, g) @ w, 2, axis=-1)
    return mk.reshape(B, M, MEM_HEADS, MEM_DIM), mv.reshape(B, M, MEM_HEADS, MEM_DIM)


def memory_attend(q, mk, mv):
    B, T = q.shape[:2]
    s = jnp.einsum('bqhd,bmhd->bhqm', q.astype(jnp.float32), mk.astype(jnp.float32)) * MEM_DIM ** -0.5
    p = jax.nn.softmax(s, axis=-1)
    o = jnp.einsum('bhqm,bmhd->bqhd', p, mv.astype(jnp.float32))
    return o.reshape(B, T, MEM_W).astype(q.dtype)


def merge_branches(x, o_sb, o_gla, o_mem, gates, w_branch, w_out):
    B, T, _ = x.shape
    o = jnp.stack([o_sb.reshape(B, T, SB_W), o_gla, o_mem], axis=2)
    p = jnp.einsum('btnc,ncd->btnd', o, w_branch)
    mixed = jnp.sum(jax.nn.sigmoid(gates) * p, axis=2)
    return x + mixed @ w_out


def hier_route(xt, w_coarse, b_coarse, w_fine, b_fine):
    lc = (xt @ w_coarse + b_coarse).astype(jnp.float32)
    pc = jax.nn.softmax(lc, axis=-1)
    p_grp, grp = lax.top_k(pc, 1)
    lf_all = jnp.einsum('td,gde->tge', xt, w_fine) + b_fine
    lf = jnp.take_along_axis(lf_all, grp[:, :, None], axis=1)[:, 0].astype(jnp.float32)
    pf = jax.nn.softmax(lf, axis=-1)
    topv, topi = lax.top_k(pf, TOP_K)
    topv = topv / jnp.sum(topv, axis=-1, keepdims=True)
    return grp * EXPERTS_PER_GROUP + topi, p_grp * topv


def moe_experts(xt, expert_idx, weights, wg, wu, wd):
    T, D = xt.shape
    A = T * TOP_K
    flat_e = expert_idx.reshape(A)
    flat_t = jnp.repeat(jnp.arange(T), TOP_K)
    flat_w = weights.reshape(A)
    order = jnp.argsort(flat_e)
    se = flat_e[order]
    counts = jnp.bincount(flat_e, length=N_EXPERTS)
    padded = (counts + MOE_BLOCK - 1) // MOE_BLOCK * MOE_BLOCK
    start = jnp.cumsum(counts) - counts
    pend = jnp.cumsum(padded)
    pstart = pend - padded
    dest = pstart[se] + jnp.arange(A) - start[se]
    n_blk = -(-(A + N_EXPERTS * (MOE_BLOCK - 1)) // MOE_BLOCK)
    P = n_blk * MOE_BLOCK
    tok_buf = jnp.full((P,), T, jnp.int32).at[dest].set(flat_t[order])
    w_buf = jnp.zeros((P,), xt.dtype).at[dest].set(flat_w[order].astype(xt.dtype))
    blk_e = jnp.minimum(jnp.searchsorted(pend, jnp.arange(n_blk) * MOE_BLOCK, side='right'), N_EXPERTS - 1)
    xpad = jnp.concatenate([xt, jnp.zeros((1, D), xt.dtype)], axis=0)
    xb = xpad[tok_buf].reshape(n_blk, MOE_BLOCK, D)
    def expert_block(args):
        xblk, e = args
        return (jax.nn.silu(xblk @ wg[e]) * (xblk @ wu[e])) @ wd[e]
    yb = lax.map(expert_block, (xb, blk_e))
    y = jnp.zeros((T + 1, D), xt.dtype).at[tok_buf].add(yb.reshape(P, D) * w_buf[:, None])
    return y[:T]


def channel_mixer(h, g, w_coarse, b_coarse, w_fine, b_fine, wg, wu, wd):
    B, T, D = h.shape
    ht = rmsnorm(h, g).reshape(B * T, D)
    idx, wts = hier_route(ht, w_coarse, b_coarse, w_fine, b_fine)
    return h + moe_experts(ht, idx, wts, wg, wu, wd).reshape(B, T, D)


def setup_inputs(seed: int = 0) -> dict:
    key = jax.random.key(seed)
    ks = jax.random.split(key, 26)
    f32 = jnp.float32
    nrm = lambda k, shape, scale: jax.random.normal(k, shape, f32) * scale
    gain = lambda k, shape: 1.0 + 0.01 * jax.random.normal(k, shape, f32)
    L, D = DEPTH, D_MODEL
    return {
        'x_prompt': nrm(ks[0], (BATCH, SEQ, D), 1.0),
        'x_sample': nrm(ks[1], (DEC_BATCH, DEC_SEQ, D), 1.0),
        'mem_prompt': nrm(ks[2], (BATCH, N_MEM, D), 1.0),
        'cache_sb_k': nrm(ks[3], (L, DEC_BATCH, PAST_LEN, SB_HEADS, SB_DIM), 1.0),
        'cache_sb_v': nrm(ks[4], (L, DEC_BATCH, PAST_LEN, SB_HEADS, SB_DIM), 1.0),
        'state_gla': nrm(ks[5], (L, DEC_BATCH, GLA_HEADS, GLA_DK, GLA_DV), 1.0),
        'cache_mem_k': nrm(ks[6], (L, DEC_BATCH, N_MEM, MEM_HEADS, MEM_DIM), 1.0),
        'cache_mem_v': nrm(ks[7], (L, DEC_BATCH, N_MEM, MEM_HEADS, MEM_DIM), 1.0),
        'norm_mix': gain(ks[8], (L, D)),
        'w_in': nrm(ks[9], (L, D, IN_W), D ** -0.5),
        'w_gla_a2': nrm(ks[10], (L, GLA_RANK, GLA_K), GLA_RANK ** -0.5),
        'b_gla_a': nrm(ks[11], (L, GLA_K), 0.02),
        'gla_norm': gain(ks[12], (L, GLA_V)),
        'norm_mem': gain(ks[13], (L, D)),
        'w_mem_kv': nrm(ks[14], (L, D, 2 * MEM_W), D ** -0.5),
        'w_branch': nrm(ks[15], (L, N_BRANCH, BRANCH_W, D), BRANCH_W ** -0.5),
        'w_out': nrm(ks[16], (L, D, D), D ** -0.5),
        'norm_ffn': gain(ks[17], (L, D)),
        'w_coarse': nrm(ks[18], (L, D, N_GROUPS), D ** -0.5),
        'b_coarse': nrm(ks[19], (L, N_GROUPS), 0.01),
        'w_fine': nrm(ks[20], (L, N_GROUPS, D, EXPERTS_PER_GROUP), D ** -0.5),
        'b_fine': nrm(ks[21], (L, N_GROUPS, EXPERTS_PER_GROUP), 0.01),
        'w_e_gate': nrm(ks[22], (L, N_EXPERTS, D, D_EXPERT), D ** -0.5),
        'w_e_up': nrm(ks[23], (L, N_EXPERTS, D, D_EXPERT), D ** -0.5),
        'w_e_down': nrm(ks[24], (L, N_EXPERTS, D_EXPERT, D), D_EXPERT ** -0.5),
        'norm_final': gain(ks[25], (D,)),
    }


def reference(x_prompt, x_sample, mem_prompt, cache_sb_k, cache_sb_v, state_gla, cache_mem_k, cache_mem_v,
              norm_mix, w_in, w_gla_a2, b_gla_a, gla_norm, norm_mem, w_mem_kv, w_branch, w_out,
              norm_ffn, w_coarse, b_coarse, w_fine, b_fine, w_e_gate, w_e_up, w_e_down, norm_final):
    hp, hs = x_prompt, x_sample
    sbk_p, sbv_p, gla_p, mk_p, mv_p = [], [], [], [], []
    sbk_s, sbv_s, gla_s = [], [], []
    for l in range(DEPTH):
        ffn_w = (norm_ffn[l], w_coarse[l], b_coarse[l], w_fine[l], b_fine[l], w_e_gate[l], w_e_up[l], w_e_down[l])
        sq, sk, sv, gq, gk, gv, la, gr, mq, gt = mix_projections(hp, norm_mix[l], w_in[l], w_gla_a2[l], b_gla_a[l])
        s0 = jnp.zeros((hp.shape[0], GLA_HEADS, GLA_DK, GLA_DV), jnp.float32)
        o_gla, s_p = gla_chunked(gq, gk, gv, la, s0, CHUNK)
        mk, mv = memory_kv(mem_prompt, norm_mem[l], w_mem_kv[l])
        hp = merge_branches(hp, sb_prompt(sq, sk, sv), gla_out(o_gla, gr, gla_norm[l]),
                            memory_attend(mq, mk, mv), gt, w_branch[l], w_out[l])
        hp = channel_mixer(hp, *ffn_w)
        sbk_p.append(sk); sbv_p.append(sv); gla_p.append(s_p.astype(hp.dtype)); mk_p.append(mk); mv_p.append(mv)
        sq, sk, sv, gq, gk, gv, la, gr, mq, gt = mix_projections(hs, norm_mix[l], w_in[l], w_gla_a2[l], b_gla_a[l])
        o_gla, s_s = gla_chunked(gq, gk, gv, la, state_gla[l], hs.shape[1])
        hs = merge_branches(hs, sb_sample(sq, sk, sv, cache_sb_k[l], cache_sb_v[l]), gla_out(o_gla, gr, gla_norm[l]),
                            memory_attend(mq, cache_mem_k[l], cache_mem_v[l]), gt, w_branch[l], w_out[l])
        hs = channel_mixer(hs, *ffn_w)
        sbk_s.append(sk); sbv_s.append(sv); gla_s.append(s_s.astype(hs.dtype))
    y_prompt = rmsnorm(hp, norm_final)
    y_sample = rmsnorm(hs, norm_final)
    sb_k_prompt, sb_v_prompt = jnp.stack(sbk_p), jnp.stack(sbv_p)
    gla_state_prompt = jnp.stack(gla_p)
    mem_k_prompt, mem_v_prompt = jnp.stack(mk_p), jnp.stack(mv_p)
    sb_k_sample, sb_v_sample = jnp.stack(sbk_s), jnp.stack(sbv_s)
    gla_state_sample = jnp.stack(gla_s)
    return (y_prompt, y_sample, sb_k_prompt, sb_v_prompt, gla_state_prompt, mem_k_prompt, mem_v_prompt,
            sb_k_sample, sb_v_sample, gla_state_sample)
```

```python
import functools

import jax
import jax.numpy as jnp
from jax import lax
from jax.experimental import pallas as pl
from jax.experimental.pallas import tpu as pltpu

F32 = jnp.float32
BF16 = jnp.bfloat16
EPS = 1e-6

SB_HEADS = 8
SB_DIM = 64
GLA_HEADS = 4
GLA_DK = 64
GLA_DV = 128
GLA_RANK = 16
GLA_TAU = 16.0
GLA_CHUNK = 64
MEM_HEADS = 4
MEM_DIM = 128
N_BRANCH = 3
N_GROUPS = 4
EXPERTS_PER_GROUP = 8
N_EXPERTS = N_GROUPS * EXPERTS_PER_GROUP
TOP_K = 2

LANES = 128
VMEM_LIMIT = 56 * 1024 * 1024
NEG_BIG = -1e30

_NT = (((1,), (1,)), ((), ()))


def _dot(a, b):
    return jnp.dot(a, b, preferred_element_type=F32)


def _dot_nt(a, b):
    return lax.dot_general(a, b, _NT, preferred_element_type=F32)


def _split_bf16(x):
    hi = x.astype(BF16)
    lo = (x - hi.astype(F32)).astype(BF16)
    return hi, lo


def _rmsnorm(x, g):
    return x * lax.rsqrt(jnp.mean(x * x, axis=-1, keepdims=True) + EPS) * g


def _log_sigmoid(x):
    return jnp.minimum(x, 0.0) - jnp.log1p(jnp.exp(-jnp.abs(x)))


def _params(*sem):
    return pltpu.CompilerParams(dimension_semantics=sem, vmem_limit_bytes=VMEM_LIMIT)


_C_SQ, _C_SK, _C_SV = 0, 512, 1024
_C_GQ, _C_GK, _C_GV = 1536, 1792, 2048
_C_GR, _C_MQ, _C_GLR = 2560, 3072, 3584
_PROJ_W = 3712


def _proj_kernel(x_ref, g_ref, w_ref, wa2_ref, ba_ref,
                 sq_ref, sk_ref, sv_ref, gq_ref, gk_ref, gv_ref, la_ref, gr_ref, mq_ref):
    xn = _rmsnorm(x_ref[...], g_ref[...]).astype(BF16)

    def seg(start, width):
        return _dot(xn, w_ref[:, start:start + width])

    sq_ref[...] = (seg(_C_SQ, 512) * SB_DIM ** -0.5).astype(BF16)
    sk_ref[...] = seg(_C_SK, 512)
    sv_ref[...] = seg(_C_SV, 512)
    gq_ref[...] = seg(_C_GQ, 256) * GLA_DK ** -0.5
    gk_ref[...] = seg(_C_GK, 256)
    gv_ref[...] = seg(_C_GV, 512).astype(BF16)
    gr_ref[...] = seg(_C_GR, 512)
    mq_ref[...] = seg(_C_MQ, 512).astype(BF16)
    glr = seg(_C_GLR, LANES).astype(BF16)
    la_ref[...] = _log_sigmoid(_dot(glr, wa2_ref[...]) + ba_ref[...]) * (1.0 / GLA_TAU)


def _proj(x, g, w_proj, wa2, ba, tm):
    n, d = x.shape
    row = lambda w: pl.BlockSpec((tm, w), lambda i: (i, 0))
    full = lambda a: pl.BlockSpec(a.shape, lambda i: (0,) * a.ndim)
    widths = (512, 512, 512, 256, 256, 512, 256, 512, 512)
    dtypes = (BF16, F32, F32, F32, F32, BF16, F32, F32, BF16)
    return pl.pallas_call(
        _proj_kernel,
        grid=(n // tm,),
        in_specs=[row(d), full(g), full(w_proj), full(wa2), full(ba)],
        out_specs=[row(w) for w in widths],
        out_shape=[jax.ShapeDtypeStruct((n, w), dt) for w, dt in zip(widths, dtypes)],
        compiler_params=_params("parallel"),
        name="proj",
    )(x, g, w_proj, wa2, ba)


def _memkv_kernel(x_ref, g_ref, w_ref, mk_ref, mv_ref):
    xn = _rmsnorm(x_ref[...], g_ref[...]).astype(BF16)
    half = mk_ref.shape[-1]
    mk_ref[...] = _dot(xn, w_ref[:, :half])
    mv_ref[...] = _dot(xn, w_ref[:, half:])


def _memkv(x, g, w, tm):
    n, d = x.shape
    half = w.shape[1] // 2
    row = lambda wd: pl.BlockSpec((tm, wd), lambda i: (i, 0))
    full = lambda a: pl.BlockSpec(a.shape, lambda i: (0,) * a.ndim)
    return pl.pallas_call(
        _memkv_kernel,
        grid=(n // tm,),
        in_specs=[row(d), full(g), full(w)],
        out_specs=[row(half), row(half)],
        out_shape=[jax.ShapeDtypeStruct((n, half), F32)] * 2,
        compiler_params=_params("parallel"),
        name="memkv",
    )(x, g, w)


def _sb_kernel(*refs, tq, past_len, tk_past):
    if past_len:
        q_ref, k_ref, v_ref, pk_ref, pv_ref, o_ref, acc_ref, car_ref = refs
    else:
        q_ref, k_ref, v_ref, o_ref, acc_ref, car_ref = refs
    qi = pl.program_id(2)
    q = q_ref[...]
    lane = lax.broadcasted_iota(jnp.int32, (tq, LANES), 1)
    q_heads = (jnp.where(lane < SB_DIM, q, jnp.zeros_like(q)),
               jnp.where(lane >= SB_DIM, q, jnp.zeros_like(q)))

    acc_ref[...] = jnp.zeros_like(acc_ref)
    car_ref[...] = jnp.zeros_like(car_ref)

    def suffix_matrix(tk):
        j = lax.broadcasted_iota(jnp.int32, (tk, tk), 0)
        s = lax.broadcasted_iota(jnp.int32, (tk, tk), 1)
        return (j > s).astype(BF16)

    def block(kb, vb, mask):
        tk = kb.shape[0]
        u = suffix_matrix(tk)
        for h in range(2):
            z = _dot_nt(q_heads[h], kb)
            l1 = _log_sigmoid(-z)
            lz = z + l1
            if mask is not None:
                l1 = jnp.where(mask, l1, 0.0)
            hi, lo = _split_bf16(l1)
            cum = _dot(hi, u) + _dot(lo, u)
            carry = car_ref[h]
            p = jnp.exp(lz + cum + carry)
            if mask is not None:
                p = jnp.where(mask, p, 0.0)
            acc_ref[h] += _dot(p.astype(BF16), vb)
            car_ref[h] = carry + cum[:, :1] + l1[:, :1]

    start = pl.multiple_of(qi * tq, tq)
    r = lax.broadcasted_iota(jnp.int32, (tq, tq), 0)
    c = lax.broadcasted_iota(jnp.int32, (tq, tq), 1)
    block(k_ref[pl.ds(start, tq), :].astype(BF16), v_ref[pl.ds(start, tq), :].astype(BF16), c < r)

    def earlier(i, _):
        s0 = pl.multiple_of((qi - 1 - i) * tq, tq)
        block(k_ref[pl.ds(s0, tq), :].astype(BF16), v_ref[pl.ds(s0, tq), :].astype(BF16), None)
        return 0

    lax.fori_loop(0, qi, earlier, 0)

    if past_len:
        n_past = past_len // tk_past

        def past(i, _):
            s0 = pl.multiple_of((n_past - 1 - i) * tk_past, tk_past)
            block(pk_ref[pl.ds(s0, tk_past), :].astype(BF16),
                  pv_ref[pl.ds(s0, tk_past), :].astype(BF16), None)
            return 0

        lax.fori_loop(0, n_past, past, 0)

    o_ref[...] = jnp.where(lane < SB_DIM, acc_ref[0], acc_ref[1]).astype(o_ref.dtype)


def _sb_attention(q, k, v, past_k=None, past_v=None, *, tq, tk_past=256):
    b, t, w = q.shape
    pairs = w // LANES
    past_len = 0 if past_k is None else past_k.shape[1]
    seq = lambda n: pl.BlockSpec((None, n, LANES), lambda bi, p, qi: (bi, 0, p))
    tile = pl.BlockSpec((None, tq, LANES), lambda bi, p, qi: (bi, qi, p))
    in_specs = [tile, seq(t), seq(t)]
    args = [q, k, v]
    if past_len:
        in_specs += [seq(past_len), seq(past_len)]
        args += [past_k, past_v]
    return pl.pallas_call(
        functools.partial(_sb_kernel, tq=tq, past_len=past_len, tk_past=tk_past),
        grid=(b, pairs, t // tq),
        in_specs=in_specs,
        out_specs=tile,
        out_shape=jax.ShapeDtypeStruct((b, t, w), BF16),
        scratch_shapes=[pltpu.VMEM((2, tq, LANES), F32), pltpu.VMEM((2, tq, 1), F32)],
        compiler_params=_params("parallel", "parallel", "arbitrary"),
        name="sb_attention",
    )(*args)


def _gla_kernel(gq_ref, gk_ref, gv_ref, la_ref, gr_ref, gn_ref, s0_ref, o_ref, st_ref, st_scr, *, n_chunks):
    c_len = GLA_CHUNK
    kw = GLA_HEADS * GLA_DK
    t = pl.program_id(1)

    @pl.when(t == 0)
    def _():
        st_scr[...] = s0_ref[...]

    r = lax.broadcasted_iota(jnp.int32, (c_len, c_len), 0)
    c = lax.broadcasted_iota(jnp.int32, (c_len, c_len), 1)
    causal = r >= c
    tril = causal.astype(BF16)
    eye = (lax.broadcasted_iota(jnp.int32, (GLA_DV, GLA_DV), 0)
           == lax.broadcasted_iota(jnp.int32, (GLA_DV, GLA_DV), 1)).astype(BF16)
    lane_q = lax.broadcasted_iota(jnp.int32, (c_len, kw), 1)
    lane_s = lax.broadcasted_iota(jnp.int32, (GLA_DV, kw), 1)

    for ci in range(n_chunks):
        rows = slice(ci * c_len, (ci + 1) * c_len)
        la_hi, la_lo = _split_bf16(la_ref[rows, :])
        b = _dot(tril, la_hi) + _dot(tril, la_lo)
        b_last = b[c_len - 1:c_len, :]
        gk = gk_ref[rows, :]
        qd = gq_ref[rows, :] * jnp.exp(b)
        kd = (gk * jnp.exp(-b)).astype(BF16)
        ke = (gk * jnp.exp(b_last - b)).astype(BF16)
        st = st_scr[...]
        st_b = st.astype(BF16)
        st_new = st * jnp.exp(b_last)
        v = gv_ref[rows, :]
        for h in range(GLA_HEADS):
            in_head = (lane_q >= h * GLA_DK) & (lane_q < (h + 1) * GLA_DK)
            qh = jnp.where(in_head, qd, 0.0).astype(BF16)
            att = jnp.where(causal, _dot_nt(qh, kd), 0.0).astype(BF16)
            cols = slice(h * GLA_DV, (h + 1) * GLA_DV)
            vh = v[:, cols]
            o = _dot(att, vh) + _dot_nt(qh, st_b)
            v_t = _dot_nt(eye, vh).astype(BF16)
            in_head_s = (lane_s >= h * GLA_DK) & (lane_s < (h + 1) * GLA_DK)
            st_new = st_new + jnp.where(in_head_s, _dot(v_t, ke), 0.0)
            on = _rmsnorm(o, gn_ref[:, cols])
            gr = gr_ref[rows, cols]
            o_ref[rows, cols] = (on * (gr * jax.nn.sigmoid(gr))).astype(o_ref.dtype)
        st_scr[...] = st_new

    @pl.when(t == pl.num_programs(1) - 1)
    def _():
        st_ref[...] = st_scr[...]


def _gla(gq, gk, gv, la, gr, gn, st0, *, tt):
    b, t, kw = gq.shape
    vw = gv.shape[-1]
    tok = lambda w: pl.BlockSpec((None, tt, w), lambda bi, ti: (bi, ti, 0))
    state = pl.BlockSpec((None, GLA_DV, kw), lambda bi, ti: (bi, 0, 0))
    return pl.pallas_call(
        functools.partial(_gla_kernel, n_chunks=tt // GLA_CHUNK),
        grid=(b, t // tt),
        in_specs=[tok(kw), tok(kw), tok(vw), tok(kw), tok(vw),
                  pl.BlockSpec(gn.shape, lambda bi, ti: (0, 0)), state],
        out_specs=[tok(vw), state],
        out_shape=[jax.ShapeDtypeStruct((b, t, vw), BF16), jax.ShapeDtypeStruct((b, GLA_DV, kw), F32)],
        scratch_shapes=[pltpu.VMEM((GLA_DV, kw), F32)],
        compiler_params=_params("parallel", "arbitrary"),
        name="gla",
    )(gq, gk, gv, la, gr, gn, st0)


def _mem_kernel(q_ref, mk_ref, mv_ref, o_ref):
    for h in range(MEM_HEADS):
        cols = slice(h * MEM_DIM, (h + 1) * MEM_DIM)
        s = _dot_nt(q_ref[:, cols], mk_ref[:, cols].astype(BF16)) * MEM_DIM ** -0.5
        e = jnp.exp(s - jnp.max(s, axis=-1, keepdims=True))
        p = e / jnp.sum(e, axis=-1, keepdims=True)
        o_ref[:, cols] = _dot(p.astype(BF16), mv_ref[:, cols].astype(BF16)).astype(o_ref.dtype)


def _mem_attention(q, mk, mv, *, tq):
    b, t, w = q.shape
    m = mk.shape[1]
    tile = pl.BlockSpec((None, tq, w), lambda bi, qi: (bi, qi, 0))
    mem = pl.BlockSpec((None, m, w), lambda bi, qi: (bi, 0, 0))
    return pl.pallas_call(
        _mem_kernel,
        grid=(b, t // tq),
        in_specs=[tile, mem, mem],
        out_specs=tile,
        out_shape=jax.ShapeDtypeStruct((b, t, w), BF16),
        compiler_params=_params("parallel", "arbitrary"),
        name="mem_attention",
    )(q, mk, mv)


def _merge_kernel(x_ref, osb_ref, ogla_ref, omem_ref, gmix_ref, wgt_ref, wbr_ref, wout_ref,
                  gffn_ref, wr_hi_ref, wr_lo_ref, br_ref, h_ref, ht_ref, idx_ref, wts_ref):
    x = x_ref[...]
    d = x.shape[-1]
    xn = _rmsnorm(x, gmix_ref[...]).astype(BF16)
    mixed = None
    for n, o_ref in enumerate((osb_ref, ogla_ref, omem_ref)):
        gate = jax.nn.sigmoid(_dot(xn, wgt_ref[:, n * d:(n + 1) * d]))
        term = gate * _dot(o_ref[...], wbr_ref[n])
        mixed = term if mixed is None else mixed + term
    h = x + _dot(mixed.astype(BF16), wout_ref[...])
    h_ref[...] = h
    hn = _rmsnorm(h, gffn_ref[...])
    ht_ref[...] = hn.astype(BF16)

    hn_hi, hn_lo = _split_bf16(hn)
    logits = (_dot(hn_hi, wr_hi_ref[...]) + _dot(hn_lo, wr_hi_ref[...])
              + _dot(hn_hi, wr_lo_ref[...]) + br_ref[...])
    lane = lax.broadcasted_iota(jnp.int32, logits.shape, 1)
    rmax = lambda a: jnp.max(a, axis=-1, keepdims=True)
    rmin = lambda a: jnp.min(a, axis=-1, keepdims=True)
    rsum = lambda a: jnp.sum(a, axis=-1, keepdims=True)

    lc = jnp.where(lane < N_GROUPS, logits, NEG_BIG)
    mc = rmax(lc)
    grp = rmin(jnp.where(lc == mc, lane, LANES))
    p_grp = 1.0 / rsum(jnp.exp(lc - mc))

    lo = N_GROUPS + grp * EXPERTS_PER_GROUP
    in_grp = (lane >= lo) & (lane < lo + EXPERTS_PER_GROUP)
    lf = jnp.where(in_grp, logits, NEG_BIG)
    ef = jnp.exp(lf - rmax(lf))
    pf = jnp.where(in_grp, ef / rsum(ef), -1.0)
    v1 = rmax(pf)
    i1 = rmin(jnp.where(pf == v1, lane, LANES))
    pf2 = jnp.where(lane == i1, -1.0, pf)
    v2 = rmax(pf2)
    i2 = rmin(jnp.where(pf2 == v2, lane, LANES))
    tot = v1 + v2
    idx_ref[...] = jnp.where(lane == 0, i1 - N_GROUPS, jnp.where(lane == 1, i2 - N_GROUPS, 0))
    wts_ref[...] = jnp.where(lane == 0, p_grp * (v1 / tot), jnp.where(lane == 1, p_grp * (v2 / tot), 0.0))


def _merge(x, osb, ogla, omem, gmix, wgt, wbr, wout, gffn, wr_hi, wr_lo, br, *, tm):
    n, d = x.shape
    bw = osb.shape[1]
    row = lambda w: pl.BlockSpec((tm, w), lambda i: (i, 0))
    full = lambda a: pl.BlockSpec(a.shape, lambda i: (0,) * a.ndim)
    return pl.pallas_call(
        _merge_kernel,
        grid=(n // tm,),
        in_specs=[row(d), row(bw), row(bw), row(bw), full(gmix), full(wgt), full(wbr), full(wout),
                  full(gffn), full(wr_hi), full(wr_lo), full(br)],
        out_specs=[row(d), row(d), row(LANES), row(LANES)],
        out_shape=[jax.ShapeDtypeStruct((n, d), F32), jax.ShapeDtypeStruct((n, d), BF16),
                   jax.ShapeDtypeStruct((n, LANES), jnp.int32), jax.ShapeDtypeStruct((n, LANES), F32)],
        compiler_params=_params("parallel"),
        name="merge_route",
    )(x, osb, ogla, omem, gmix, wgt, wbr, wout, gffn, wr_hi, wr_lo, br)


def _expert_kernel(blk_e_ref, n_used_ref, x_ref, wg_ref, wu_ref, wd_ref, y_ref):
    i = pl.program_id(0)

    @pl.when(i < n_used_ref[0])
    def _():
        x = x_ref[...]
        g = _dot(x, wg_ref[...])
        u = _dot(x, wu_ref[...])
        a = (g * jax.nn.sigmoid(g) * u).astype(BF16)
        y_ref[...] = _dot(a, wd_ref[...])

    @pl.when(i >= n_used_ref[0])
    def _():
        y_ref[...] = jnp.zeros_like(y_ref)


def _experts(xb, blk_e, n_used, wg, wu, wd, *, tm):
    p, d = xb.shape
    de = wg.shape[-1]
    grid_spec = pltpu.PrefetchScalarGridSpec(
        num_scalar_prefetch=2,
        grid=(p // tm,),
        in_specs=[pl.BlockSpec((tm, d), lambda i, be, nu: (i, 0)),
                  pl.BlockSpec((None, d, de), lambda i, be, nu: (be[i], 0, 0)),
                  pl.BlockSpec((None, d, de), lambda i, be, nu: (be[i], 0, 0)),
                  pl.BlockSpec((None, de, d), lambda i, be, nu: (be[i], 0, 0))],
        out_specs=pl.BlockSpec((tm, d), lambda i, be, nu: (i, 0)),
    )
    return pl.pallas_call(
        _expert_kernel,
        grid_spec=grid_spec,
        out_shape=jax.ShapeDtypeStruct((p, d), F32),
        compiler_params=_params("arbitrary"),
        name="experts",
    )(blk_e, n_used, xb, wg, wu, wd)


def _final_kernel(h_ref, y0_ref, y1_ref, wts_ref, g_ref, o_ref):
    w = wts_ref[...]
    h = h_ref[...] + (y0_ref[...] * w[:, 0:1] + y1_ref[...] * w[:, 1:2])
    o_ref[...] = _rmsnorm(h, g_ref[...])


def _final(h, y0, y1, wts, g, *, tm):
    n, d = h.shape
    row = lambda w: pl.BlockSpec((tm, w), lambda i: (i, 0))
    return pl.pallas_call(
        _final_kernel,
        grid=(n // tm,),
        in_specs=[row(d), row(d), row(d), row(LANES), pl.BlockSpec(g.shape, lambda i: (0, 0))],
        out_specs=row(d),
        out_shape=jax.ShapeDtypeStruct((n, d), F32),
        compiler_params=_params("parallel"),
        name="final_norm",
    )(h, y0, y1, wts, g)


def _dispatch_plan(idx, tm):
    n = idx.shape[0]
    a = n * TOP_K
    flat_e = idx.reshape(a)
    onehot = (flat_e[:, None] == jnp.arange(N_EXPERTS, dtype=jnp.int32)[None, :]).astype(jnp.int32)
    ranks = jnp.cumsum(onehot, axis=0)
    counts = ranks[-1]
    rank = jnp.sum(onehot * ranks, axis=1) - 1
    padded = (counts + tm - 1) // tm * tm
    pend = jnp.cumsum(padded)
    pstart = pend - padded
    dest = pstart[flat_e] + rank
    n_blk = -(-(a + N_EXPERTS * (tm - 1)) // tm)
    blk_e = jnp.minimum(jnp.searchsorted(pend, jnp.arange(n_blk, dtype=jnp.int32) * tm, side="right"),
                        N_EXPERTS - 1).astype(jnp.int32)
    n_used = (pend[-1] // tm).astype(jnp.int32).reshape(1)
    return dest.astype(jnp.int32), blk_e, n_used, n_blk


def _moe(ht, idx, wts, h, wg, wu, wd, g_final, *, tm_e, tm):
    n = ht.shape[0]
    dest, blk_e, n_used, n_blk = _dispatch_plan(idx[:, :TOP_K], tm_e)
    tok = jnp.zeros((n_blk * tm_e,), jnp.int32).at[dest].set(jnp.repeat(jnp.arange(n, dtype=jnp.int32), TOP_K))
    yb = _experts(ht[tok], blk_e, n_used, wg, wu, wd, tm=tm_e)
    pos = dest.reshape(n, TOP_K)
    return _final(h, yb[pos[:, 0]], yb[pos[:, 1]], wts, g_final, tm=tm)


def _layer(x, sb_past, gla_state, mem_kv, wts, *, tm, tq_sb, tt_gla, tq_mem, tm_e):
    b, t, d = x.shape
    n = b * t
    xf = x.reshape(n, d)
    sq, sk, sv, gq, gk, gv, la, gr, mq = _proj(xf, wts["g_mix"], wts["w_proj"], wts["wa2"], wts["ba"], tm)
    r3 = lambda a: a.reshape(b, t, a.shape[-1])
    if sb_past is None:
        o_sb = _sb_attention(r3(sq), r3(sk), r3(sv), tq=tq_sb)
    else:
        o_sb = _sb_attention(r3(sq), r3(sk), r3(sv), sb_past[0], sb_past[1], tq=tq_sb)
    o_gla, st = _gla(r3(gq), r3(gk), r3(gv), r3(la), r3(gr), wts["g_gla"], gla_state, tt=tt_gla)
    o_mem = _mem_attention(r3(mq), mem_kv[0], mem_kv[1], tq=tq_mem)
    h, ht, idx, rw = _merge(xf, o_sb.reshape(n, -1), o_gla.reshape(n, -1), o_mem.reshape(n, -1),
                            wts["g_mix"], wts["w_gt"], wts["w_br"], wts["w_out"], wts["g_ffn"],
                            wts["wr_hi"], wts["wr_lo"], wts["b_r"], tm=tm)
    y = _moe(ht, idx, rw, h, wts["wg"], wts["wu"], wts["wd"], wts["g_final"], tm_e=tm_e, tm=tm)
    return y.reshape(b, t, d), sk, sv, st


def _state_to_t(s):
    b = s.shape[0]
    return s.transpose(0, 3, 1, 2).reshape(b, GLA_DV, GLA_HEADS * GLA_DK)


def _state_from_t(st):
    b = st.shape[0]
    return st.reshape(b, GLA_DV, GLA_HEADS, GLA_DK).transpose(0, 2, 3, 1)


def kernel(x_prompt, x_sample, mem_prompt, cache_sb_k, cache_sb_v, state_gla, cache_mem_k, cache_mem_v,
           norm_mix, w_in, w_gla_a2, b_gla_a, gla_norm, norm_mem, w_mem_kv, w_branch, w_out,
           norm_ffn, w_coarse, b_coarse, w_fine, b_fine, w_e_gate, w_e_up, w_e_down, norm_final):
    assert w_in.shape[0] == 1, "single-layer model"
    d = x_prompt.shape[-1]
    bp, tp, _ = x_prompt.shape
    bs, ts, _ = x_sample.shape
    bw = d // 2

    w = w_in[0]
    offs = [0]
    for s in (bw, bw, bw, 256, 256, bw, GLA_RANK, bw, bw, N_BRANCH * d):
        offs.append(offs[-1] + s)
    col = lambda i: w[:, offs[i]:offs[i + 1]]
    w_proj = jnp.concatenate(
        [col(0), col(1), col(2), col(3), col(4), col(5), col(7), col(8),
         jnp.pad(col(6), ((0, 0), (0, LANES - GLA_RANK)))], axis=1).astype(BF16)
    assert w_proj.shape[1] == _PROJ_W
    row2 = lambda a: a.reshape(1, -1)
    n_route = N_GROUPS + N_EXPERTS
    w_route = jnp.concatenate([w_coarse[0], w_fine[0].transpose(1, 0, 2).reshape(d, N_EXPERTS)], axis=1)
    w_route = jnp.pad(w_route, ((0, 0), (0, LANES - n_route)))
    wr_hi = w_route.astype(BF16)
    wr_lo = (w_route - wr_hi.astype(F32)).astype(BF16)
    b_route = jnp.pad(jnp.concatenate([b_coarse[0], b_fine[0].reshape(-1)]), (0, LANES - n_route)).reshape(1, LANES)
    wts = dict(
        g_mix=row2(norm_mix[0]), w_proj=w_proj,
        wa2=jnp.pad(w_gla_a2[0], ((0, LANES - GLA_RANK), (0, 0))).astype(BF16), ba=row2(b_gla_a[0]),
        g_gla=row2(gla_norm[0]),
        w_gt=col(9).astype(BF16), w_br=w_branch[0].astype(BF16), w_out=w_out[0].astype(BF16),
        g_ffn=row2(norm_ffn[0]), wr_hi=wr_hi, wr_lo=wr_lo, b_r=b_route,
        wg=w_e_gate[0].astype(BF16), wu=w_e_up[0].astype(BF16), wd=w_e_down[0].astype(BF16),
        g_final=row2(norm_final),
    )

    m = mem_prompt.shape[1]
    mk, mv = _memkv(mem_prompt.reshape(bp * m, d), row2(norm_mem[0]), w_mem_kv[0].astype(BF16), tm=min(512, bp * m))
    mk3, mv3 = mk.reshape(bp, m, bw), mv.reshape(bp, m, bw)
    st0 = jnp.zeros((bp, GLA_DV, GLA_HEADS * GLA_DK), F32)
    y_p, sk_p, sv_p, st_p = _layer(x_prompt, None, st0, (mk3, mv3), wts,
                                   tm=256, tq_sb=min(256, tp), tt_gla=min(256, tp), tq_mem=min(512, tp), tm_e=256)

    past = (cache_sb_k[0].reshape(bs, -1, bw), cache_sb_v[0].reshape(bs, -1, bw))
    mem_s = (cache_mem_k[0].reshape(bs, -1, bw), cache_mem_v[0].reshape(bs, -1, bw))
    y_s, sk_s, sv_s, st_s = _layer(x_sample, past, _state_to_t(state_gla[0]), mem_s, wts,
                                   tm=min(256, bs * ts), tq_sb=ts, tt_gla=ts, tq_mem=ts, tm_e=256)

    hd = lambda a, bb, tt, hh: a.reshape(1, bb, tt, hh, -1)
    return (y_p, y_s,
            hd(sk_p, bp, tp, SB_HEADS), hd(sv_p, bp, tp, SB_HEADS),
            _state_from_t(st_p)[None],
            hd(mk, bp, m, MEM_HEADS), hd(mv, bp, m, MEM_HEADS),
            hd(sk_s, bs, ts, SB_HEADS), hd(sv_s, bs, ts, SB_HEADS),
            _state_from_t(st_s)[None])
```

```python
import functools

import jax
import jax.numpy as jnp
from jax import lax
from jax.experimental import pallas as pl
from jax.experimental.pallas import tpu as pltpu

F32 = jnp.float32
BF16 = jnp.bfloat16
EPS = 1e-6

SB_HEADS = 8
SB_DIM = 64
GLA_HEADS = 4
GLA_DK = 64
GLA_DV = 128
GLA_RANK = 16
GLA_TAU = 16.0
GLA_CHUNK = 64
MEM_HEADS = 4
MEM_DIM = 128
N_BRANCH = 3
N_GROUPS = 4
EXPERTS_PER_GROUP = 8
N_EXPERTS = N_GROUPS * EXPERTS_PER_GROUP
TOP_K = 2

LANES = 128
VMEM_LIMIT = 56 * 1024 * 1024
NEG_BIG = -1e30
_LOG_F32_ZERO = -110.0

_NT = (((1,), (1,)), ((), ()))


def _dot(a, b):
    return jnp.dot(a, b, preferred_element_type=F32)


def _dot_nt(a, b):
    return lax.dot_general(a, b, _NT, preferred_element_type=F32)


def _split_bf16(x):
    hi = x.astype(BF16)
    lo = (x - hi.astype(F32)).astype(BF16)
    return hi, lo


def _rmsnorm(x, g):
    return x * lax.rsqrt(jnp.mean(x * x, axis=-1, keepdims=True) + EPS) * g


def _log_sigmoid(x):
    return jnp.minimum(x, 0.0) - jnp.log1p(jnp.exp(-jnp.abs(x)))


def _params(*sem):
    return pltpu.CompilerParams(dimension_semantics=sem, vmem_limit_bytes=VMEM_LIMIT)


_C_SQ, _C_SK, _C_SV = 0, 512, 1024
_C_GQ, _C_GK, _C_GV = 1536, 1792, 2048
_C_GR, _C_MQ, _C_GLR = 2560, 3072, 3584
_PROJ_W = 3712


def _proj_kernel(x_ref, g_ref, w_ref, wa2_ref, ba_ref,
                 sq_ref, sk_ref, sv_ref, gq_ref, gk_ref, gv_ref, la_ref, gr_ref, mq_ref):
    xn = _rmsnorm(x_ref[...], g_ref[...]).astype(BF16)

    def seg(start, width):
        return _dot(xn, w_ref[:, start:start + width])

    sq_ref[...] = (seg(_C_SQ, 512) * SB_DIM ** -0.5).astype(BF16)
    sk_ref[...] = seg(_C_SK, 512)
    sv_ref[...] = seg(_C_SV, 512)
    gq_ref[...] = seg(_C_GQ, 256) * GLA_DK ** -0.5
    gk_ref[...] = seg(_C_GK, 256)
    gv_ref[...] = seg(_C_GV, 512).astype(BF16)
    gr_ref[...] = seg(_C_GR, 512)
    mq_ref[...] = seg(_C_MQ, 512).astype(BF16)
    glr = seg(_C_GLR, LANES).astype(BF16)
    la_ref[...] = _log_sigmoid(_dot(glr, wa2_ref[...]) + ba_ref[...]) * (1.0 / GLA_TAU)


def _proj(x, g, w_proj, wa2, ba, tm):
    n, d = x.shape
    row = lambda w: pl.BlockSpec((tm, w), lambda i: (i, 0))
    full = lambda a: pl.BlockSpec(a.shape, lambda i: (0,) * a.ndim)
    widths = (512, 512, 512, 256, 256, 512, 256, 512, 512)
    dtypes = (BF16, F32, F32, F32, F32, BF16, F32, F32, BF16)
    return pl.pallas_call(
        _proj_kernel,
        grid=(n // tm,),
        in_specs=[row(d), full(g), full(w_proj), full(wa2), full(ba)],
        out_specs=[row(w) for w in widths],
        out_shape=[jax.ShapeDtypeStruct((n, w), dt) for w, dt in zip(widths, dtypes)],
        compiler_params=_params("parallel"),
        name="proj",
    )(x, g, w_proj, wa2, ba)


def _memkv_kernel(x_ref, g_ref, w_ref, mk_ref, mv_ref):
    xn = _rmsnorm(x_ref[...], g_ref[...]).astype(BF16)
    half = mk_ref.shape[-1]
    mk_ref[...] = _dot(xn, w_ref[:, :half])
    mv_ref[...] = _dot(xn, w_ref[:, half:])


def _memkv(x, g, w, tm):
    n, d = x.shape
    half = w.shape[1] // 2
    row = lambda wd: pl.BlockSpec((tm, wd), lambda i: (i, 0))
    full = lambda a: pl.BlockSpec(a.shape, lambda i: (0,) * a.ndim)
    return pl.pallas_call(
        _memkv_kernel,
        grid=(n // tm,),
        in_specs=[row(d), full(g), full(w)],
        out_specs=[row(half), row(half)],
        out_shape=[jax.ShapeDtypeStruct((n, half), F32)] * 2,
        compiler_params=_params("parallel"),
        name="memkv",
    )(x, g, w)


def _sb_kernel(*refs, tq, past_len, tk_past):
    if past_len:
        q_ref, k_ref, v_ref, pk_ref, pv_ref, o_ref, acc_ref, car_ref = refs
    else:
        q_ref, k_ref, v_ref, o_ref, acc_ref, car_ref = refs
    qi = pl.program_id(2)
    q = q_ref[...]
    lane = lax.broadcasted_iota(jnp.int32, (tq, LANES), 1)
    q_heads = (jnp.where(lane < SB_DIM, q, jnp.zeros_like(q)),
               jnp.where(lane >= SB_DIM, q, jnp.zeros_like(q)))

    acc_ref[...] = jnp.zeros_like(acc_ref)
    car_ref[...] = jnp.zeros_like(car_ref)

    def suffix_matrix(tk):
        j = lax.broadcasted_iota(jnp.int32, (2 * tk, tk), 0)
        s = lax.broadcasted_iota(jnp.int32, (2 * tk, tk), 1)
        return ((j > s) & ((j < tk) | (j > s + tk))).astype(BF16)

    def block(kb, vb, mask):
        tk = kb.shape[0]
        u2 = suffix_matrix(tk)
        top = None
        for h in range(2):
            z = _dot_nt(q_heads[h], kb)
            lz = jnp.minimum(z, 0.0) - jnp.log(1.0 + jnp.exp(-jnp.abs(z)))
            l1 = lz - z
            if mask is not None:
                l1 = jnp.where(mask, l1, 0.0)
            hi, lo = _split_bf16(l1)
            cum = _dot(jnp.concatenate([hi, lo], axis=1), u2)
            carry = car_ref[h]
            p = jnp.exp(lz + cum + carry)
            if mask is not None:
                p = jnp.where(mask, p, 0.0)
            acc_ref[h] += _dot(p.astype(BF16), vb)
            carry = carry + cum[:, :1] + l1[:, :1]
            car_ref[h] = carry
            m = jnp.max(carry)
            top = m if top is None else jnp.maximum(top, m)
        return top > _LOG_F32_ZERO

    def fold_descending(kr, vr, n_blocks, tk, alive):
        def cond(state):
            i, live = state
            return (i < n_blocks) & live

        def body(state):
            i, _ = state
            s0 = pl.multiple_of((n_blocks - 1 - i) * tk, tk)
            live = block(kr[pl.ds(s0, tk), :].astype(BF16), vr[pl.ds(s0, tk), :].astype(BF16), None)
            return i + 1, live

        return lax.while_loop(cond, body, (jnp.int32(0), alive))[1]

    start = pl.multiple_of(qi * tq, tq)
    r = lax.broadcasted_iota(jnp.int32, (tq, tq), 0)
    c = lax.broadcasted_iota(jnp.int32, (tq, tq), 1)
    alive = block(k_ref[pl.ds(start, tq), :].astype(BF16), v_ref[pl.ds(start, tq), :].astype(BF16), c < r)
    alive = fold_descending(k_ref, v_ref, qi, tq, alive)
    if past_len:
        fold_descending(pk_ref, pv_ref, past_len // tk_past, tk_past, alive)

    o_ref[...] = jnp.where(lane < SB_DIM, acc_ref[0], acc_ref[1]).astype(o_ref.dtype)


def _sb_attention(q, k, v, past_k=None, past_v=None, *, tq, tk_past=256):
    b, t, w = q.shape
    pairs = w // LANES
    past_len = 0 if past_k is None else past_k.shape[1]
    seq = lambda n: pl.BlockSpec((None, n, LANES), lambda bi, p, qi: (bi, 0, p))
    tile = pl.BlockSpec((None, tq, LANES), lambda bi, p, qi: (bi, qi, p))
    in_specs = [tile, seq(t), seq(t)]
    args = [q, k, v]
    if past_len:
        in_specs += [seq(past_len), seq(past_len)]
        args += [past_k, past_v]
    return pl.pallas_call(
        functools.partial(_sb_kernel, tq=tq, past_len=past_len, tk_past=tk_past),
        grid=(b, pairs, t // tq),
        in_specs=in_specs,
        out_specs=tile,
        out_shape=jax.ShapeDtypeStruct((b, t, w), BF16),
        scratch_shapes=[pltpu.VMEM((2, tq, LANES), F32), pltpu.VMEM((2, tq, 1), F32)],
        compiler_params=_params("parallel", "parallel", "arbitrary"),
        name="sb_attention",
    )(*args)


def _gla_kernel(gq_ref, gk_ref, gv_ref, la_ref, gr_ref, gn_ref, s0_ref, o_ref, st_ref, st_scr, *, n_chunks):
    c_len = GLA_CHUNK
    kw = GLA_HEADS * GLA_DK
    t = pl.program_id(1)

    @pl.when(t == 0)
    def _():
        st_scr[...] = s0_ref[...]

    r = lax.broadcasted_iota(jnp.int32, (c_len, c_len), 0)
    c = lax.broadcasted_iota(jnp.int32, (c_len, c_len), 1)
    causal = r >= c
    tril = causal.astype(BF16)
    eye = (lax.broadcasted_iota(jnp.int32, (GLA_DV, GLA_DV), 0)
           == lax.broadcasted_iota(jnp.int32, (GLA_DV, GLA_DV), 1)).astype(BF16)
    lane_q = lax.broadcasted_iota(jnp.int32, (c_len, kw), 1)
    lane_s = lax.broadcasted_iota(jnp.int32, (GLA_DV, kw), 1)

    for ci in range(n_chunks):
        rows = slice(ci * c_len, (ci + 1) * c_len)
        la_hi, la_lo = _split_bf16(la_ref[rows, :])
        b = _dot(tril, la_hi) + _dot(tril, la_lo)
        b_last = b[c_len - 1:c_len, :]
        gk = gk_ref[rows, :]
        qd = gq_ref[rows, :] * jnp.exp(b)
        kd = (gk * jnp.exp(-b)).astype(BF16)
        ke = (gk * jnp.exp(b_last - b)).astype(BF16)
        st = st_scr[...]
        st_b = st.astype(BF16)
        st_new = st * jnp.exp(b_last)
        v = gv_ref[rows, :]
        for h in range(GLA_HEADS):
            in_head = (lane_q >= h * GLA_DK) & (lane_q < (h + 1) * GLA_DK)
            qh = jnp.where(in_head, qd, 0.0).astype(BF16)
            att = jnp.where(causal, _dot_nt(qh, kd), 0.0).astype(BF16)
            cols = slice(h * GLA_DV, (h + 1) * GLA_DV)
            vh = v[:, cols]
            o = _dot(att, vh) + _dot_nt(qh, st_b)
            v_t = _dot_nt(eye, vh).astype(BF16)
            in_head_s = (lane_s >= h * GLA_DK) & (lane_s < (h + 1) * GLA_DK)
            st_new = st_new + jnp.where(in_head_s, _dot(v_t, ke), 0.0)
            on = _rmsnorm(o, gn_ref[:, cols])
            gr = gr_ref[rows, cols]
            o_ref[rows, cols] = (on * (gr * jax.nn.sigmoid(gr))).astype(o_ref.dtype)
        st_scr[...] = st_new

    @pl.when(t == pl.num_programs(1) - 1)
    def _():
        st_ref[...] = st_scr[...]


def _gla(gq, gk, gv, la, gr, gn, st0, *, tt):
    b, t, kw = gq.shape
    vw = gv.shape[-1]
    tok = lambda w: pl.BlockSpec((None, tt, w), lambda bi, ti: (bi, ti, 0))
    state = pl.BlockSpec((None, GLA_DV, kw), lambda bi, ti: (bi, 0, 0))
    return pl.pallas_call(
        functools.partial(_gla_kernel, n_chunks=tt // GLA_CHUNK),
        grid=(b, t // tt),
        in_specs=[tok(kw), tok(kw), tok(vw), tok(kw), tok(vw),
                  pl.BlockSpec(gn.shape, lambda bi, ti: (0, 0)), state],
        out_specs=[tok(vw), state],
        out_shape=[jax.ShapeDtypeStruct((b, t, vw), BF16), jax.ShapeDtypeStruct((b, GLA_DV, kw), F32)],
        scratch_shapes=[pltpu.VMEM((GLA_DV, kw), F32)],
        compiler_params=_params("parallel", "arbitrary"),
        name="gla",
    )(gq, gk, gv, la, gr, gn, st0)


def _mem_kernel(q_ref, mk_ref, mv_ref, o_ref):
    for h in range(MEM_HEADS):
        cols = slice(h * MEM_DIM, (h + 1) * MEM_DIM)
        s = _dot_nt(q_ref[:, cols], mk_ref[:, cols].astype(BF16)) * MEM_DIM ** -0.5
        e = jnp.exp(s - jnp.max(s, axis=-1, keepdims=True))
        p = e / jnp.sum(e, axis=-1, keepdims=True)
        o_ref[:, cols] = _dot(p.astype(BF16), mv_ref[:, cols].astype(BF16)).astype(o_ref.dtype)


def _mem_attention(q, mk, mv, *, tq):
    b, t, w = q.shape
    m = mk.shape[1]
    tile = pl.BlockSpec((None, tq, w), lambda bi, qi: (bi, qi, 0))
    mem = pl.BlockSpec((None, m, w), lambda bi, qi: (bi, 0, 0))
    return pl.pallas_call(
        _mem_kernel,
        grid=(b, t // tq),
        in_specs=[tile, mem, mem],
        out_specs=tile,
        out_shape=jax.ShapeDtypeStruct((b, t, w), BF16),
        compiler_params=_params("parallel", "arbitrary"),
        name="mem_attention",
    )(q, mk, mv)


def _merge_kernel(x_ref, osb_ref, ogla_ref, omem_ref, gmix_ref, wgt_ref, wbr_ref, wout_ref,
                  gffn_ref, wr_hi_ref, wr_lo_ref, br_ref, h_ref, ht_ref, idx_ref, wts_ref):
    x = x_ref[...]
    d = x.shape[-1]
    xn = _rmsnorm(x, gmix_ref[...]).astype(BF16)
    mixed = None
    for n, o_ref in enumerate((osb_ref, ogla_ref, omem_ref)):
        gate = jax.nn.sigmoid(_dot(xn, wgt_ref[:, n * d:(n + 1) * d]))
        term = gate * _dot(o_ref[...], wbr_ref[n])
        mixed = term if mixed is None else mixed + term
    h = x + _dot(mixed.astype(BF16), wout_ref[...])
    h_ref[...] = h
    hn = _rmsnorm(h, gffn_ref[...])
    ht_ref[...] = hn.astype(BF16)

    hn_hi, hn_lo = _split_bf16(hn)
    logits = (_dot(hn_hi, wr_hi_ref[...]) + _dot(hn_lo, wr_hi_ref[...])
              + _dot(hn_hi, wr_lo_ref[...]) + br_ref[...])
    lane = lax.broadcasted_iota(jnp.int32, logits.shape, 1)
    rmax = lambda a: jnp.max(a, axis=-1, keepdims=True)
    rmin = lambda a: jnp.min(a, axis=-1, keepdims=True)
    rsum = lambda a: jnp.sum(a, axis=-1, keepdims=True)

    lc = jnp.where(lane < N_GROUPS, logits, NEG_BIG)
    mc = rmax(lc)
    grp = rmin(jnp.where(lc == mc, lane, LANES))
    p_grp = 1.0 / rsum(jnp.exp(lc - mc))

    lo = N_GROUPS + grp * EXPERTS_PER_GROUP
    in_grp = (lane >= lo) & (lane < lo + EXPERTS_PER_GROUP)
    lf = jnp.where(in_grp, logits, NEG_BIG)
    ef = jnp.exp(lf - rmax(lf))
    pf = jnp.where(in_grp, ef / rsum(ef), -1.0)
    v1 = rmax(pf)
    i1 = rmin(jnp.where(pf == v1, lane, LANES))
    pf2 = jnp.where(lane == i1, -1.0, pf)
    v2 = rmax(pf2)
    i2 = rmin(jnp.where(pf2 == v2, lane, LANES))
    tot = v1 + v2
    idx_ref[...] = jnp.where(lane == 0, i1 - N_GROUPS, jnp.where(lane == 1, i2 - N_GROUPS, 0))
    wts_ref[...] = jnp.where(lane == 0, p_grp * (v1 / tot), jnp.where(lane == 1, p_grp * (v2 / tot), 0.0))


def _merge(x, osb, ogla, omem, gmix, wgt, wbr, wout, gffn, wr_hi, wr_lo, br, *, tm):
    n, d = x.shape
    bw = osb.shape[1]
    row = lambda w: pl.BlockSpec((tm, w), lambda i: (i, 0))
    full = lambda a: pl.BlockSpec(a.shape, lambda i: (0,) * a.ndim)
    return pl.pallas_call(
        _merge_kernel,
        grid=(n // tm,),
        in_specs=[row(d), row(bw), row(bw), row(bw), full(gmix), full(wgt), full(wbr), full(wout),
                  full(gffn), full(wr_hi), full(wr_lo), full(br)],
        out_specs=[row(d), row(d), row(LANES), row(LANES)],
        out_shape=[jax.ShapeDtypeStruct((n, d), F32), jax.ShapeDtypeStruct((n, d), BF16),
                   jax.ShapeDtypeStruct((n, LANES), jnp.int32), jax.ShapeDtypeStruct((n, LANES), F32)],
        compiler_params=_params("parallel"),
        name="merge_route",
    )(x, osb, ogla, omem, gmix, wgt, wbr, wout, gffn, wr_hi, wr_lo, br)


def _expert_kernel(blk_e_ref, n_used_ref, x_ref, wg_ref, wu_ref, wd_ref, y_ref):
    i = pl.program_id(0)

    @pl.when(i < n_used_ref[0])
    def _():
        x = x_ref[...]
        g = _dot(x, wg_ref[...])
        u = _dot(x, wu_ref[...])
        a = (g * jax.nn.sigmoid(g) * u).astype(BF16)
        y_ref[...] = _dot(a, wd_ref[...])

    @pl.when(i >= n_used_ref[0])
    def _():
        y_ref[...] = jnp.zeros_like(y_ref)


def _experts(xb, blk_e, n_used, wg, wu, wd, *, tm):
    p, d = xb.shape
    de = wg.shape[-1]
    grid_spec = pltpu.PrefetchScalarGridSpec(
        num_scalar_prefetch=2,
        grid=(p // tm,),
        in_specs=[pl.BlockSpec((tm, d), lambda i, be, nu: (i, 0)),
                  pl.BlockSpec((None, d, de), lambda i, be, nu: (be[i], 0, 0)),
                  pl.BlockSpec((None, d, de), lambda i, be, nu: (be[i], 0, 0)),
                  pl.BlockSpec((None, de, d), lambda i, be, nu: (be[i], 0, 0))],
        out_specs=pl.BlockSpec((tm, d), lambda i, be, nu: (i, 0)),
    )
    return pl.pallas_call(
        _expert_kernel,
        grid_spec=grid_spec,
        out_shape=jax.ShapeDtypeStruct((p, d), F32),
        compiler_params=_params("arbitrary"),
        name="experts",
    )(blk_e, n_used, xb, wg, wu, wd)


def _final_kernel(h_ref, y0_ref, y1_ref, wts_ref, g_ref, o_ref):
    w = wts_ref[...]
    h = h_ref[...] + (y0_ref[...] * w[:, 0:1] + y1_ref[...] * w[:, 1:2])
    o_ref[...] = _rmsnorm(h, g_ref[...])


def _final(h, y0, y1, wts, g, *, tm):
    n, d = h.shape
    row = lambda w: pl.BlockSpec((tm, w), lambda i: (i, 0))
    return pl.pallas_call(
        _final_kernel,
        grid=(n // tm,),
        in_specs=[row(d), row(d), row(d), row(LANES), pl.BlockSpec(g.shape, lambda i: (0, 0))],
        out_specs=row(d),
        out_shape=jax.ShapeDtypeStruct((n, d), F32),
        compiler_params=_params("parallel"),
        name="final_norm",
    )(h, y0, y1, wts, g)


def _dispatch_plan(idx, tm):
    n = idx.shape[0]
    a = n * TOP_K
    flat_e = idx.reshape(a)
    onehot = (flat_e[:, None] == jnp.arange(N_EXPERTS, dtype=jnp.int32)[None, :]).astype(jnp.int32)
    ranks = jnp.cumsum(onehot, axis=0)
    counts = ranks[-1]
    rank = jnp.sum(onehot * ranks, axis=1) - 1
    padded = (counts + tm - 1) // tm * tm
    pend = jnp.cumsum(padded)
    pstart = pend - padded
    dest = pstart[flat_e] + rank
    n_blk = -(-(a + N_EXPERTS * (tm - 1)) // tm)
    blk_start = jnp.arange(n_blk, dtype=jnp.int32) * tm
    blk_e = jnp.minimum(jnp.sum((pend[None, :] <= blk_start[:, None]).astype(jnp.int32), axis=1),
                        N_EXPERTS - 1).astype(jnp.int32)
    n_used = (pend[-1] // tm).astype(jnp.int32).reshape(1)
    return dest.astype(jnp.int32), blk_e, n_used, n_blk


def _moe(ht, idx, wts, h, wg, wu, wd, g_final, *, tm_e, tm):
    n = ht.shape[0]
    dest, blk_e, n_used, n_blk = _dispatch_plan(idx[:, :TOP_K], tm_e)
    tok = jnp.zeros((n_blk * tm_e,), jnp.int32).at[dest].set(jnp.repeat(jnp.arange(n, dtype=jnp.int32), TOP_K))
    yb = _experts(ht[tok], blk_e, n_used, wg, wu, wd, tm=tm_e)
    pos = dest.reshape(n, TOP_K)
    return _final(h, yb[pos[:, 0]], yb[pos[:, 1]], wts, g_final, tm=tm)


def _layer(x, sb_past, gla_state, mem_kv, wts, *, tm, tq_sb, tt_gla, tq_mem, tm_e):
    b, t, d = x.shape
    n = b * t
    xf = x.reshape(n, d)
    sq, sk, sv, gq, gk, gv, la, gr, mq = _proj(xf, wts["g_mix"], wts["w_proj"], wts["wa2"], wts["ba"], tm)
    r3 = lambda a: a.reshape(b, t, a.shape[-1])
    if sb_past is None:
        o_sb = _sb_attention(r3(sq), r3(sk), r3(sv), tq=tq_sb)
    else:
        o_sb = _sb_attention(r3(sq), r3(sk), r3(sv), sb_past[0], sb_past[1], tq=tq_sb)
    o_gla, st = _gla(r3(gq), r3(gk), r3(gv), r3(la), r3(gr), wts["g_gla"], gla_state, tt=tt_gla)
    o_mem = _mem_attention(r3(mq), mem_kv[0], mem_kv[1], tq=tq_mem)
    h, ht, idx, rw = _merge(xf, o_sb.reshape(n, -1), o_gla.reshape(n, -1), o_mem.reshape(n, -1),
                            wts["g_mix"], wts["w_gt"], wts["w_br"], wts["w_out"], wts["g_ffn"],
                            wts["wr_hi"], wts["wr_lo"], wts["b_r"], tm=tm)
    y = _moe(ht, idx, rw, h, wts["wg"], wts["wu"], wts["wd"], wts["g_final"], tm_e=tm_e, tm=tm)
    return y.reshape(b, t, d), sk, sv, st


def _state_to_t(s):
    b = s.shape[0]
    return s.transpose(0, 3, 1, 2).reshape(b, GLA_DV, GLA_HEADS * GLA_DK)


def _state_from_t(st):
    b = st.shape[0]
    return st.reshape(b, GLA_DV, GLA_HEADS, GLA_DK).transpose(0, 2, 3, 1)


def kernel(x_prompt, x_sample, mem_prompt, cache_sb_k, cache_sb_v, state_gla, cache_mem_k, cache_mem_v,
           norm_mix, w_in, w_gla_a2, b_gla_a, gla_norm, norm_mem, w_mem_kv, w_branch, w_out,
           norm_ffn, w_coarse, b_coarse, w_fine, b_fine, w_e_gate, w_e_up, w_e_down, norm_final):
    assert w_in.shape[0] == 1, "single-layer model"
    d = x_prompt.shape[-1]
    bp, tp, _ = x_prompt.shape
    bs, ts, _ = x_sample.shape
    bw = d // 2

    w = w_in[0]
    offs = [0]
    for s in (bw, bw, bw, 256, 256, bw, GLA_RANK, bw, bw, N_BRANCH * d):
        offs.append(offs[-1] + s)
    col = lambda i: w[:, offs[i]:offs[i + 1]]
    w_proj = jnp.concatenate(
        [col(0), col(1), col(2), col(3), col(4), col(5), col(7), col(8),
         jnp.pad(col(6), ((0, 0), (0, LANES - GLA_RANK)))], axis=1).astype(BF16)
    assert w_proj.shape[1] == _PROJ_W
    row2 = lambda a: a.reshape(1, -1)
    n_route = N_GROUPS + N_EXPERTS
    w_route = jnp.concatenate([w_coarse[0], w_fine[0].transpose(1, 0, 2).reshape(d, N_EXPERTS)], axis=1)
    w_route = jnp.pad(w_route, ((0, 0), (0, LANES - n_route)))
    wr_hi = w_route.astype(BF16)
    wr_lo = (w_route - wr_hi.astype(F32)).astype(BF16)
    b_route = jnp.pad(jnp.concatenate([b_coarse[0], b_fine[0].reshape(-1)]), (0, LANES - n_route)).reshape(1, LANES)
    wts = dict(
        g_mix=row2(norm_mix[0]), w_proj=w_proj,
        wa2=jnp.pad(w_gla_a2[0], ((0, LANES - GLA_RANK), (0, 0))).astype(BF16), ba=row2(b_gla_a[0]),
        g_gla=row2(gla_norm[0]),
        w_gt=col(9).astype(BF16), w_br=w_branch[0].astype(BF16), w_out=w_out[0].astype(BF16),
        g_ffn=row2(norm_ffn[0]), wr_hi=wr_hi, wr_lo=wr_lo, b_r=b_route,
        wg=w_e_gate[0].astype(BF16), wu=w_e_up[0].astype(BF16), wd=w_e_down[0].astype(BF16),
        g_final=row2(norm_final),
    )

    m = mem_prompt.shape[1]
    mk, mv = _memkv(mem_prompt.reshape(bp * m, d), row2(norm_mem[0]), w_mem_kv[0].astype(BF16), tm=min(512, bp * m))
    mk3, mv3 = mk.reshape(bp, m, bw), mv.reshape(bp, m, bw)
    st0 = jnp.zeros((bp, GLA_DV, GLA_HEADS * GLA_DK), F32)
    y_p, sk_p, sv_p, st_p = _layer(x_prompt, None, st0, (mk3, mv3), wts,
                                   tm=256, tq_sb=min(256, tp), tt_gla=min(256, tp), tq_mem=min(512, tp), tm_e=256)

    past = (cache_sb_k[0].reshape(bs, -1, bw), cache_sb_v[0].reshape(bs, -1, bw))
    mem_s = (cache_mem_k[0].reshape(bs, -1, bw), cache_mem_v[0].reshape(bs, -1, bw))
    y_s, sk_s, sv_s, st_s = _layer(x_sample, past, _state_to_t(state_gla[0]), mem_s, wts,
                                   tm=min(256, bs * ts), tq_sb=ts, tt_gla=ts, tq_mem=ts, tm_e=256)

    hd = lambda a, bb, tt, hh: a.reshape(1, bb, tt, hh, -1)
    return (y_p, y_s,
            hd(sk_p, bp, tp, SB_HEADS), hd(sv_p, bp, tp, SB_HEADS),
            _state_from_t(st_p)[None],
            hd(mk, bp, m, MEM_HEADS), hd(mv, bp, m, MEM_HEADS),
            hd(sk_s, bs, ts, SB_HEADS), hd(sv_s, bs, ts, SB_HEADS),
            _state_from_t(st_s)[None])
```

```python
import functools

import jax
import jax.numpy as jnp
from jax import lax
from jax.experimental import pallas as pl
from jax.experimental.pallas import tpu as pltpu

F32 = jnp.float32
BF16 = jnp.bfloat16
EPS = 1e-6

SB_HEADS = 8
SB_DIM = 64
GLA_HEADS = 4
GLA_DK = 64
GLA_DV = 128
GLA_RANK = 16
GLA_TAU = 16.0
GLA_CHUNK = 64
MEM_HEADS = 4
MEM_DIM = 128
N_BRANCH = 3
N_GROUPS = 4
EXPERTS_PER_GROUP = 8
N_EXPERTS = N_GROUPS * EXPERTS_PER_GROUP
TOP_K = 2

LANES = 128
VMEM_LIMIT = 56 * 1024 * 1024
NEG_BIG = -1e30
_LOG_F32_ZERO = -110.0

_NT = (((1,), (1,)), ((), ()))


def _dot(a, b):
    return jnp.dot(a, b, preferred_element_type=F32)


def _dot_nt(a, b):
    return lax.dot_general(a, b, _NT, preferred_element_type=F32)


def _split_bf16(x):
    hi = x.astype(BF16)
    lo = (x - hi.astype(F32)).astype(BF16)
    return hi, lo


def _rmsnorm(x, g):
    return x * lax.rsqrt(jnp.mean(x * x, axis=-1, keepdims=True) + EPS) * g


def _log_sigmoid(x):
    return jnp.minimum(x, 0.0) - jnp.log1p(jnp.exp(-jnp.abs(x)))


def _resident(a):
    return pl.BlockSpec(a.shape, lambda *_: (0,) * a.ndim, pipeline_mode=pl.Buffered(1))


def _params(*sem):
    return pltpu.CompilerParams(dimension_semantics=sem, vmem_limit_bytes=VMEM_LIMIT)


_C_SQ, _C_SK, _C_SV = 0, 512, 1024
_C_GQ, _C_GK, _C_GV = 1536, 1792, 2048
_C_GR, _C_MQ, _C_GLR = 2560, 3072, 3584
_PROJ_W = 3712


def _proj_kernel(x_ref, g_ref, w_ref, wa2_ref, ba_ref,
                 sq_ref, skb_ref, svb_ref, sk_ref, sv_ref, gq_ref, gk_ref, gv_ref, la_ref, gr_ref, mq_ref):
    xn = _rmsnorm(x_ref[...], g_ref[...]).astype(BF16)

    def seg(start, width):
        return _dot(xn, w_ref[:, start:start + width])

    sq_ref[...] = (seg(_C_SQ, 512) * SB_DIM ** -0.5).astype(BF16)
    for c0, bf_ref, heads_ref in ((_C_SK, skb_ref, sk_ref), (_C_SV, svb_ref, sv_ref)):
        kv = seg(c0, 512)
        bf_ref[...] = kv.astype(BF16)
        for h in range(SB_HEADS):
            heads_ref[:, h, :] = kv[:, h * SB_DIM:(h + 1) * SB_DIM]
    gq_ref[...] = seg(_C_GQ, 256) * GLA_DK ** -0.5
    gk_ref[...] = seg(_C_GK, 256)
    gv_ref[...] = seg(_C_GV, 512).astype(BF16)
    gr_ref[...] = seg(_C_GR, 512)
    mq_ref[...] = seg(_C_MQ, 512).astype(BF16)
    glr = seg(_C_GLR, LANES).astype(BF16)
    la_ref[...] = _log_sigmoid(_dot(glr, wa2_ref[...]) + ba_ref[...]) * (1.0 / GLA_TAU)


def _proj(x, g, w_proj, wa2, ba, tm):
    n, d = x.shape
    row = lambda w: pl.BlockSpec((tm, w), lambda i: (i, 0))
    heads = pl.BlockSpec((tm, SB_HEADS, SB_DIM), lambda i: (i, 0, 0))
    flat = lambda w, dt: (row(w), jax.ShapeDtypeStruct((n, w), dt))
    head_major = (heads, jax.ShapeDtypeStruct((n, SB_HEADS, SB_DIM), F32))
    outs = [flat(512, BF16), flat(512, BF16), flat(512, BF16), head_major, head_major,
            flat(256, F32), flat(256, F32), flat(512, BF16), flat(256, F32), flat(512, F32), flat(512, BF16)]
    return pl.pallas_call(
        _proj_kernel,
        grid=(n // tm,),
        in_specs=[row(d), _resident(g), _resident(w_proj), _resident(wa2), _resident(ba)],
        out_specs=[o[0] for o in outs],
        out_shape=[o[1] for o in outs],
        compiler_params=_params("parallel"),
        name="proj",
    )(x, g, w_proj, wa2, ba)


def _memkv_kernel(x_ref, g_ref, w_ref, mk_ref, mv_ref):
    xn = _rmsnorm(x_ref[...], g_ref[...]).astype(BF16)
    half = mk_ref.shape[-1]
    mk_ref[...] = _dot(xn, w_ref[:, :half])
    mv_ref[...] = _dot(xn, w_ref[:, half:])


def _memkv(x, g, w, tm):
    n, d = x.shape
    half = w.shape[1] // 2
    row = lambda wd: pl.BlockSpec((tm, wd), lambda i: (i, 0))
    return pl.pallas_call(
        _memkv_kernel,
        grid=(n // tm,),
        in_specs=[row(d), _resident(g), _resident(w)],
        out_specs=[row(half), row(half)],
        out_shape=[jax.ShapeDtypeStruct((n, half), F32)] * 2,
        compiler_params=_params("parallel"),
        name="memkv",
    )(x, g, w)


def _suffix_matrix(tk):
    j = lax.broadcasted_iota(jnp.int32, (2 * tk, tk), 0)
    s = lax.broadcasted_iota(jnp.int32, (2 * tk, tk), 1)
    return ((j > s) & ((j < tk) | (j > s + tk))).astype(BF16)


def _sb_weights(z, carry, mask):
    lz = jnp.minimum(z, 0.0) - jnp.log(1.0 + jnp.exp(-jnp.abs(z)))
    l1 = lz - z
    if mask is not None:
        l1 = jnp.where(mask, l1, 0.0)
    hi, lo = _split_bf16(l1)
    cum = _dot(jnp.concatenate([hi, lo], axis=1), _suffix_matrix(z.shape[1]))
    p = jnp.exp(lz + cum + carry)
    if mask is not None:
        p = jnp.where(mask, p, 0.0)
    return p.astype(BF16), carry + cum[:, :1] + l1[:, :1]


def _fold_descending(n_blocks, alive, fold_block):
    def cond(state):
        i, live = state
        return (i < n_blocks) & live

    def body(state):
        i, _ = state
        return i + 1, fold_block(n_blocks - 1 - i)

    return lax.while_loop(cond, body, (jnp.int32(0), alive))[1]


def _sb_prompt_kernel(q_ref, k_ref, v_ref, o_ref, acc_ref, car_ref, *, tq):
    qi = pl.program_id(2)
    q = q_ref[...]
    lane = lax.broadcasted_iota(jnp.int32, (tq, LANES), 1)
    q_heads = (jnp.where(lane < SB_DIM, q, jnp.zeros_like(q)),
               jnp.where(lane >= SB_DIM, q, jnp.zeros_like(q)))
    acc_ref[...] = jnp.zeros_like(acc_ref)
    car_ref[...] = jnp.zeros_like(car_ref)

    def block(s0, mask):
        kb = k_ref[pl.ds(s0, tq), :]
        vb = v_ref[pl.ds(s0, tq), :]
        top = None
        for h in range(2):
            p, carry = _sb_weights(_dot_nt(q_heads[h], kb), car_ref[h], mask)
            acc_ref[h] += _dot(p, vb)
            car_ref[h] = carry
            m = jnp.max(carry)
            top = m if top is None else jnp.maximum(top, m)
        return top > _LOG_F32_ZERO

    r = lax.broadcasted_iota(jnp.int32, (tq, tq), 0)
    c = lax.broadcasted_iota(jnp.int32, (tq, tq), 1)
    alive = block(pl.multiple_of(qi * tq, tq), c < r)
    _fold_descending(qi, alive, lambda kb: block(pl.multiple_of(kb * tq, tq), None))
    o_ref[...] = jnp.where(lane < SB_DIM, acc_ref[0], acc_ref[1]).astype(o_ref.dtype)


def _sb_prompt(q, k, v, *, tq):
    b, t, w = q.shape
    seq = pl.BlockSpec((None, t, LANES), lambda bi, p, qi: (bi, 0, p))
    tile = pl.BlockSpec((None, tq, LANES), lambda bi, p, qi: (bi, qi, p))
    return pl.pallas_call(
        functools.partial(_sb_prompt_kernel, tq=tq),
        grid=(b, w // LANES, t // tq),
        in_specs=[tile, seq, seq],
        out_specs=tile,
        out_shape=jax.ShapeDtypeStruct((b, t, w), BF16),
        scratch_shapes=[pltpu.VMEM((2, tq, LANES), F32), pltpu.VMEM((2, tq, 1), F32)],
        compiler_params=_params("parallel", "parallel", "arbitrary"),
        name="sb_prompt",
    )(q, k, v)


def _sb_sample_kernel(q_ref, k_ref, v_ref, pk_ref, pv_ref, o_ref, acc_ref, car_ref, *, tk_past):
    t = q_ref.shape[0]
    past_len = pk_ref.shape[0]
    acc_ref[...] = jnp.zeros_like(acc_ref)
    car_ref[...] = jnp.zeros_like(car_ref)
    head_cols = [slice(h * SB_DIM, (h + 1) * SB_DIM) for h in range(SB_HEADS)]

    def fold_all_heads(get_k, get_v, mask):
        z = jnp.concatenate([_dot_nt(q_ref[:, head_cols[h]], get_k(h)) for h in range(SB_HEADS)], axis=0)
        p, carry = _sb_weights(z, car_ref[...], mask)
        car_ref[...] = carry
        for h in range(SB_HEADS):
            acc_ref[h] += _dot(p[h * t:(h + 1) * t], get_v(h))
        return jnp.max(carry) > _LOG_F32_ZERO

    r = lax.broadcasted_iota(jnp.int32, (SB_HEADS * t, t), 0)
    c = lax.broadcasted_iota(jnp.int32, (SB_HEADS * t, t), 1)
    alive = fold_all_heads(lambda h: k_ref[:, head_cols[h]], lambda h: v_ref[:, head_cols[h]],
                           c < (r & (t - 1)))

    def past_block(kb_idx):
        rows = pl.ds(pl.multiple_of(kb_idx * tk_past, tk_past), tk_past)
        return fold_all_heads(lambda h: pk_ref[rows, h, :].astype(BF16),
                              lambda h: pv_ref[rows, h, :].astype(BF16), None)

    _fold_descending(past_len // tk_past, alive, past_block)
    for h in range(SB_HEADS):
        o_ref[:, head_cols[h]] = acc_ref[h].astype(o_ref.dtype)


def _sb_sample(q, k, v, past_k, past_v, *, tk_past):
    b, t, w = q.shape
    assert t & (t - 1) == 0, "query chunk length must be a power of two"
    cur = pl.BlockSpec((None, t, w), lambda bi: (bi, 0, 0))
    past = pl.BlockSpec((None,) + past_k.shape[1:], lambda bi: (bi, 0, 0, 0))
    return pl.pallas_call(
        functools.partial(_sb_sample_kernel, tk_past=tk_past),
        grid=(b,),
        in_specs=[cur, cur, cur, past, past],
        out_specs=cur,
        out_shape=jax.ShapeDtypeStruct((b, t, w), BF16),
        scratch_shapes=[pltpu.VMEM((SB_HEADS, t, SB_DIM), F32), pltpu.VMEM((SB_HEADS * t, 1), F32)],
        compiler_params=_params("parallel"),
        name="sb_sample",
    )(q, k, v, past_k, past_v)


def _gla_kernel(gq_ref, gk_ref, gv_ref, la_ref, gr_ref, gn_ref, s0_ref, o_ref, st_ref, st_scr, *, n_chunks):
    c_len = GLA_CHUNK
    tt = n_chunks * c_len
    kw = GLA_HEADS * GLA_DK
    vw = GLA_HEADS * GLA_DV
    t = pl.program_id(1)

    @pl.when(t == 0)
    def _():
        st_scr[...] = s0_ref[...]

    r = lax.broadcasted_iota(jnp.int32, (tt, tt), 0)
    c = lax.broadcasted_iota(jnp.int32, (tt, tt), 1)
    chunk_bits = c_len.bit_length() - 1
    same_chunk = (r >> chunk_bits) == (c >> chunk_bits)
    tril = (same_chunk & (r >= c)).astype(BF16)
    ones = same_chunk.astype(BF16)
    la_hi, la_lo = _split_bf16(la_ref[...])
    b = _dot(tril, la_hi) + _dot(tril, la_lo)
    b_tot = _dot(ones, la_hi) + _dot(ones, la_lo)
    gk = gk_ref[...]
    qd = gq_ref[...] * jnp.exp(b)
    kd = (gk * jnp.exp(-b)).astype(BF16)
    ke = (gk * jnp.exp(b_tot - b)).astype(BF16)
    decay = jnp.exp(b_tot)

    rc = lax.broadcasted_iota(jnp.int32, (GLA_HEADS * c_len, c_len), 0)
    cc = lax.broadcasted_iota(jnp.int32, (GLA_HEADS * c_len, c_len), 1)
    causal = (rc & (c_len - 1)) >= cc
    eye = (lax.broadcasted_iota(jnp.int32, (vw, vw), 0)
           == lax.broadcasted_iota(jnp.int32, (vw, vw), 1)).astype(BF16)
    lane_q = lax.broadcasted_iota(jnp.int32, (c_len, kw), 1)
    lane_s = lax.broadcasted_iota(jnp.int32, (GLA_DV, kw), 1)

    for ci in range(n_chunks):
        rows = slice(ci * c_len, (ci + 1) * c_len)
        qd_c = qd[rows]
        qs = jnp.concatenate(
            [jnp.where((lane_q >= h * GLA_DK) & (lane_q < (h + 1) * GLA_DK), qd_c, 0.0)
             for h in range(GLA_HEADS)], axis=0).astype(BF16)
        att = jnp.where(causal, _dot_nt(qs, kd[rows]), 0.0).astype(BF16)
        st = st_scr[...]
        o_state = _dot_nt(qs, st.astype(BF16))
        v = gv_ref[rows, :]
        v_t = _dot_nt(eye, v).astype(BF16)
        upd = _dot(v_t, ke[rows])
        st_new = st * decay[ci * c_len:ci * c_len + 1, :]
        for h in range(GLA_HEADS):
            hrows = slice(h * c_len, (h + 1) * c_len)
            cols = slice(h * GLA_DV, (h + 1) * GLA_DV)
            o = _dot(att[hrows], v[:, cols]) + o_state[hrows]
            on = _rmsnorm(o, gn_ref[:, cols])
            gr = gr_ref[rows, cols]
            o_ref[rows, cols] = (on * (gr * jax.nn.sigmoid(gr))).astype(o_ref.dtype)
            in_head = (lane_s >= h * GLA_DK) & (lane_s < (h + 1) * GLA_DK)
            st_new = st_new + jnp.where(in_head, upd[cols], 0.0)
        st_scr[...] = st_new

    @pl.when(t == pl.num_programs(1) - 1)
    def _():
        st_ref[...] = st_scr[...]


def _gla(gq, gk, gv, la, gr, gn, st0, *, tt):
    b, t, kw = gq.shape
    vw = gv.shape[-1]
    tok = lambda w: pl.BlockSpec((None, tt, w), lambda bi, ti: (bi, ti, 0))
    state = pl.BlockSpec((None, GLA_DV, kw), lambda bi, ti: (bi, 0, 0))
    return pl.pallas_call(
        functools.partial(_gla_kernel, n_chunks=tt // GLA_CHUNK),
        grid=(b, t // tt),
        in_specs=[tok(kw), tok(kw), tok(vw), tok(kw), tok(vw), _resident(gn), state],
        out_specs=[tok(vw), state],
        out_shape=[jax.ShapeDtypeStruct((b, t, vw), BF16), jax.ShapeDtypeStruct((b, GLA_DV, kw), F32)],
        scratch_shapes=[pltpu.VMEM((GLA_DV, kw), F32)],
        compiler_params=_params("parallel", "arbitrary"),
        name="gla",
    )(gq, gk, gv, la, gr, gn, st0)


def _mem_kernel(q_ref, mk_ref, mv_ref, o_ref):
    head_major = len(mk_ref.shape) == 3
    for h in range(MEM_HEADS):
        cols = slice(h * MEM_DIM, (h + 1) * MEM_DIM)
        kh = mk_ref[:, h, :] if head_major else mk_ref[:, cols]
        vh = mv_ref[:, h, :] if head_major else mv_ref[:, cols]
        s = _dot_nt(q_ref[:, cols], kh.astype(BF16)) * MEM_DIM ** -0.5
        e = jnp.exp(s - jnp.max(s, axis=-1, keepdims=True))
        p = e / jnp.sum(e, axis=-1, keepdims=True)
        o_ref[:, cols] = _dot(p.astype(BF16), vh.astype(BF16)).astype(o_ref.dtype)


def _mem_attention(q, mk, mv, *, tq):
    b, t, w = q.shape
    tile = pl.BlockSpec((None, tq, w), lambda bi, qi: (bi, qi, 0))
    mem = pl.BlockSpec((None,) + mk.shape[1:], lambda bi, qi: (bi,) + (0,) * (mk.ndim - 1))
    return pl.pallas_call(
        _mem_kernel,
        grid=(b, t // tq),
        in_specs=[tile, mem, mem],
        out_specs=tile,
        out_shape=jax.ShapeDtypeStruct((b, t, w), BF16),
        compiler_params=_params("parallel", "arbitrary"),
        name="mem_attention",
    )(q, mk, mv)


def _merge_kernel(x_ref, osb_ref, ogla_ref, omem_ref, gmix_ref, wgt_ref, wbr_ref, wout_ref,
                  gffn_ref, wr_hi_ref, wr_lo_ref, br_ref, h_ref, ht_ref, idx_ref, wts_ref):
    x = x_ref[...]
    d = x.shape[-1]
    xn = _rmsnorm(x, gmix_ref[...]).astype(BF16)
    mixed = None
    for n, o_ref in enumerate((osb_ref, ogla_ref, omem_ref)):
        gate = jax.nn.sigmoid(_dot(xn, wgt_ref[:, n * d:(n + 1) * d]))
        term = gate * _dot(o_ref[...], wbr_ref[n])
        mixed = term if mixed is None else mixed + term
    h = x + _dot(mixed.astype(BF16), wout_ref[...])
    h_ref[...] = h
    hn = _rmsnorm(h, gffn_ref[...])
    ht_ref[...] = hn.astype(BF16)

    hn_hi, hn_lo = _split_bf16(hn)
    logits = (_dot(hn_hi, wr_hi_ref[...]) + _dot(hn_lo, wr_hi_ref[...])
              + _dot(hn_hi, wr_lo_ref[...]) + br_ref[...])
    lane = lax.broadcasted_iota(jnp.int32, logits.shape, 1)
    rmax = lambda a: jnp.max(a, axis=-1, keepdims=True)
    rmin = lambda a: jnp.min(a, axis=-1, keepdims=True)
    rsum = lambda a: jnp.sum(a, axis=-1, keepdims=True)

    lc = jnp.where(lane < N_GROUPS, logits, NEG_BIG)
    mc = rmax(lc)
    grp = rmin(jnp.where(lc == mc, lane, LANES))
    p_grp = 1.0 / rsum(jnp.exp(lc - mc))

    lo = N_GROUPS + grp * EXPERTS_PER_GROUP
    in_grp = (lane >= lo) & (lane < lo + EXPERTS_PER_GROUP)
    lf = jnp.where(in_grp, logits, NEG_BIG)
    ef = jnp.exp(lf - rmax(lf))
    pf = jnp.where(in_grp, ef / rsum(ef), -1.0)
    v1 = rmax(pf)
    i1 = rmin(jnp.where(pf == v1, lane, LANES))
    pf2 = jnp.where(lane == i1, -1.0, pf)
    v2 = rmax(pf2)
    i2 = rmin(jnp.where(pf2 == v2, lane, LANES))
    tot = v1 + v2
    idx_ref[...] = jnp.where(lane == 0, i1 - N_GROUPS, jnp.where(lane == 1, i2 - N_GROUPS, 0))
    wts_ref[...] = jnp.where(lane == 0, p_grp * (v1 / tot), jnp.where(lane == 1, p_grp * (v2 / tot), 0.0))


def _merge(x, osb, ogla, omem, gmix, wgt, wbr, wout, gffn, wr_hi, wr_lo, br, *, tm):
    n, d = x.shape
    bw = osb.shape[1]
    row = lambda w: pl.BlockSpec((tm, w), lambda i: (i, 0))
    return pl.pallas_call(
        _merge_kernel,
        grid=(n // tm,),
        in_specs=[row(d), row(bw), row(bw), row(bw), _resident(gmix), _resident(wgt), _resident(wbr),
                  _resident(wout), _resident(gffn), _resident(wr_hi), _resident(wr_lo), _resident(br)],
        out_specs=[row(d), row(d), row(LANES), row(LANES)],
        out_shape=[jax.ShapeDtypeStruct((n, d), F32), jax.ShapeDtypeStruct((n, d), BF16),
                   jax.ShapeDtypeStruct((n, LANES), jnp.int32), jax.ShapeDtypeStruct((n, LANES), F32)],
        compiler_params=_params("parallel"),
        name="merge_route",
    )(x, osb, ogla, omem, gmix, wgt, wbr, wout, gffn, wr_hi, wr_lo, br)


def _expert_kernel(blk_e_ref, n_used_ref, x_ref, wg_ref, wu_ref, wd_ref, y_ref):
    i = pl.program_id(0)

    @pl.when(i < n_used_ref[0])
    def _():
        x = x_ref[...]
        g = _dot(x, wg_ref[...])
        u = _dot(x, wu_ref[...])
        a = (g * jax.nn.sigmoid(g) * u).astype(BF16)
        y_ref[...] = _dot(a, wd_ref[...])

    @pl.when(i >= n_used_ref[0])
    def _():
        y_ref[...] = jnp.zeros_like(y_ref)


def _experts(xb, blk_e, n_used, wg, wu, wd, *, tm):
    p, d = xb.shape
    de = wg.shape[-1]
    grid_spec = pltpu.PrefetchScalarGridSpec(
        num_scalar_prefetch=2,
        grid=(p // tm,),
        in_specs=[pl.BlockSpec((tm, d), lambda i, be, nu: (i, 0)),
                  pl.BlockSpec((None, d, de), lambda i, be, nu: (be[i], 0, 0)),
                  pl.BlockSpec((None, d, de), lambda i, be, nu: (be[i], 0, 0)),
                  pl.BlockSpec((None, de, d), lambda i, be, nu: (be[i], 0, 0))],
        out_specs=pl.BlockSpec((tm, d), lambda i, be, nu: (i, 0)),
    )
    return pl.pallas_call(
        _expert_kernel,
        grid_spec=grid_spec,
        out_shape=jax.ShapeDtypeStruct((p, d), F32),
        compiler_params=_params("arbitrary"),
        name="experts",
    )(blk_e, n_used, xb, wg, wu, wd)


def _final_kernel(h_ref, y0_ref, y1_ref, wts_ref, g_ref, o_ref):
    w = wts_ref[...]
    h = h_ref[...] + (y0_ref[...] * w[:, 0:1] + y1_ref[...] * w[:, 1:2])
    o_ref[...] = _rmsnorm(h, g_ref[...])


def _final(h, y0, y1, wts, g, *, tm):
    n, d = h.shape
    row = lambda w: pl.BlockSpec((tm, w), lambda i: (i, 0))
    return pl.pallas_call(
        _final_kernel,
        grid=(n // tm,),
        in_specs=[row(d), row(d), row(d), row(LANES), _resident(g)],
        out_specs=row(d),
        out_shape=jax.ShapeDtypeStruct((n, d), F32),
        compiler_params=_params("parallel"),
        name="final_norm",
    )(h, y0, y1, wts, g)


def _dispatch_plan(idx, tm):
    n = idx.shape[0]
    a = n * TOP_K
    flat_e = idx.reshape(a)
    onehot = (flat_e[:, None] == jnp.arange(N_EXPERTS, dtype=jnp.int32)[None, :]).astype(jnp.int32)
    ranks = jnp.cumsum(onehot, axis=0)
    counts = ranks[-1]
    rank = jnp.sum(onehot * ranks, axis=1) - 1
    padded = (counts + tm - 1) // tm * tm
    pend = jnp.cumsum(padded)
    pstart = pend - padded
    dest = pstart[flat_e] + rank
    n_blk = -(-(a + N_EXPERTS * (tm - 1)) // tm)
    blk_start = jnp.arange(n_blk, dtype=jnp.int32) * tm
    blk_e = jnp.minimum(jnp.sum((pend[None, :] <= blk_start[:, None]).astype(jnp.int32), axis=1),
                        N_EXPERTS - 1).astype(jnp.int32)
    n_used = (pend[-1] // tm).astype(jnp.int32).reshape(1)
    return dest.astype(jnp.int32), blk_e, n_used, n_blk


def _moe(ht, idx, wts, h, wg, wu, wd, g_final, *, tm_e, tm):
    n = ht.shape[0]
    dest, blk_e, n_used, n_blk = _dispatch_plan(idx[:, :TOP_K], tm_e)
    tok = jnp.zeros((n_blk * tm_e,), jnp.int32).at[dest].set(jnp.repeat(jnp.arange(n, dtype=jnp.int32), TOP_K))
    yb = _experts(ht[tok], blk_e, n_used, wg, wu, wd, tm=tm_e)
    pos = dest.reshape(n, TOP_K)
    return _final(h, yb[pos[:, 0]], yb[pos[:, 1]], wts, g_final, tm=tm)


def _layer(x, sb_past, gla_state, mem_kv, wts, *, tm, tq_sb, tt_gla, tq_mem, tm_e):
    b, t, d = x.shape
    n = b * t
    xf = x.reshape(n, d)
    sq, skb, svb, sk, sv, gq, gk, gv, la, gr, mq = _proj(
        xf, wts["g_mix"], wts["w_proj"], wts["wa2"], wts["ba"], tm)
    r3 = lambda a: a.reshape(b, t, a.shape[-1])
    if sb_past is None:
        o_sb = _sb_prompt(r3(sq), r3(skb), r3(svb), tq=tq_sb)
    else:
        o_sb = _sb_sample(r3(sq), r3(skb), r3(svb), sb_past[0], sb_past[1], tk_past=256)
    o_gla, st = _gla(r3(gq), r3(gk), r3(gv), r3(la), r3(gr), wts["g_gla"], gla_state, tt=tt_gla)
    o_mem = _mem_attention(r3(mq), mem_kv[0], mem_kv[1], tq=tq_mem)
    h, ht, idx, rw = _merge(xf, o_sb.reshape(n, -1), o_gla.reshape(n, -1), o_mem.reshape(n, -1),
                            wts["g_mix"], wts["w_gt"], wts["w_br"], wts["w_out"], wts["g_ffn"],
                            wts["wr_hi"], wts["wr_lo"], wts["b_r"], tm=tm)
    y = _moe(ht, idx, rw, h, wts["wg"], wts["wu"], wts["wd"], wts["g_final"], tm_e=tm_e, tm=tm)
    return y.reshape(b, t, d), sk, sv, st


def _state_to_t(s):
    b = s.shape[0]
    return s.transpose(0, 3, 1, 2).reshape(b, GLA_DV, GLA_HEADS * GLA_DK)


def _state_from_t(st):
    b = st.shape[0]
    return st.reshape(b, GLA_DV, GLA_HEADS, GLA_DK).transpose(0, 2, 3, 1)


def kernel(x_prompt, x_sample, mem_prompt, cache_sb_k, cache_sb_v, state_gla, cache_mem_k, cache_mem_v,
           norm_mix, w_in, w_gla_a2, b_gla_a, gla_norm, norm_mem, w_mem_kv, w_branch, w_out,
           norm_ffn, w_coarse, b_coarse, w_fine, b_fine, w_e_gate, w_e_up, w_e_down, norm_final):
    assert w_in.shape[0] == 1, "single-layer model"
    d = x_prompt.shape[-1]
    bp, tp, _ = x_prompt.shape
    bs, ts, _ = x_sample.shape
    bw = d // 2

    w = w_in[0]
    offs = [0]
    for s in (bw, bw, bw, 256, 256, bw, GLA_RANK, bw, bw, N_BRANCH * d):
        offs.append(offs[-1] + s)
    col = lambda i: w[:, offs[i]:offs[i + 1]]
    w_proj = jnp.concatenate(
        [col(0), col(1), col(2), col(3), col(4), col(5), col(7), col(8),
         jnp.pad(col(6), ((0, 0), (0, LANES - GLA_RANK)))], axis=1).astype(BF16)
    assert w_proj.shape[1] == _PROJ_W
    row2 = lambda a: a.reshape(1, -1)
    n_route = N_GROUPS + N_EXPERTS
    w_route = jnp.concatenate([w_coarse[0], w_fine[0].transpose(1, 0, 2).reshape(d, N_EXPERTS)], axis=1)
    w_route = jnp.pad(w_route, ((0, 0), (0, LANES - n_route)))
    wr_hi = w_route.astype(BF16)
    wr_lo = (w_route - wr_hi.astype(F32)).astype(BF16)
    b_route = jnp.pad(jnp.concatenate([b_coarse[0], b_fine[0].reshape(-1)]), (0, LANES - n_route)).reshape(1, LANES)
    wts = dict(
        g_mix=row2(norm_mix[0]), w_proj=w_proj,
        wa2=jnp.pad(w_gla_a2[0], ((0, LANES - GLA_RANK), (0, 0))).astype(BF16), ba=row2(b_gla_a[0]),
        g_gla=row2(gla_norm[0]),
        w_gt=col(9).astype(BF16), w_br=w_branch[0].astype(BF16), w_out=w_out[0].astype(BF16),
        g_ffn=row2(norm_ffn[0]), wr_hi=wr_hi, wr_lo=wr_lo, b_r=b_route,
        wg=w_e_gate[0].astype(BF16), wu=w_e_up[0].astype(BF16), wd=w_e_down[0].astype(BF16),
        g_final=row2(norm_final),
    )

    m = mem_prompt.shape[1]
    mk, mv = _memkv(mem_prompt.reshape(bp * m, d), row2(norm_mem[0]), w_mem_kv[0].astype(BF16), tm=min(512, bp * m))
    st0 = jnp.zeros((bp, GLA_DV, GLA_HEADS * GLA_DK), F32)
    y_p, sk_p, sv_p, st_p = _layer(x_prompt, None, st0, (mk.reshape(bp, m, bw), mv.reshape(bp, m, bw)), wts,
                                   tm=min(512, bp * tp), tq_sb=min(256, tp), tt_gla=min(256, tp),
                                   tq_mem=min(512, tp), tm_e=256)

    y_s, sk_s, sv_s, st_s = _layer(x_sample, (cache_sb_k[0], cache_sb_v[0]), _state_to_t(state_gla[0]),
                                   (cache_mem_k[0], cache_mem_v[0]), wts,
                                   tm=min(512, bs * ts), tq_sb=ts, tt_gla=ts, tq_mem=ts, tm_e=256)

    hd = lambda a, bb, tt: a.reshape(1, bb, tt, SB_HEADS, SB_DIM)
    return (y_p, y_s,
            hd(sk_p, bp, tp), hd(sv_p, bp, tp),
            _state_from_t(st_p)[None],
            mk.reshape(1, bp, m, MEM_HEADS, MEM_DIM), mv.reshape(1, bp, m, MEM_HEADS, MEM_DIM),
            hd(sk_s, bs, ts), hd(sv_s, bs, ts),
            _state_from_t(st_s)[None])
```

```python
import functools

import jax
import jax.numpy as jnp
from jax import lax
from jax.experimental import pallas as pl
from jax.experimental.pallas import tpu as pltpu

F32 = jnp.float32
BF16 = jnp.bfloat16
EPS = 1e-6

SB_HEADS = 8
SB_DIM = 64
GLA_HEADS = 4
GLA_DK = 64
GLA_DV = 128
GLA_RANK = 16
GLA_TAU = 16.0
GLA_CHUNK = 64
MEM_HEADS = 4
MEM_DIM = 128
N_BRANCH = 3
N_GROUPS = 4
EXPERTS_PER_GROUP = 8
N_EXPERTS = N_GROUPS * EXPERTS_PER_GROUP
TOP_K = 2

LANES = 128
VMEM_LIMIT = 56 * 1024 * 1024
NEG_BIG = -1e30
_LOG_F32_ZERO = -110.0

_NT = (((1,), (1,)), ((), ()))


def _dot(a, b):
    return jnp.dot(a, b, preferred_element_type=F32)


def _dot_nt(a, b):
    return lax.dot_general(a, b, _NT, preferred_element_type=F32)


def _split_bf16(x):
    hi = x.astype(BF16)
    lo = (x - hi.astype(F32)).astype(BF16)
    return hi, lo


def _rmsnorm(x, g):
    return x * lax.rsqrt(jnp.mean(x * x, axis=-1, keepdims=True) + EPS) * g


def _log_sigmoid(x):
    return jnp.minimum(x, 0.0) - jnp.log1p(jnp.exp(-jnp.abs(x)))


def _resident(a):
    return pl.BlockSpec(a.shape, lambda *_: (0,) * a.ndim, pipeline_mode=pl.Buffered(1))


def _params(*sem):
    return pltpu.CompilerParams(dimension_semantics=sem, vmem_limit_bytes=VMEM_LIMIT)


_C_SQ, _C_SK, _C_SV = 0, 512, 1024
_C_GQ, _C_GK, _C_GV = 1536, 1792, 2048
_C_GR, _C_MQ, _C_GLR = 2560, 3072, 3584
_PROJ_W = 3712


def _proj_kernel(*refs, kv_transposed, key_block):
    if kv_transposed:
        x_ref, g_ref, w_ref, wa2_ref, ba_ref, wkvt_ref = refs[:6]
    else:
        x_ref, g_ref, w_ref, wa2_ref, ba_ref = refs[:5]
    sq_ref, skb_ref, svb_ref, sk_ref, sv_ref, gq_ref, gk_ref, gv_ref, la_ref, gr_ref, mq_ref = refs[-11:]
    xn = _rmsnorm(x_ref[...], g_ref[...]).astype(BF16)

    def seg(start, width):
        return _dot(xn, w_ref[:, start:start + width])

    sq_ref[...] = (seg(_C_SQ, 512) * SB_DIM ** -0.5).astype(BF16)
    if kv_transposed:
        tm = xn.shape[0]
        for half, (bf_ref, heads_ref) in enumerate(((skb_ref, sk_ref), (svb_ref, sv_ref))):
            kv_t = _dot_nt(wkvt_ref[half * 512:(half + 1) * 512, :], xn)
            heads_ref[...] = kv_t.reshape(SB_HEADS, SB_DIM, tm)
            for j in range(tm // key_block):
                bf_ref[j] = kv_t[:, j * key_block:(j + 1) * key_block].astype(BF16)
    else:
        for c0, bf_ref, heads_ref in ((_C_SK, skb_ref, sk_ref), (_C_SV, svb_ref, sv_ref)):
            kv = seg(c0, 512)
            bf_ref[...] = kv.astype(BF16)
            for h in range(SB_HEADS):
                heads_ref[:, h, :] = kv[:, h * SB_DIM:(h + 1) * SB_DIM]
    gq_ref[...] = seg(_C_GQ, 256) * GLA_DK ** -0.5
    gk_ref[...] = seg(_C_GK, 256)
    gv_ref[...] = seg(_C_GV, 512).astype(BF16)
    gr_ref[...] = seg(_C_GR, 512)
    mq_ref[...] = seg(_C_MQ, 512).astype(BF16)
    glr = seg(_C_GLR, LANES).astype(BF16)
    la_ref[...] = _log_sigmoid(_dot(glr, wa2_ref[...]) + ba_ref[...]) * (1.0 / GLA_TAU)


def _proj(x, g, w_proj, wa2, ba, w_kvt, *, tm, seq_len, key_block):
    n, d = x.shape
    kv_transposed = w_kvt is not None
    row = lambda w: pl.BlockSpec((tm, w), lambda i: (i, 0))
    flat = lambda w, dt: (row(w), jax.ShapeDtypeStruct((n, w), dt))
    if kv_transposed:
        nb, per_b = n // seq_len, seq_len // tm
        kv_bf = (pl.BlockSpec((None, tm // key_block, 512, key_block), lambda i: (i // per_b, i % per_b, 0, 0)),
                 jax.ShapeDtypeStruct((nb, seq_len // key_block, 512, key_block), BF16))
        kv_f32 = (pl.BlockSpec((None, SB_HEADS, SB_DIM, tm), lambda i: (i // per_b, 0, 0, i % per_b)),
                  jax.ShapeDtypeStruct((nb, SB_HEADS, SB_DIM, seq_len), F32))
    else:
        kv_bf = flat(512, BF16)
        kv_f32 = (pl.BlockSpec((tm, SB_HEADS, SB_DIM), lambda i: (i, 0, 0)),
                  jax.ShapeDtypeStruct((n, SB_HEADS, SB_DIM), F32))
    outs = [flat(512, BF16), kv_bf, kv_bf, kv_f32, kv_f32,
            flat(256, F32), flat(256, F32), flat(512, BF16), flat(256, F32), flat(512, F32), flat(512, BF16)]
    args = [x, g, w_proj, wa2, ba] + ([w_kvt] if kv_transposed else [])
    return pl.pallas_call(
        functools.partial(_proj_kernel, kv_transposed=kv_transposed, key_block=key_block),
        grid=(n // tm,),
        in_specs=[row(d)] + [_resident(a) for a in args[1:]],
        out_specs=[o[0] for o in outs],
        out_shape=[o[1] for o in outs],
        compiler_params=_params("parallel"),
        name="proj",
    )(*args)


def _memkv_kernel(x_ref, g_ref, w_ref, mk_ref, mv_ref):
    xn = _rmsnorm(x_ref[...], g_ref[...]).astype(BF16)
    half = mk_ref.shape[-1]
    mk_ref[...] = _dot(xn, w_ref[:, :half])
    mv_ref[...] = _dot(xn, w_ref[:, half:])


def _memkv(x, g, w, tm):
    n, d = x.shape
    half = w.shape[1] // 2
    row = lambda wd: pl.BlockSpec((tm, wd), lambda i: (i, 0))
    return pl.pallas_call(
        _memkv_kernel,
        grid=(n // tm,),
        in_specs=[row(d), _resident(g), _resident(w)],
        out_specs=[row(half), row(half)],
        out_shape=[jax.ShapeDtypeStruct((n, half), F32)] * 2,
        compiler_params=_params("parallel"),
        name="memkv",
    )(x, g, w)


def _suffix_matrix(tk):
    j = lax.broadcasted_iota(jnp.int32, (2 * tk, tk), 0)
    s = lax.broadcasted_iota(jnp.int32, (2 * tk, tk), 1)
    return ((j > s) & ((j < tk) | (j > s + tk))).astype(BF16)


def _sb_weights(z, carry, mask):
    lz = jnp.minimum(z, 0.0) - jnp.log(1.0 + jnp.exp(-jnp.abs(z)))
    l1 = lz - z
    if mask is not None:
        l1 = jnp.where(mask, l1, 0.0)
    hi, lo = _split_bf16(l1)
    cum = _dot(jnp.concatenate([hi, lo], axis=1), _suffix_matrix(z.shape[1]))
    p = jnp.exp(lz + cum + carry)
    if mask is not None:
        p = jnp.where(mask, p, 0.0)
    return p.astype(BF16), carry + cum[:, :1] + l1[:, :1]


def _fold_descending(n_blocks, alive, fold_block):
    def cond(state):
        i, live = state
        return (i < n_blocks) & live

    def body(state):
        i, _ = state
        return i + 1, fold_block(n_blocks - 1 - i)

    return lax.while_loop(cond, body, (jnp.int32(0), alive))[1]


def _sb_prompt_kernel(q_ref, k_ref, v_ref, o_ref, acc_ref, car_ref, *, tq):
    qi = pl.program_id(2)
    q = q_ref[...]
    lane = lax.broadcasted_iota(jnp.int32, (tq, LANES), 1)
    q_heads = (jnp.where(lane < SB_DIM, q, jnp.zeros_like(q)),
               jnp.where(lane >= SB_DIM, q, jnp.zeros_like(q)))
    acc_ref[...] = jnp.zeros_like(acc_ref)
    car_ref[...] = jnp.zeros_like(car_ref)

    def block(kb, mask):
        kt = k_ref[kb]
        vt = v_ref[kb]
        top = None
        for h in range(2):
            p, carry = _sb_weights(_dot(q_heads[h], kt), car_ref[h], mask)
            acc_ref[h] += _dot_nt(p, vt)
            car_ref[h] = carry
            m = jnp.max(carry)
            top = m if top is None else jnp.maximum(top, m)
        return top > _LOG_F32_ZERO

    r = lax.broadcasted_iota(jnp.int32, (tq, tq), 0)
    c = lax.broadcasted_iota(jnp.int32, (tq, tq), 1)
    alive = block(qi, c < r)
    _fold_descending(qi, alive, lambda kb: block(kb, None))
    o_ref[...] = jnp.where(lane < SB_DIM, acc_ref[0], acc_ref[1]).astype(o_ref.dtype)


def _sb_prompt(q, k_t, v_t, *, tq):
    b, t, w = q.shape
    assert k_t.shape == (b, t // tq, w, tq)
    seq = pl.BlockSpec((None, t // tq, LANES, tq), lambda bi, p, qi: (bi, 0, p, 0))
    tile = pl.BlockSpec((None, tq, LANES), lambda bi, p, qi: (bi, qi, p))
    return pl.pallas_call(
        functools.partial(_sb_prompt_kernel, tq=tq),
        grid=(b, w // LANES, t // tq),
        in_specs=[tile, seq, seq],
        out_specs=tile,
        out_shape=jax.ShapeDtypeStruct((b, t, w), BF16),
        scratch_shapes=[pltpu.VMEM((2, tq, LANES), F32), pltpu.VMEM((2, tq, 1), F32)],
        compiler_params=_params("parallel", "parallel", "arbitrary"),
        name="sb_prompt",
    )(q, k_t, v_t)


def _sb_sample_kernel(q_ref, k_ref, v_ref, pk_ref, pv_ref, o_ref, acc_ref, car_ref, *, tk_past):
    t = q_ref.shape[0]
    past_len = pk_ref.shape[-1]
    acc_ref[...] = jnp.zeros_like(acc_ref)
    car_ref[...] = jnp.zeros_like(car_ref)
    head_cols = [slice(h * SB_DIM, (h + 1) * SB_DIM) for h in range(SB_HEADS)]

    def fold_all_heads(logits, weighted_values, mask):
        z = jnp.concatenate([logits(h, q_ref[:, head_cols[h]]) for h in range(SB_HEADS)], axis=0)
        p, carry = _sb_weights(z, car_ref[...], mask)
        car_ref[...] = carry
        for h in range(SB_HEADS):
            acc_ref[h] += weighted_values(h, p[h * t:(h + 1) * t])
        return jnp.max(carry) > _LOG_F32_ZERO

    r = lax.broadcasted_iota(jnp.int32, (SB_HEADS * t, t), 0)
    c = lax.broadcasted_iota(jnp.int32, (SB_HEADS * t, t), 1)
    alive = fold_all_heads(lambda h, qh: _dot_nt(qh, k_ref[:, head_cols[h]]),
                           lambda h, ph: _dot(ph, v_ref[:, head_cols[h]]),
                           c < (r & (t - 1)))

    def past_block(kb_idx):
        pos = pl.ds(pl.multiple_of(kb_idx * tk_past, tk_past), tk_past)
        return fold_all_heads(lambda h, qh: _dot(qh, pk_ref[h, :, pos].astype(BF16)),
                              lambda h, ph: _dot_nt(ph, pv_ref[h, :, pos].astype(BF16)), None)

    _fold_descending(past_len // tk_past, alive, past_block)
    for h in range(SB_HEADS):
        o_ref[:, head_cols[h]] = acc_ref[h].astype(o_ref.dtype)


def _sb_sample(q, k, v, past_k, past_v, *, tk_past):
    b, t, w = q.shape
    assert t & (t - 1) == 0, "query chunk length must be a power of two"
    cur = pl.BlockSpec((None, t, w), lambda bi: (bi, 0, 0))
    past = pl.BlockSpec((None,) + past_k.shape[1:], lambda bi: (bi, 0, 0, 0))
    return pl.pallas_call(
        functools.partial(_sb_sample_kernel, tk_past=tk_past),
        grid=(b,),
        in_specs=[cur, cur, cur, past, past],
        out_specs=cur,
        out_shape=jax.ShapeDtypeStruct((b, t, w), BF16),
        scratch_shapes=[pltpu.VMEM((SB_HEADS, t, SB_DIM), F32), pltpu.VMEM((SB_HEADS * t, 1), F32)],
        compiler_params=_params("parallel"),
        name="sb_sample",
    )(q, k, v, past_k, past_v)


def _gla_kernel(gq_ref, gk_ref, gv_ref, la_ref, gr_ref, gn_ref, s0_ref, o_ref, st_ref, st_scr, *, n_chunks):
    c_len = GLA_CHUNK
    tt = n_chunks * c_len
    kw = GLA_HEADS * GLA_DK
    vw = GLA_HEADS * GLA_DV
    t = pl.program_id(1)

    @pl.when(t == 0)
    def _():
        st_scr[...] = s0_ref[...]

    r = lax.broadcasted_iota(jnp.int32, (tt, tt), 0)
    c = lax.broadcasted_iota(jnp.int32, (tt, tt), 1)
    chunk_bits = c_len.bit_length() - 1
    same_chunk = (r >> chunk_bits) == (c >> chunk_bits)
    tril = (same_chunk & (r >= c)).astype(BF16)
    ones = same_chunk.astype(BF16)
    la_hi, la_lo = _split_bf16(la_ref[...])
    b = _dot(tril, la_hi) + _dot(tril, la_lo)
    b_tot = _dot(ones, la_hi) + _dot(ones, la_lo)
    gk = gk_ref[...]
    qd = gq_ref[...] * jnp.exp(b)
    kd = (gk * jnp.exp(-b)).astype(BF16)
    ke = (gk * jnp.exp(b_tot - b)).astype(BF16)
    decay = jnp.exp(b_tot)

    rc = lax.broadcasted_iota(jnp.int32, (GLA_HEADS * c_len, c_len), 0)
    cc = lax.broadcasted_iota(jnp.int32, (GLA_HEADS * c_len, c_len), 1)
    causal = (rc & (c_len - 1)) >= cc
    eye = (lax.broadcasted_iota(jnp.int32, (vw, vw), 0)
           == lax.broadcasted_iota(jnp.int32, (vw, vw), 1)).astype(BF16)
    lane_q = lax.broadcasted_iota(jnp.int32, (c_len, kw), 1)
    lane_s = lax.broadcasted_iota(jnp.int32, (GLA_DV, kw), 1)

    for ci in range(n_chunks):
        rows = slice(ci * c_len, (ci + 1) * c_len)
        qd_c = qd[rows]
        qs = jnp.concatenate(
            [jnp.where((lane_q >= h * GLA_DK) & (lane_q < (h + 1) * GLA_DK), qd_c, 0.0)
             for h in range(GLA_HEADS)], axis=0).astype(BF16)
        att = jnp.where(causal, _dot_nt(qs, kd[rows]), 0.0).astype(BF16)
        st = st_scr[...]
        o_state = _dot_nt(qs, st.astype(BF16))
        v = gv_ref[rows, :]
        v_t = _dot_nt(eye, v).astype(BF16)
        upd = _dot(v_t, ke[rows])
        st_new = st * decay[ci * c_len:ci * c_len + 1, :]
        for h in range(GLA_HEADS):
            hrows = slice(h * c_len, (h + 1) * c_len)
            cols = slice(h * GLA_DV, (h + 1) * GLA_DV)
            o = _dot(att[hrows], v[:, cols]) + o_state[hrows]
            on = _rmsnorm(o, gn_ref[:, cols])
            gr = gr_ref[rows, cols]
            o_ref[rows, cols] = (on * (gr * jax.nn.sigmoid(gr))).astype(o_ref.dtype)
            in_head = (lane_s >= h * GLA_DK) & (lane_s < (h + 1) * GLA_DK)
            st_new = st_new + jnp.where(in_head, upd[cols], 0.0)
        st_scr[...] = st_new

    @pl.when(t == pl.num_programs(1) - 1)
    def _():
        st_ref[...] = st_scr[...]


def _gla(gq, gk, gv, la, gr, gn, st0, *, tt):
    b, t, kw = gq.shape
    vw = gv.shape[-1]
    tok = lambda w: pl.BlockSpec((None, tt, w), lambda bi, ti: (bi, ti, 0))
    state = pl.BlockSpec((None, GLA_DV, kw), lambda bi, ti: (bi, 0, 0))
    return pl.pallas_call(
        functools.partial(_gla_kernel, n_chunks=tt // GLA_CHUNK),
        grid=(b, t // tt),
        in_specs=[tok(kw), tok(kw), tok(vw), tok(kw), tok(vw), _resident(gn), state],
        out_specs=[tok(vw), state],
        out_shape=[jax.ShapeDtypeStruct((b, t, vw), BF16), jax.ShapeDtypeStruct((b, GLA_DV, kw), F32)],
        scratch_shapes=[pltpu.VMEM((GLA_DV, kw), F32)],
        compiler_params=_params("parallel", "arbitrary"),
        name="gla",
    )(gq, gk, gv, la, gr, gn, st0)


def _mem_kernel(q_ref, mk_ref, mv_ref, o_ref):
    head_major = len(mk_ref.shape) == 3
    for h in range(MEM_HEADS):
        cols = slice(h * MEM_DIM, (h + 1) * MEM_DIM)
        kh = mk_ref[:, h, :] if head_major else mk_ref[:, cols]
        vh = mv_ref[:, h, :] if head_major else mv_ref[:, cols]
        s = _dot_nt(q_ref[:, cols], kh.astype(BF16)) * MEM_DIM ** -0.5
        e = jnp.exp(s - jnp.max(s, axis=-1, keepdims=True))
        p = e / jnp.sum(e, axis=-1, keepdims=True)
        o_ref[:, cols] = _dot(p.astype(BF16), vh.astype(BF16)).astype(o_ref.dtype)


def _mem_attention(q, mk, mv, *, tq):
    b, t, w = q.shape
    tile = pl.BlockSpec((None, tq, w), lambda bi, qi: (bi, qi, 0))
    mem = pl.BlockSpec((None,) + mk.shape[1:], lambda bi, qi: (bi,) + (0,) * (mk.ndim - 1))
    return pl.pallas_call(
        _mem_kernel,
        grid=(b, t // tq),
        in_specs=[tile, mem, mem],
        out_specs=tile,
        out_shape=jax.ShapeDtypeStruct((b, t, w), BF16),
        compiler_params=_params("parallel", "arbitrary"),
        name="mem_attention",
    )(q, mk, mv)


def _merge_kernel(x_ref, osb_ref, ogla_ref, omem_ref, gmix_ref, wgt_ref, wbr_ref, wout_ref,
                  gffn_ref, wr_hi_ref, wr_lo_ref, br_ref, h_ref, ht_ref, idx_ref, wts_ref):
    x = x_ref[...]
    d = x.shape[-1]
    xn = _rmsnorm(x, gmix_ref[...]).astype(BF16)
    mixed = None
    for n, o_ref in enumerate((osb_ref, ogla_ref, omem_ref)):
        gate = jax.nn.sigmoid(_dot(xn, wgt_ref[:, n * d:(n + 1) * d]))
        term = gate * _dot(o_ref[...], wbr_ref[n])
        mixed = term if mixed is None else mixed + term
    h = x + _dot(mixed.astype(BF16), wout_ref[...])
    h_ref[...] = h
    hn = _rmsnorm(h, gffn_ref[...])
    ht_ref[...] = hn.astype(BF16)

    hn_hi, hn_lo = _split_bf16(hn)
    logits = (_dot(hn_hi, wr_hi_ref[...]) + _dot(hn_lo, wr_hi_ref[...])
              + _dot(hn_hi, wr_lo_ref[...]) + br_ref[...])
    lane = lax.broadcasted_iota(jnp.int32, logits.shape, 1)
    rmax = lambda a: jnp.max(a, axis=-1, keepdims=True)
    rmin = lambda a: jnp.min(a, axis=-1, keepdims=True)
    rsum = lambda a: jnp.sum(a, axis=-1, keepdims=True)

    lc = jnp.where(lane < N_GROUPS, logits, NEG_BIG)
    mc = rmax(lc)
    grp = rmin(jnp.where(lc == mc, lane, LANES))
    p_grp = 1.0 / rsum(jnp.exp(lc - mc))

    lo = N_GROUPS + grp * EXPERTS_PER_GROUP
    in_grp = (lane >= lo) & (lane < lo + EXPERTS_PER_GROUP)
    lf = jnp.where(in_grp, logits, NEG_BIG)
    ef = jnp.exp(lf - rmax(lf))
    pf = jnp.where(in_grp, ef / rsum(ef), -1.0)
    v1 = rmax(pf)
    i1 = rmin(jnp.where(pf == v1, lane, LANES))
    pf2 = jnp.where(lane == i1, -1.0, pf)
    v2 = rmax(pf2)
    i2 = rmin(jnp.where(pf2 == v2, lane, LANES))
    tot = v1 + v2
    idx_ref[...] = jnp.where(lane == 0, i1 - N_GROUPS, jnp.where(lane == 1, i2 - N_GROUPS, 0))
    wts_ref[...] = jnp.where(lane == 0, p_grp * (v1 / tot), jnp.where(lane == 1, p_grp * (v2 / tot), 0.0))


def _merge(x, osb, ogla, omem, gmix, wgt, wbr, wout, gffn, wr_hi, wr_lo, br, *, tm):
    n, d = x.shape
    bw = osb.shape[1]
    row = lambda w: pl.BlockSpec((tm, w), lambda i: (i, 0))
    return pl.pallas_call(
        _merge_kernel,
        grid=(n // tm,),
        in_specs=[row(d), row(bw), row(bw), row(bw), _resident(gmix), _resident(wgt), _resident(wbr),
                  _resident(wout), _resident(gffn), _resident(wr_hi), _resident(wr_lo), _resident(br)],
        out_specs=[row(d), row(d), row(LANES), row(LANES)],
        out_shape=[jax.ShapeDtypeStruct((n, d), F32), jax.ShapeDtypeStruct((n, d), BF16),
                   jax.ShapeDtypeStruct((n, LANES), jnp.int32), jax.ShapeDtypeStruct((n, LANES), F32)],
        compiler_params=_params("parallel"),
        name="merge_route",
    )(x, osb, ogla, omem, gmix, wgt, wbr, wout, gffn, wr_hi, wr_lo, br)


def _expert_kernel(blk_e_ref, n_used_ref, x_ref, wg_ref, wu_ref, wd_ref, y_ref):
    i = pl.program_id(0)

    @pl.when(i < n_used_ref[0])
    def _():
        x = x_ref[...]
        g = _dot(x, wg_ref[...])
        u = _dot(x, wu_ref[...])
        a = (g * jax.nn.sigmoid(g) * u).astype(BF16)
        y_ref[...] = _dot(a, wd_ref[...])

    @pl.when(i >= n_used_ref[0])
    def _():
        y_ref[...] = jnp.zeros_like(y_ref)


def _experts(xb, blk_e, n_used, wg, wu, wd, *, tm):
    p, d = xb.shape
    de = wg.shape[-1]
    grid_spec = pltpu.PrefetchScalarGridSpec(
        num_scalar_prefetch=2,
        grid=(p // tm,),
        in_specs=[pl.BlockSpec((tm, d), lambda i, be, nu: (i, 0)),
                  pl.BlockSpec((None, d, de), lambda i, be, nu: (be[i], 0, 0)),
                  pl.BlockSpec((None, d, de), lambda i, be, nu: (be[i], 0, 0)),
                  pl.BlockSpec((None, de, d), lambda i, be, nu: (be[i], 0, 0))],
        out_specs=pl.BlockSpec((tm, d), lambda i, be, nu: (i, 0)),
    )
    return pl.pallas_call(
        _expert_kernel,
        grid_spec=grid_spec,
        out_shape=jax.ShapeDtypeStruct((p, d), F32),
        compiler_params=_params("arbitrary"),
        name="experts",
    )(blk_e, n_used, xb, wg, wu, wd)


def _final_kernel(h_ref, y0_ref, y1_ref, wts_ref, g_ref, o_ref):
    w = wts_ref[...]
    h = h_ref[...] + (y0_ref[...] * w[:, 0:1] + y1_ref[...] * w[:, 1:2])
    o_ref[...] = _rmsnorm(h, g_ref[...])


def _final(h, y0, y1, wts, g, *, tm):
    n, d = h.shape
    row = lambda w: pl.BlockSpec((tm, w), lambda i: (i, 0))
    return pl.pallas_call(
        _final_kernel,
        grid=(n // tm,),
        in_specs=[row(d), row(d), row(d), row(LANES), _resident(g)],
        out_specs=row(d),
        out_shape=jax.ShapeDtypeStruct((n, d), F32),
        compiler_params=_params("parallel"),
        name="final_norm",
    )(h, y0, y1, wts, g)


def _dispatch_plan(idx, tm):
    n = idx.shape[0]
    a = n * TOP_K
    flat_e = idx.reshape(a)
    onehot = (flat_e[:, None] == jnp.arange(N_EXPERTS, dtype=jnp.int32)[None, :]).astype(jnp.int32)
    ranks = jnp.cumsum(onehot, axis=0)
    counts = ranks[-1]
    rank = jnp.sum(onehot * ranks, axis=1) - 1
    padded = (counts + tm - 1) // tm * tm
    pend = jnp.cumsum(padded)
    pstart = pend - padded
    dest = pstart[flat_e] + rank
    n_blk = -(-(a + N_EXPERTS * (tm - 1)) // tm)
    blk_start = jnp.arange(n_blk, dtype=jnp.int32) * tm
    blk_e = jnp.minimum(jnp.sum((pend[None, :] <= blk_start[:, None]).astype(jnp.int32), axis=1),
                        N_EXPERTS - 1).astype(jnp.int32)
    n_used = (pend[-1] // tm).astype(jnp.int32).reshape(1)
    return dest.astype(jnp.int32), blk_e, n_used, n_blk


def _moe(ht, idx, wts, h, wg, wu, wd, g_final, *, tm_e, tm):
    n = ht.shape[0]
    dest, blk_e, n_used, n_blk = _dispatch_plan(idx[:, :TOP_K], tm_e)
    tok = jnp.zeros((n_blk * tm_e,), jnp.int32).at[dest].set(jnp.repeat(jnp.arange(n, dtype=jnp.int32), TOP_K))
    yb = _experts(ht[tok], blk_e, n_used, wg, wu, wd, tm=tm_e)
    pos = dest.reshape(n, TOP_K)
    return _final(h, yb[pos[:, 0]], yb[pos[:, 1]], wts, g_final, tm=tm)


def _layer(x, sb_past, gla_state, mem_kv, wts, *, tm, tq_sb, tt_gla, tq_mem, tm_e):
    b, t, d = x.shape
    n = b * t
    xf = x.reshape(n, d)
    sq, skb, svb, sk, sv, gq, gk, gv, la, gr, mq = _proj(
        xf, wts["g_mix"], wts["w_proj"], wts["wa2"], wts["ba"], wts["w_kvt"] if sb_past is None else None,
        tm=tm, seq_len=t, key_block=tq_sb)
    r3 = lambda a: a.reshape(b, t, a.shape[-1])
    if sb_past is None:
        o_sb = _sb_prompt(r3(sq), skb, svb, tq=tq_sb)
    else:
        o_sb = _sb_sample(r3(sq), r3(skb), r3(svb), sb_past[0], sb_past[1], tk_past=256)
    o_gla, st = _gla(r3(gq), r3(gk), r3(gv), r3(la), r3(gr), wts["g_gla"], gla_state, tt=tt_gla)
    o_mem = _mem_attention(r3(mq), mem_kv[0], mem_kv[1], tq=tq_mem)
    h, ht, idx, rw = _merge(xf, o_sb.reshape(n, -1), o_gla.reshape(n, -1), o_mem.reshape(n, -1),
                            wts["g_mix"], wts["w_gt"], wts["w_br"], wts["w_out"], wts["g_ffn"],
                            wts["wr_hi"], wts["wr_lo"], wts["b_r"], tm=tm)
    y = _moe(ht, idx, rw, h, wts["wg"], wts["wu"], wts["wd"], wts["g_final"], tm_e=tm_e, tm=tm)
    return y.reshape(b, t, d), sk, sv, st


def _state_to_t(s):
    b = s.shape[0]
    return s.transpose(0, 3, 1, 2).reshape(b, GLA_DV, GLA_HEADS * GLA_DK)


def _state_from_t(st):
    b = st.shape[0]
    return st.reshape(b, GLA_DV, GLA_HEADS, GLA_DK).transpose(0, 2, 3, 1)


def kernel(x_prompt, x_sample, mem_prompt, cache_sb_k, cache_sb_v, state_gla, cache_mem_k, cache_mem_v,
           norm_mix, w_in, w_gla_a2, b_gla_a, gla_norm, norm_mem, w_mem_kv, w_branch, w_out,
           norm_ffn, w_coarse, b_coarse, w_fine, b_fine, w_e_gate, w_e_up, w_e_down, norm_final):
    assert w_in.shape[0] == 1, "single-layer model"
    d = x_prompt.shape[-1]
    bp, tp, _ = x_prompt.shape
    bs, ts, _ = x_sample.shape
    bw = d // 2

    w = w_in[0]
    offs = [0]
    for s in (bw, bw, bw, 256, 256, bw, GLA_RANK, bw, bw, N_BRANCH * d):
        offs.append(offs[-1] + s)
    col = lambda i: w[:, offs[i]:offs[i + 1]]
    w_proj = jnp.concatenate(
        [col(0), col(1), col(2), col(3), col(4), col(5), col(7), col(8),
         jnp.pad(col(6), ((0, 0), (0, LANES - GLA_RANK)))], axis=1).astype(BF16)
    assert w_proj.shape[1] == _PROJ_W
    row2 = lambda a: a.reshape(1, -1)
    n_route = N_GROUPS + N_EXPERTS
    w_route = jnp.concatenate([w_coarse[0], w_fine[0].transpose(1, 0, 2).reshape(d, N_EXPERTS)], axis=1)
    w_route = jnp.pad(w_route, ((0, 0), (0, LANES - n_route)))
    wr_hi = w_route.astype(BF16)
    wr_lo = (w_route - wr_hi.astype(F32)).astype(BF16)
    b_route = jnp.pad(jnp.concatenate([b_coarse[0], b_fine[0].reshape(-1)]), (0, LANES - n_route)).reshape(1, LANES)
    wts = dict(
        g_mix=row2(norm_mix[0]), w_proj=w_proj,
        w_kvt=jnp.concatenate([col(1), col(2)], axis=1).T.astype(BF16),
        wa2=jnp.pad(w_gla_a2[0], ((0, LANES - GLA_RANK), (0, 0))).astype(BF16), ba=row2(b_gla_a[0]),
        g_gla=row2(gla_norm[0]),
        w_gt=col(9).astype(BF16), w_br=w_branch[0].astype(BF16), w_out=w_out[0].astype(BF16),
        g_ffn=row2(norm_ffn[0]), wr_hi=wr_hi, wr_lo=wr_lo, b_r=b_route,
        wg=w_e_gate[0].astype(BF16), wu=w_e_up[0].astype(BF16), wd=w_e_down[0].astype(BF16),
        g_final=row2(norm_final),
    )

    m = mem_prompt.shape[1]
    mk, mv = _memkv(mem_prompt.reshape(bp * m, d), row2(norm_mem[0]), w_mem_kv[0].astype(BF16), tm=min(512, bp * m))
    st0 = jnp.zeros((bp, GLA_DV, GLA_HEADS * GLA_DK), F32)
    y_p, sk_p, sv_p, st_p = _layer(x_prompt, None, st0, (mk.reshape(bp, m, bw), mv.reshape(bp, m, bw)), wts,
                                   tm=min(512, bp * tp), tq_sb=min(256, tp), tt_gla=min(256, tp),
                                   tq_mem=min(512, tp), tm_e=256)

    past = (cache_sb_k[0].transpose(0, 2, 3, 1), cache_sb_v[0].transpose(0, 2, 3, 1))
    y_s, sk_s, sv_s, st_s = _layer(x_sample, past, _state_to_t(state_gla[0]),
                                   (cache_mem_k[0], cache_mem_v[0]), wts,
                                   tm=min(512, bs * ts), tq_sb=ts, tt_gla=ts, tq_mem=ts, tm_e=256)

    hd = lambda a, bb, tt: a.reshape(1, bb, tt, SB_HEADS, SB_DIM)
    return (y_p, y_s,
            sk_p.transpose(0, 3, 1, 2)[None], sv_p.transpose(0, 3, 1, 2)[None],
            _state_from_t(st_p)[None],
            mk.reshape(1, bp, m, MEM_HEADS, MEM_DIM), mv.reshape(1, bp, m, MEM_HEADS, MEM_DIM),
            hd(sk_s, bs, ts), hd(sv_s, bs, ts),
            _state_from_t(st_s)[None])
```

```python
import functools

import jax
import jax.numpy as jnp
from jax import lax
from jax.experimental import pallas as pl
from jax.experimental.pallas import tpu as pltpu

F32 = jnp.float32
BF16 = jnp.bfloat16
EPS = 1e-6

SB_HEADS = 8
SB_DIM = 64
GLA_HEADS = 4
GLA_DK = 64
GLA_DV = 128
GLA_RANK = 16
GLA_TAU = 16.0
GLA_CHUNK = 64
MEM_HEADS = 4
MEM_DIM = 128
N_BRANCH = 3
N_GROUPS = 4
EXPERTS_PER_GROUP = 8
N_EXPERTS = N_GROUPS * EXPERTS_PER_GROUP
TOP_K = 2

LANES = 128
VMEM_LIMIT = 56 * 1024 * 1024
NEG_BIG = -1e30
_LOG_F32_ZERO = -110.0

_NT = (((1,), (1,)), ((), ()))


def _dot(a, b):
    return jnp.dot(a, b, preferred_element_type=F32)


def _dot_nt(a, b):
    return lax.dot_general(a, b, _NT, preferred_element_type=F32)


def _split_bf16(x):
    hi = x.astype(BF16)
    lo = (x - hi.astype(F32)).astype(BF16)
    return hi, lo


def _rmsnorm(x, g):
    return x * lax.rsqrt(jnp.mean(x * x, axis=-1, keepdims=True) + EPS) * g


def _log_sigmoid(x):
    return jnp.minimum(x, 0.0) - jnp.log1p(jnp.exp(-jnp.abs(x)))


def _resident(a):
    return pl.BlockSpec(a.shape, lambda *_: (0,) * a.ndim, pipeline_mode=pl.Buffered(1))


def _params(*sem):
    return pltpu.CompilerParams(dimension_semantics=sem, vmem_limit_bytes=VMEM_LIMIT)


_C_SQ, _C_SK, _C_SV = 0, 512, 1024
_C_GQ, _C_GK, _C_GV = 1536, 1792, 2048
_C_GR, _C_MQ, _C_GLR = 2560, 3072, 3584
_PROJ_W = 3712


def _proj_kernel(*refs, kv_transposed, key_block):
    if kv_transposed:
        x_ref, g_ref, w_ref, wa2_ref, ba_ref, wkvt_ref = refs[:6]
    else:
        x_ref, g_ref, w_ref, wa2_ref, ba_ref = refs[:5]
    sq_ref, skb_ref, svb_ref, sk_ref, sv_ref, gq_ref, gk_ref, gv_ref, la_ref, gr_ref, mq_ref = refs[-11:]
    xn = _rmsnorm(x_ref[...], g_ref[...]).astype(BF16)

    def seg(start, width):
        return _dot(xn, w_ref[:, start:start + width])

    sq_ref[...] = (seg(_C_SQ, 512) * SB_DIM ** -0.5).astype(BF16)
    if kv_transposed:
        tm = xn.shape[0]
        for half, (bf_ref, heads_ref) in enumerate(((skb_ref, sk_ref), (svb_ref, sv_ref))):
            kv_t = _dot_nt(wkvt_ref[half * 512:(half + 1) * 512, :], xn)
            heads_ref[...] = kv_t.reshape(SB_HEADS, SB_DIM, tm)
            for j in range(tm // key_block):
                bf_ref[j] = kv_t[:, j * key_block:(j + 1) * key_block].astype(BF16)
    else:
        for c0, bf_ref, heads_ref in ((_C_SK, skb_ref, sk_ref), (_C_SV, svb_ref, sv_ref)):
            kv = seg(c0, 512)
            bf_ref[...] = kv.astype(BF16)
            for h in range(SB_HEADS):
                heads_ref[:, h, :] = kv[:, h * SB_DIM:(h + 1) * SB_DIM]
    gq_ref[...] = seg(_C_GQ, 256) * GLA_DK ** -0.5
    gk_ref[...] = seg(_C_GK, 256)
    gv_ref[...] = seg(_C_GV, 512).astype(BF16)
    gr_ref[...] = seg(_C_GR, 512)
    mq_ref[...] = seg(_C_MQ, 512).astype(BF16)
    glr = seg(_C_GLR, LANES).astype(BF16)
    la_ref[...] = _log_sigmoid(_dot(glr, wa2_ref[...]) + ba_ref[...]) * (1.0 / GLA_TAU)


def _proj(x, g, w_proj, wa2, ba, w_kvt, *, tm, seq_len, key_block):
    n, d = x.shape
    kv_transposed = w_kvt is not None
    row = lambda w: pl.BlockSpec((tm, w), lambda i: (i, 0))
    flat = lambda w, dt: (row(w), jax.ShapeDtypeStruct((n, w), dt))
    if kv_transposed:
        nb, per_b = n // seq_len, seq_len // tm
        kv_bf = (pl.BlockSpec((None, tm // key_block, 512, key_block), lambda i: (i // per_b, i % per_b, 0, 0)),
                 jax.ShapeDtypeStruct((nb, seq_len // key_block, 512, key_block), BF16))
        kv_f32 = (pl.BlockSpec((None, SB_HEADS, SB_DIM, tm), lambda i: (i // per_b, 0, 0, i % per_b)),
                  jax.ShapeDtypeStruct((nb, SB_HEADS, SB_DIM, seq_len), F32))
    else:
        kv_bf = flat(512, BF16)
        kv_f32 = (pl.BlockSpec((tm, SB_HEADS, SB_DIM), lambda i: (i, 0, 0)),
                  jax.ShapeDtypeStruct((n, SB_HEADS, SB_DIM), F32))
    outs = [flat(512, BF16), kv_bf, kv_bf, kv_f32, kv_f32,
            flat(256, F32), flat(256, F32), flat(512, BF16), flat(256, F32), flat(512, F32), flat(512, BF16)]
    args = [x, g, w_proj, wa2, ba] + ([w_kvt] if kv_transposed else [])
    return pl.pallas_call(
        functools.partial(_proj_kernel, kv_transposed=kv_transposed, key_block=key_block),
        grid=(n // tm,),
        in_specs=[row(d)] + [_resident(a) for a in args[1:]],
        out_specs=[o[0] for o in outs],
        out_shape=[o[1] for o in outs],
        compiler_params=_params("parallel"),
        name="proj",
    )(*args)


def _memkv_kernel(x_ref, g_ref, w_ref, mk_ref, mv_ref):
    xn = _rmsnorm(x_ref[...], g_ref[...]).astype(BF16)
    half = mk_ref.shape[-1]
    mk_ref[...] = _dot(xn, w_ref[:, :half])
    mv_ref[...] = _dot(xn, w_ref[:, half:])


def _memkv(x, g, w, tm):
    n, d = x.shape
    half = w.shape[1] // 2
    row = lambda wd: pl.BlockSpec((tm, wd), lambda i: (i, 0))
    return pl.pallas_call(
        _memkv_kernel,
        grid=(n // tm,),
        in_specs=[row(d), _resident(g), _resident(w)],
        out_specs=[row(half), row(half)],
        out_shape=[jax.ShapeDtypeStruct((n, half), F32)] * 2,
        compiler_params=_params("parallel"),
        name="memkv",
    )(x, g, w)


def _suffix_matrix(tk):
    j = lax.broadcasted_iota(jnp.int32, (2 * tk, tk), 0)
    s = lax.broadcasted_iota(jnp.int32, (2 * tk, tk), 1)
    return ((j > s) & ((j < tk) | (j > s + tk))).astype(BF16)


def _sb_weights(z, carry, mask):
    lz = jnp.minimum(z, 0.0) - jnp.log(1.0 + jnp.exp(-jnp.abs(z)))
    l1 = lz - z
    if mask is not None:
        l1 = jnp.where(mask, l1, 0.0)
    hi, lo = _split_bf16(l1)
    cum = _dot(jnp.concatenate([hi, lo], axis=1), _suffix_matrix(z.shape[1]))
    p = jnp.exp(lz + cum + carry)
    if mask is not None:
        p = jnp.where(mask, p, 0.0)
    return p.astype(BF16), carry + cum[:, :1] + l1[:, :1]


def _fold_descending(n_blocks, alive, fold_block):
    def cond(state):
        i, live = state
        return (i < n_blocks) & live

    def body(state):
        i, _ = state
        return i + 1, fold_block(n_blocks - 1 - i)

    return lax.while_loop(cond, body, (jnp.int32(0), alive))[1]


def _sb_prompt_kernel(q_ref, k_ref, v_ref, o_ref, acc_ref, car_ref, *, tq):
    qi = pl.program_id(2)
    q = q_ref[...]
    lane = lax.broadcasted_iota(jnp.int32, (tq, LANES), 1)
    q_heads = (jnp.where(lane < SB_DIM, q, jnp.zeros_like(q)),
               jnp.where(lane >= SB_DIM, q, jnp.zeros_like(q)))
    acc_ref[...] = jnp.zeros_like(acc_ref)
    car_ref[...] = jnp.zeros_like(car_ref)

    def block(kb, mask):
        kt = k_ref[kb]
        vt = v_ref[kb]
        top = None
        for h in range(2):
            p, carry = _sb_weights(_dot(q_heads[h], kt), car_ref[h], mask)
            acc_ref[h] += _dot_nt(p, vt)
            car_ref[h] = carry
            m = jnp.max(carry)
            top = m if top is None else jnp.maximum(top, m)
        return top > _LOG_F32_ZERO

    r = lax.broadcasted_iota(jnp.int32, (tq, tq), 0)
    c = lax.broadcasted_iota(jnp.int32, (tq, tq), 1)
    alive = block(qi, c < r)
    _fold_descending(qi, alive, lambda kb: block(kb, None))
    o_ref[...] = jnp.where(lane < SB_DIM, acc_ref[0], acc_ref[1]).astype(o_ref.dtype)


def _sb_prompt(q, k_t, v_t, *, tq):
    b, t, w = q.shape
    assert k_t.shape == (b, t // tq, w, tq)
    seq = pl.BlockSpec((None, t // tq, LANES, tq), lambda bi, p, qi: (bi, 0, p, 0))
    tile = pl.BlockSpec((None, tq, LANES), lambda bi, p, qi: (bi, qi, p))
    return pl.pallas_call(
        functools.partial(_sb_prompt_kernel, tq=tq),
        grid=(b, w // LANES, t // tq),
        in_specs=[tile, seq, seq],
        out_specs=tile,
        out_shape=jax.ShapeDtypeStruct((b, t, w), BF16),
        scratch_shapes=[pltpu.VMEM((2, tq, LANES), F32), pltpu.VMEM((2, tq, 1), F32)],
        compiler_params=_params("parallel", "parallel", "arbitrary"),
        name="sb_prompt",
    )(q, k_t, v_t)


def _sb_sample_kernel(q_ref, k_ref, v_ref, pk_ref, pv_ref, o_ref, acc_ref, car_ref, *, tk_past):
    t = q_ref.shape[0]
    past_len = pk_ref.shape[-1]
    acc_ref[...] = jnp.zeros_like(acc_ref)
    car_ref[...] = jnp.zeros_like(car_ref)
    head_cols = [slice(h * SB_DIM, (h + 1) * SB_DIM) for h in range(SB_HEADS)]

    def fold_all_heads(logits, weighted_values, mask):
        z = jnp.concatenate([logits(h, q_ref[:, head_cols[h]]) for h in range(SB_HEADS)], axis=0)
        p, carry = _sb_weights(z, car_ref[...], mask)
        car_ref[...] = carry
        for h in range(SB_HEADS):
            acc_ref[h] += weighted_values(h, p[h * t:(h + 1) * t])
        return jnp.max(carry) > _LOG_F32_ZERO

    r = lax.broadcasted_iota(jnp.int32, (SB_HEADS * t, t), 0)
    c = lax.broadcasted_iota(jnp.int32, (SB_HEADS * t, t), 1)
    alive = fold_all_heads(lambda h, qh: _dot_nt(qh, k_ref[:, head_cols[h]]),
                           lambda h, ph: _dot(ph, v_ref[:, head_cols[h]]),
                           c < (r & (t - 1)))

    def past_block(kb_idx):
        pos = pl.ds(pl.multiple_of(kb_idx * tk_past, tk_past), tk_past)
        return fold_all_heads(lambda h, qh: _dot(qh, pk_ref[h, :, pos].astype(BF16)),
                              lambda h, ph: _dot_nt(ph, pv_ref[h, :, pos].astype(BF16)), None)

    _fold_descending(past_len // tk_past, alive, past_block)
    for h in range(SB_HEADS):
        o_ref[:, head_cols[h]] = acc_ref[h].astype(o_ref.dtype)


def _sb_sample(q, k, v, past_k, past_v, *, tk_past):
    b, t, w = q.shape
    assert t & (t - 1) == 0, "query chunk length must be a power of two"
    cur = pl.BlockSpec((None, t, w), lambda bi: (bi, 0, 0))
    past = pl.BlockSpec((None,) + past_k.shape[1:], lambda bi: (bi, 0, 0, 0))
    return pl.pallas_call(
        functools.partial(_sb_sample_kernel, tk_past=tk_past),
        grid=(b,),
        in_specs=[cur, cur, cur, past, past],
        out_specs=cur,
        out_shape=jax.ShapeDtypeStruct((b, t, w), BF16),
        scratch_shapes=[pltpu.VMEM((SB_HEADS, t, SB_DIM), F32), pltpu.VMEM((SB_HEADS * t, 1), F32)],
        compiler_params=_params("parallel"),
        name="sb_sample",
    )(q, k, v, past_k, past_v)


def _gla_kernel(gq_ref, gk_ref, gv_ref, la_ref, gr_ref, gn_ref, s0_ref, o_ref, st_ref, st_scr, *, n_chunks):
    c_len = GLA_CHUNK
    tt = n_chunks * c_len
    kw = GLA_HEADS * GLA_DK
    vw = GLA_HEADS * GLA_DV
    t = pl.program_id(1)

    @pl.when(t == 0)
    def _():
        st_scr[...] = s0_ref[...]

    r = lax.broadcasted_iota(jnp.int32, (tt, tt), 0)
    c = lax.broadcasted_iota(jnp.int32, (tt, tt), 1)
    chunk_bits = c_len.bit_length() - 1
    same_chunk = (r >> chunk_bits) == (c >> chunk_bits)
    tril = (same_chunk & (r >= c)).astype(BF16)
    ones = same_chunk.astype(BF16)
    la_hi, la_lo = _split_bf16(la_ref[...])
    b = _dot(tril, la_hi) + _dot(tril, la_lo)
    b_tot = _dot(ones, la_hi) + _dot(ones, la_lo)
    gk = gk_ref[...]
    qd = gq_ref[...] * jnp.exp(b)
    kd = (gk * jnp.exp(-b)).astype(BF16)
    ke = (gk * jnp.exp(b_tot - b)).astype(BF16)
    decay = jnp.exp(b_tot)

    rc = lax.broadcasted_iota(jnp.int32, (GLA_HEADS * c_len, c_len), 0)
    cc = lax.broadcasted_iota(jnp.int32, (GLA_HEADS * c_len, c_len), 1)
    causal = (rc & (c_len - 1)) >= cc
    eye = (lax.broadcasted_iota(jnp.int32, (vw, vw), 0)
           == lax.broadcasted_iota(jnp.int32, (vw, vw), 1)).astype(BF16)
    lane_q = lax.broadcasted_iota(jnp.int32, (c_len, kw), 1)
    lane_s = lax.broadcasted_iota(jnp.int32, (GLA_DV, kw), 1)

    for ci in range(n_chunks):
        rows = slice(ci * c_len, (ci + 1) * c_len)
        qd_c = qd[rows]
        qs = jnp.concatenate(
            [jnp.where((lane_q >= h * GLA_DK) & (lane_q < (h + 1) * GLA_DK), qd_c, 0.0)
             for h in range(GLA_HEADS)], axis=0).astype(BF16)
        att = jnp.where(causal, _dot_nt(qs, kd[rows]), 0.0).astype(BF16)
        st = st_scr[...]
        o_state = _dot_nt(qs, st.astype(BF16))
        v = gv_ref[rows, :]
        v_t = _dot_nt(eye, v).astype(BF16)
        upd = _dot(v_t, ke[rows])
        st_new = st * decay[ci * c_len:ci * c_len + 1, :]
        for h in range(GLA_HEADS):
            hrows = slice(h * c_len, (h + 1) * c_len)
            cols = slice(h * GLA_DV, (h + 1) * GLA_DV)
            o = _dot(att[hrows], v[:, cols]) + o_state[hrows]
            on = _rmsnorm(o, gn_ref[:, cols])
            gr = gr_ref[rows, cols]
            o_ref[rows, cols] = (on * (gr * jax.nn.sigmoid(gr))).astype(o_ref.dtype)
            in_head = (lane_s >= h * GLA_DK) & (lane_s < (h + 1) * GLA_DK)
            st_new = st_new + jnp.where(in_head, upd[cols], 0.0)
        st_scr[...] = st_new

    @pl.when(t == pl.num_programs(1) - 1)
    def _():
        st_ref[...] = st_scr[...]


def _gla(gq, gk, gv, la, gr, gn, st0, *, tt):
    b, t, kw = gq.shape
    vw = gv.shape[-1]
    tok = lambda w: pl.BlockSpec((None, tt, w), lambda bi, ti: (bi, ti, 0))
    state = pl.BlockSpec((None, GLA_DV, kw), lambda bi, ti: (bi, 0, 0))
    return pl.pallas_call(
        functools.partial(_gla_kernel, n_chunks=tt // GLA_CHUNK),
        grid=(b, t // tt),
        in_specs=[tok(kw), tok(kw), tok(vw), tok(kw), tok(vw), _resident(gn), state],
        out_specs=[tok(vw), state],
        out_shape=[jax.ShapeDtypeStruct((b, t, vw), BF16), jax.ShapeDtypeStruct((b, GLA_DV, kw), F32)],
        scratch_shapes=[pltpu.VMEM((GLA_DV, kw), F32)],
        compiler_params=_params("parallel", "arbitrary"),
        name="gla",
    )(gq, gk, gv, la, gr, gn, st0)


def _mem_kernel(q_ref, mk_ref, mv_ref, o_ref):
    head_major = len(mk_ref.shape) == 3
    for h in range(MEM_HEADS):
        cols = slice(h * MEM_DIM, (h + 1) * MEM_DIM)
        kh = mk_ref[:, h, :] if head_major else mk_ref[:, cols]
        vh = mv_ref[:, h, :] if head_major else mv_ref[:, cols]
        s = _dot_nt(q_ref[:, cols], kh.astype(BF16)) * MEM_DIM ** -0.5
        e = jnp.exp(s - jnp.max(s, axis=-1, keepdims=True))
        p = e / jnp.sum(e, axis=-1, keepdims=True)
        o_ref[:, cols] = _dot(p.astype(BF16), vh.astype(BF16)).astype(o_ref.dtype)


def _mem_attention(q, mk, mv, *, tq):
    b, t, w = q.shape
    tile = pl.BlockSpec((None, tq, w), lambda bi, qi: (bi, qi, 0))
    mem = pl.BlockSpec((None,) + mk.shape[1:], lambda bi, qi: (bi,) + (0,) * (mk.ndim - 1))
    return pl.pallas_call(
        _mem_kernel,
        grid=(b, t // tq),
        in_specs=[tile, mem, mem],
        out_specs=tile,
        out_shape=jax.ShapeDtypeStruct((b, t, w), BF16),
        compiler_params=_params("parallel", "arbitrary"),
        name="mem_attention",
    )(q, mk, mv)


def _merge_kernel(x_ref, osb_ref, ogla_ref, omem_ref, gmix_ref, wgt_ref, wbr_ref, wout_ref,
                  gffn_ref, wr_hi_ref, wr_lo_ref, br_ref, h_ref, ht_ref, idx_ref, wts_ref):
    x = x_ref[...]
    d = x.shape[-1]
    xn = _rmsnorm(x, gmix_ref[...]).astype(BF16)
    mixed = None
    for n, o_ref in enumerate((osb_ref, ogla_ref, omem_ref)):
        gate = jax.nn.sigmoid(_dot(xn, wgt_ref[:, n * d:(n + 1) * d]))
        term = gate * _dot(o_ref[...], wbr_ref[n])
        mixed = term if mixed is None else mixed + term
    h = x + _dot(mixed.astype(BF16), wout_ref[...])
    h_ref[...] = h
    hn = _rmsnorm(h, gffn_ref[...])
    ht_ref[...] = hn

    hn_hi, hn_lo = _split_bf16(hn)
    logits = (_dot(hn_hi, wr_hi_ref[...]) + _dot(hn_lo, wr_hi_ref[...])
              + _dot(hn_hi, wr_lo_ref[...]) + br_ref[...])
    lane = lax.broadcasted_iota(jnp.int32, logits.shape, 1)
    rmax = lambda a: jnp.max(a, axis=-1, keepdims=True)
    rmin = lambda a: jnp.min(a, axis=-1, keepdims=True)
    rsum = lambda a: jnp.sum(a, axis=-1, keepdims=True)

    lc = jnp.where(lane < N_GROUPS, logits, NEG_BIG)
    mc = rmax(lc)
    grp = rmin(jnp.where(lc == mc, lane, LANES))
    p_grp = 1.0 / rsum(jnp.exp(lc - mc))

    lo = N_GROUPS + grp * EXPERTS_PER_GROUP
    in_grp = (lane >= lo) & (lane < lo + EXPERTS_PER_GROUP)
    lf = jnp.where(in_grp, logits, NEG_BIG)
    ef = jnp.exp(lf - rmax(lf))
    pf = jnp.where(in_grp, ef / rsum(ef), -1.0)
    v1 = rmax(pf)
    i1 = rmin(jnp.where(pf == v1, lane, LANES))
    pf2 = jnp.where(lane == i1, -1.0, pf)
    v2 = rmax(pf2)
    i2 = rmin(jnp.where(pf2 == v2, lane, LANES))
    tot = v1 + v2
    idx_ref[...] = jnp.where(lane == 0, i1 - N_GROUPS, jnp.where(lane == 1, i2 - N_GROUPS, 0))
    wts_ref[...] = jnp.where(lane == 0, p_grp * (v1 / tot), jnp.where(lane == 1, p_grp * (v2 / tot), 0.0))


def _merge(x, osb, ogla, omem, gmix, wgt, wbr, wout, gffn, wr_hi, wr_lo, br, *, tm):
    n, d = x.shape
    bw = osb.shape[1]
    row = lambda w: pl.BlockSpec((tm, w), lambda i: (i, 0))
    return pl.pallas_call(
        _merge_kernel,
        grid=(n // tm,),
        in_specs=[row(d), row(bw), row(bw), row(bw), _resident(gmix), _resident(wgt), _resident(wbr),
                  _resident(wout), _resident(gffn), _resident(wr_hi), _resident(wr_lo), _resident(br)],
        out_specs=[row(d), row(d), row(LANES), row(LANES)],
        out_shape=[jax.ShapeDtypeStruct((n, d), F32), jax.ShapeDtypeStruct((n, d), F32),
                   jax.ShapeDtypeStruct((n, LANES), jnp.int32), jax.ShapeDtypeStruct((n, LANES), F32)],
        compiler_params=_params("parallel"),
        name="merge_route",
    )(x, osb, ogla, omem, gmix, wgt, wbr, wout, gffn, wr_hi, wr_lo, br)


def _dispatch_kernel(dest_ref, x_ref, init_ref, xb_ref, sem):
    del init_ref
    tm = x_ref.shape[0]

    def row_copy(r, slot):
        return pltpu.make_async_copy(x_ref.at[pl.ds(r, 1), :], xb_ref.at[pl.ds(slot, 1), :], sem)

    def issue(r, carry):
        for k in range(TOP_K):
            row_copy(r, dest_ref[r * TOP_K + k]).start()
        return carry

    lax.fori_loop(0, tm, issue, 0, unroll=8)
    for k in range(TOP_K):
        pltpu.make_async_copy(x_ref, xb_ref.at[pl.ds(0, tm), :], sem).wait()


def _dispatch(x, dest, n_slots, *, tm):
    n, d = x.shape
    return pl.pallas_call(
        _dispatch_kernel,
        grid=(n // tm,),
        in_specs=[pl.BlockSpec((tm * TOP_K,), lambda i: (i,), memory_space=pltpu.SMEM),
                  pl.BlockSpec((tm, d), lambda i: (i, 0)),
                  pl.BlockSpec(memory_space=pl.ANY)],
        out_specs=pl.BlockSpec(memory_space=pl.ANY),
        out_shape=jax.ShapeDtypeStruct((n_slots, d), F32),
        scratch_shapes=[pltpu.SemaphoreType.DMA(())],
        input_output_aliases={2: 0},
        compiler_params=_params("arbitrary"),
        name="dispatch",
    )(dest, x, jnp.zeros((n_slots, d), F32))


def _expert_kernel(blk_e_ref, n_used_ref, x_ref, wg_ref, wu_ref, wd_ref, y_ref):
    i = pl.program_id(0)

    @pl.when(i < n_used_ref[0])
    def _():
        x = x_ref[...].astype(BF16)
        g = _dot(x, wg_ref[...])
        u = _dot(x, wu_ref[...])
        a = (g * jax.nn.sigmoid(g) * u).astype(BF16)
        y_ref[...] = _dot(a, wd_ref[...])

    @pl.when(i >= n_used_ref[0])
    def _():
        y_ref[...] = jnp.zeros_like(y_ref)


def _experts(xb, blk_e, n_used, wg, wu, wd, *, tm):
    p, d = xb.shape
    de = wg.shape[-1]
    grid_spec = pltpu.PrefetchScalarGridSpec(
        num_scalar_prefetch=2,
        grid=(p // tm,),
        in_specs=[pl.BlockSpec((tm, d), lambda i, be, nu: (i, 0)),
                  pl.BlockSpec((None, d, de), lambda i, be, nu: (be[i], 0, 0)),
                  pl.BlockSpec((None, d, de), lambda i, be, nu: (be[i], 0, 0)),
                  pl.BlockSpec((None, de, d), lambda i, be, nu: (be[i], 0, 0))],
        out_specs=pl.BlockSpec((tm, d), lambda i, be, nu: (i, 0)),
    )
    return pl.pallas_call(
        _expert_kernel,
        grid_spec=grid_spec,
        out_shape=jax.ShapeDtypeStruct((p, d), F32),
        compiler_params=_params("arbitrary"),
        name="experts",
    )(blk_e, n_used, xb, wg, wu, wd)


def _final_kernel(pos_ref, pos_next_ref, h_ref, wts_ref, g_ref, yb_ref, o_ref, rows_ref, sem):
    i = pl.program_id(0)
    tm = h_ref.shape[0]
    cur = i % 2

    def gather(slots_ref, buf):
        def issue(r, carry):
            for k in range(TOP_K):
                pltpu.make_async_copy(yb_ref.at[pl.ds(slots_ref[r * TOP_K + k], 1), :],
                                      rows_ref.at[buf, k, pl.ds(r, 1), :], sem.at[buf]).start()
            return carry

        lax.fori_loop(0, tm, issue, 0, unroll=8)

    @pl.when(i == 0)
    def _():
        gather(pos_ref, 0)

    @pl.when(i + 1 < pl.num_programs(0))
    def _():
        gather(pos_next_ref, 1 - cur)

    for k in range(TOP_K):
        pltpu.make_async_copy(yb_ref.at[pl.ds(0, tm), :], rows_ref.at[cur, k], sem.at[cur]).wait()
    w = wts_ref[...]
    h = h_ref[...] + (rows_ref[cur, 0] * w[:, 0:1] + rows_ref[cur, 1] * w[:, 1:2])
    o_ref[...] = _rmsnorm(h, g_ref[...])


def _final(h, yb, pos, wts, g, *, tm):
    n, d = h.shape
    steps = n // tm
    row = lambda w: pl.BlockSpec((tm, w), lambda i: (i, 0))
    slots = lambda nxt: pl.BlockSpec((tm * TOP_K,), lambda i: (jnp.minimum(i + nxt, steps - 1),),
                                     memory_space=pltpu.SMEM)
    return pl.pallas_call(
        _final_kernel,
        grid=(steps,),
        in_specs=[slots(0), slots(1), row(d), row(LANES), _resident(g), pl.BlockSpec(memory_space=pl.ANY)],
        out_specs=row(d),
        out_shape=jax.ShapeDtypeStruct((n, d), F32),
        scratch_shapes=[pltpu.VMEM((2, TOP_K, tm, d), F32), pltpu.SemaphoreType.DMA((2,))],
        compiler_params=_params("arbitrary"),
        name="combine_norm",
    )(pos, pos, h, wts, g, yb)


def _dispatch_plan(idx, tm):
    n = idx.shape[0]
    a = n * TOP_K
    flat_e = idx.reshape(a)
    onehot = (flat_e[:, None] == jnp.arange(N_EXPERTS, dtype=jnp.int32)[None, :]).astype(jnp.int32)
    ranks = jnp.cumsum(onehot, axis=0)
    counts = ranks[-1]
    rank = jnp.sum(onehot * ranks, axis=1) - 1
    padded = (counts + tm - 1) // tm * tm
    pend = jnp.cumsum(padded)
    pstart = pend - padded
    dest = pstart[flat_e] + rank
    n_blk = -(-(a + N_EXPERTS * (tm - 1)) // tm)
    blk_start = jnp.arange(n_blk, dtype=jnp.int32) * tm
    blk_e = jnp.minimum(jnp.sum((pend[None, :] <= blk_start[:, None]).astype(jnp.int32), axis=1),
                        N_EXPERTS - 1).astype(jnp.int32)
    n_used = (pend[-1] // tm).astype(jnp.int32).reshape(1)
    return dest.astype(jnp.int32), blk_e, n_used, n_blk


def _moe(ht, idx, wts, h, wg, wu, wd, g_final, *, tm_e, tm):
    dest, blk_e, n_used, n_blk = _dispatch_plan(idx[:, :TOP_K], tm_e)
    xb = _dispatch(ht, dest, n_blk * tm_e, tm=tm)
    yb = _experts(xb, blk_e, n_used, wg, wu, wd, tm=tm_e)
    return _final(h, yb, dest, wts, g_final, tm=tm)


def _layer(x, sb_past, gla_state, mem_kv, wts, *, tm, tq_sb, tt_gla, tq_mem, tm_e):
    b, t, d = x.shape
    n = b * t
    xf = x.reshape(n, d)
    sq, skb, svb, sk, sv, gq, gk, gv, la, gr, mq = _proj(
        xf, wts["g_mix"], wts["w_proj"], wts["wa2"], wts["ba"], wts["w_kvt"] if sb_past is None else None,
        tm=tm, seq_len=t, key_block=tq_sb)
    r3 = lambda a: a.reshape(b, t, a.shape[-1])
    if sb_past is None:
        o_sb = _sb_prompt(r3(sq), skb, svb, tq=tq_sb)
    else:
        o_sb = _sb_sample(r3(sq), r3(skb), r3(svb), sb_past[0], sb_past[1], tk_past=256)
    o_gla, st = _gla(r3(gq), r3(gk), r3(gv), r3(la), r3(gr), wts["g_gla"], gla_state, tt=tt_gla)
    o_mem = _mem_attention(r3(mq), mem_kv[0], mem_kv[1], tq=tq_mem)
    h, ht, idx, rw = _merge(xf, o_sb.reshape(n, -1), o_gla.reshape(n, -1), o_mem.reshape(n, -1),
                            wts["g_mix"], wts["w_gt"], wts["w_br"], wts["w_out"], wts["g_ffn"],
                            wts["wr_hi"], wts["wr_lo"], wts["b_r"], tm=tm)
    y = _moe(ht, idx, rw, h, wts["wg"], wts["wu"], wts["wd"], wts["g_final"], tm_e=tm_e, tm=tm)
    return y.reshape(b, t, d), sk, sv, st


def _state_to_t(s):
    b = s.shape[0]
    return s.transpose(0, 3, 1, 2).reshape(b, GLA_DV, GLA_HEADS * GLA_DK)


def _state_from_t(st):
    b = st.shape[0]
    return st.reshape(b, GLA_DV, GLA_HEADS, GLA_DK).transpose(0, 2, 3, 1)


def kernel(x_prompt, x_sample, mem_prompt, cache_sb_k, cache_sb_v, state_gla, cache_mem_k, cache_mem_v,
           norm_mix, w_in, w_gla_a2, b_gla_a, gla_norm, norm_mem, w_mem_kv, w_branch, w_out,
           norm_ffn, w_coarse, b_coarse, w_fine, b_fine, w_e_gate, w_e_up, w_e_down, norm_final):
    assert w_in.shape[0] == 1, "single-layer model"
    d = x_prompt.shape[-1]
    bp, tp, _ = x_prompt.shape
    bs, ts, _ = x_sample.shape
    bw = d // 2

    w = w_in[0]
    offs = [0]
    for s in (bw, bw, bw, 256, 256, bw, GLA_RANK, bw, bw, N_BRANCH * d):
        offs.append(offs[-1] + s)
    col = lambda i: w[:, offs[i]:offs[i + 1]]
    w_proj = jnp.concatenate(
        [col(0), col(1), col(2), col(3), col(4), col(5), col(7), col(8),
         jnp.pad(col(6), ((0, 0), (0, LANES - GLA_RANK)))], axis=1).astype(BF16)
    assert w_proj.shape[1] == _PROJ_W
    row2 = lambda a: a.reshape(1, -1)
    n_route = N_GROUPS + N_EXPERTS
    w_route = jnp.concatenate([w_coarse[0], w_fine[0].transpose(1, 0, 2).reshape(d, N_EXPERTS)], axis=1)
    w_route = jnp.pad(w_route, ((0, 0), (0, LANES - n_route)))
    wr_hi = w_route.astype(BF16)
    wr_lo = (w_route - wr_hi.astype(F32)).astype(BF16)
    b_route = jnp.pad(jnp.concatenate([b_coarse[0], b_fine[0].reshape(-1)]), (0, LANES - n_route)).reshape(1, LANES)
    wts = dict(
        g_mix=row2(norm_mix[0]), w_proj=w_proj,
        w_kvt=jnp.concatenate([col(1), col(2)], axis=1).T.astype(BF16),
        wa2=jnp.pad(w_gla_a2[0], ((0, LANES - GLA_RANK), (0, 0))).astype(BF16), ba=row2(b_gla_a[0]),
        g_gla=row2(gla_norm[0]),
        w_gt=col(9).astype(BF16), w_br=w_branch[0].astype(BF16), w_out=w_out[0].astype(BF16),
        g_ffn=row2(norm_ffn[0]), wr_hi=wr_hi, wr_lo=wr_lo, b_r=b_route,
        wg=w_e_gate[0].astype(BF16), wu=w_e_up[0].astype(BF16), wd=w_e_down[0].astype(BF16),
        g_final=row2(norm_final),
    )

    m = mem_prompt.shape[1]
    mk, mv = _memkv(mem_prompt.reshape(bp * m, d), row2(norm_mem[0]), w_mem_kv[0].astype(BF16), tm=min(512, bp * m))
    st0 = jnp.zeros((bp, GLA_DV, GLA_HEADS * GLA_DK), F32)
    y_p, sk_p, sv_p, st_p = _layer(x_prompt, None, st0, (mk.reshape(bp, m, bw), mv.reshape(bp, m, bw)), wts,
                                   tm=min(512, bp * tp), tq_sb=min(256, tp), tt_gla=min(256, tp),
                                   tq_mem=min(512, tp), tm_e=256)

    past = (cache_sb_k[0].transpose(0, 2, 3, 1), cache_sb_v[0].transpose(0, 2, 3, 1))
    y_s, sk_s, sv_s, st_s = _layer(x_sample, past, _state_to_t(state_gla[0]),
                                   (cache_mem_k[0], cache_mem_v[0]), wts,
                                   tm=min(512, bs * ts), tq_sb=ts, tt_gla=ts, tq_mem=ts, tm_e=256)

    hd = lambda a, bb, tt: a.reshape(1, bb, tt, SB_HEADS, SB_DIM)
    return (y_p, y_s,
            sk_p.transpose(0, 3, 1, 2)[None], sv_p.transpose(0, 3, 1, 2)[None],
            _state_from_t(st_p)[None],
            mk.reshape(1, bp, m, MEM_HEADS, MEM_DIM), mv.reshape(1, bp, m, MEM_HEADS, MEM_DIM),
            hd(sk_s, bs, ts), hd(sv_s, bs, ts),
            _state_from_t(st_s)[None])
```

```python
import functools

import jax
import jax.numpy as jnp
from jax import lax
from jax.experimental import pallas as pl
from jax.experimental.pallas import tpu as pltpu

F32 = jnp.float32
BF16 = jnp.bfloat16
EPS = 1e-6

SB_HEADS = 8
SB_DIM = 64
GLA_HEADS = 4
GLA_DK = 64
GLA_DV = 128
GLA_RANK = 16
GLA_TAU = 16.0
GLA_CHUNK = 64
MEM_HEADS = 4
MEM_DIM = 128
N_BRANCH = 3
N_GROUPS = 4
EXPERTS_PER_GROUP = 8
N_EXPERTS = N_GROUPS * EXPERTS_PER_GROUP
TOP_K = 2

LANES = 128
VMEM_LIMIT = 56 * 1024 * 1024
NEG_BIG = -1e30
_LOG_F32_ZERO = -110.0

_NT = (((1,), (1,)), ((), ()))
_TN = (((0,), (0,)), ((), ()))


def _dot(a, b):
    return jnp.dot(a, b, preferred_element_type=F32)


def _dot_nt(a, b):
    return lax.dot_general(a, b, _NT, preferred_element_type=F32)


def _split_bf16(x):
    hi = x.astype(BF16)
    lo = (x - hi.astype(F32)).astype(BF16)
    return hi, lo


def _rmsnorm(x, g):
    return x * lax.rsqrt(jnp.mean(x * x, axis=-1, keepdims=True) + EPS) * g


def _log_sigmoid(x):
    return jnp.minimum(x, 0.0) - jnp.log1p(jnp.exp(-jnp.abs(x)))


def _resident(a):
    return pl.BlockSpec(a.shape, lambda *_: (0,) * a.ndim, pipeline_mode=pl.Buffered(1))


def _params(*sem):
    return pltpu.CompilerParams(dimension_semantics=sem, vmem_limit_bytes=VMEM_LIMIT)


_C_SQ, _C_SK, _C_SV = 0, 512, 1024
_C_GQ, _C_GK, _C_GV = 1536, 1792, 2048
_C_GR, _C_MQ, _C_GLR = 2560, 3072, 3584
_PROJ_W = 3712


def _proj_kernel(*refs, kv_transposed, key_block):
    if kv_transposed:
        x_ref, g_ref, w_ref, wa2_ref, ba_ref, wkvt_ref = refs[:6]
    else:
        x_ref, g_ref, w_ref, wa2_ref, ba_ref = refs[:5]
    sq_ref, skb_ref, svb_ref, sk_ref, sv_ref, gq_ref, gk_ref, gv_ref, la_ref, gr_ref, mq_ref = refs[-11:]
    xn = _rmsnorm(x_ref[...], g_ref[...]).astype(BF16)

    def seg(start, width):
        return _dot(xn, w_ref[:, start:start + width])

    sq_ref[...] = (seg(_C_SQ, 512) * SB_DIM ** -0.5).astype(BF16)
    if kv_transposed:
        tm = xn.shape[0]
        for half, (bf_ref, heads_ref) in enumerate(((skb_ref, sk_ref), (svb_ref, sv_ref))):
            kv_t = _dot_nt(wkvt_ref[half * 512:(half + 1) * 512, :], xn)
            heads_ref[...] = kv_t.reshape(SB_HEADS, SB_DIM, tm)
            for j in range(tm // key_block):
                bf_ref[j] = kv_t[:, j * key_block:(j + 1) * key_block].astype(BF16)
    else:
        for c0, bf_ref, heads_ref in ((_C_SK, skb_ref, sk_ref), (_C_SV, svb_ref, sv_ref)):
            kv = seg(c0, 512)
            bf_ref[...] = kv.astype(BF16)
            for h in range(SB_HEADS):
                heads_ref[:, h, :] = kv[:, h * SB_DIM:(h + 1) * SB_DIM]
    gq_ref[...] = seg(_C_GQ, 256) * GLA_DK ** -0.5
    gk_ref[...] = seg(_C_GK, 256)
    gv_ref[...] = seg(_C_GV, 512).astype(BF16)
    gr_ref[...] = seg(_C_GR, 512)
    mq_ref[...] = seg(_C_MQ, 512).astype(BF16)
    glr = seg(_C_GLR, LANES).astype(BF16)
    la_ref[...] = _log_sigmoid(_dot(glr, wa2_ref[...]) + ba_ref[...]) * (1.0 / GLA_TAU)


def _proj(x, g, w_proj, wa2, ba, w_kvt, *, tm, seq_len, key_block):
    n, d = x.shape
    kv_transposed = w_kvt is not None
    row = lambda w: pl.BlockSpec((tm, w), lambda i: (i, 0))
    flat = lambda w, dt: (row(w), jax.ShapeDtypeStruct((n, w), dt))
    if kv_transposed:
        nb, per_b = n // seq_len, seq_len // tm
        kv_bf = (pl.BlockSpec((None, tm // key_block, 512, key_block), lambda i: (i // per_b, i % per_b, 0, 0)),
                 jax.ShapeDtypeStruct((nb, seq_len // key_block, 512, key_block), BF16))
        kv_f32 = (pl.BlockSpec((None, SB_HEADS, SB_DIM, tm), lambda i: (i // per_b, 0, 0, i % per_b)),
                  jax.ShapeDtypeStruct((nb, SB_HEADS, SB_DIM, seq_len), F32))
    else:
        kv_bf = flat(512, BF16)
        kv_f32 = (pl.BlockSpec((tm, SB_HEADS, SB_DIM), lambda i: (i, 0, 0)),
                  jax.ShapeDtypeStruct((n, SB_HEADS, SB_DIM), F32))
    outs = [flat(512, BF16), kv_bf, kv_bf, kv_f32, kv_f32,
            flat(256, F32), flat(256, F32), flat(512, BF16), flat(256, F32), flat(512, F32), flat(512, BF16)]
    args = [x, g, w_proj, wa2, ba] + ([w_kvt] if kv_transposed else [])
    return pl.pallas_call(
        functools.partial(_proj_kernel, kv_transposed=kv_transposed, key_block=key_block),
        grid=(n // tm,),
        in_specs=[row(d)] + [_resident(a) for a in args[1:]],
        out_specs=[o[0] for o in outs],
        out_shape=[o[1] for o in outs],
        compiler_params=_params("parallel"),
        name="proj",
    )(*args)


def _memkv_kernel(x_ref, g_ref, w_ref, mk_ref, mv_ref):
    xn = _rmsnorm(x_ref[...], g_ref[...]).astype(BF16)
    half = mk_ref.shape[-1]
    mk_ref[...] = _dot(xn, w_ref[:, :half])
    mv_ref[...] = _dot(xn, w_ref[:, half:])


def _memkv(x, g, w, tm):
    n, d = x.shape
    half = w.shape[1] // 2
    row = lambda wd: pl.BlockSpec((tm, wd), lambda i: (i, 0))
    return pl.pallas_call(
        _memkv_kernel,
        grid=(n // tm,),
        in_specs=[row(d), _resident(g), _resident(w)],
        out_specs=[row(half), row(half)],
        out_shape=[jax.ShapeDtypeStruct((n, half), F32)] * 2,
        compiler_params=_params("parallel"),
        name="memkv",
    )(x, g, w)


def _suffix_matrix(tk):
    j = lax.broadcasted_iota(jnp.int32, (2 * tk, tk), 0)
    s = lax.broadcasted_iota(jnp.int32, (2 * tk, tk), 1)
    return ((j > s) & ((j < tk) | (j > s + tk))).astype(BF16)


def _sb_weights(z, carry, mask):
    lz = jnp.minimum(z, 0.0) - jnp.log(1.0 + jnp.exp(-jnp.abs(z)))
    l1 = lz - z
    if mask is not None:
        l1 = jnp.where(mask, l1, 0.0)
    hi, lo = _split_bf16(l1)
    cum = _dot(jnp.concatenate([hi, lo], axis=1), _suffix_matrix(z.shape[1]))
    p = jnp.exp(lz + cum + carry)
    if mask is not None:
        p = jnp.where(mask, p, 0.0)
    return p.astype(BF16), carry + cum[:, :1] + l1[:, :1]


def _fold_descending(n_blocks, alive, fold_block):
    def cond(state):
        i, live = state
        return (i < n_blocks) & live

    def body(state):
        i, _ = state
        return i + 1, fold_block(n_blocks - 1 - i)

    return lax.while_loop(cond, body, (jnp.int32(0), alive))[1]


def _sb_prompt_kernel(q_ref, k_ref, v_ref, o_ref, acc_ref, car_ref, *, tq, pairs):
    qi = pl.program_id(2)
    heads = 2 * pairs
    lane = lax.broadcasted_iota(jnp.int32, (tq, LANES), 1)
    pair_cols = [slice(j * LANES, (j + 1) * LANES) for j in range(pairs)]
    q_heads = []
    for j in range(pairs):
        q = q_ref[:, pair_cols[j]]
        q_heads += [jnp.where(lane < SB_DIM, q, jnp.zeros_like(q)),
                    jnp.where(lane >= SB_DIM, q, jnp.zeros_like(q))]
    acc_ref[...] = jnp.zeros_like(acc_ref)
    car_ref[...] = jnp.zeros_like(car_ref)

    def block(kb, mask):
        z = jnp.concatenate([_dot(q_heads[h], k_ref[kb, pair_cols[h // 2], :]) for h in range(heads)], axis=0)
        p, carry = _sb_weights(z, car_ref[...], mask)
        car_ref[...] = carry
        for h in range(heads):
            acc_ref[h] += _dot_nt(p[h * tq:(h + 1) * tq], v_ref[kb, pair_cols[h // 2], :])
        return jnp.max(carry) > _LOG_F32_ZERO

    r = lax.broadcasted_iota(jnp.int32, (heads * tq, tq), 0)
    c = lax.broadcasted_iota(jnp.int32, (heads * tq, tq), 1)
    alive = block(qi, c < (r & (tq - 1)))
    _fold_descending(qi, alive, lambda kb: block(kb, None))
    for j in range(pairs):
        o_ref[:, pair_cols[j]] = jnp.where(lane < SB_DIM, acc_ref[2 * j], acc_ref[2 * j + 1]).astype(o_ref.dtype)


def _sb_prompt(q, k_t, v_t, *, tq, pairs):
    b, t, w = q.shape
    assert k_t.shape == (b, t // tq, w, tq) and tq & (tq - 1) == 0
    pw = pairs * LANES
    seq = pl.BlockSpec((None, t // tq, pw, tq), lambda bi, p, qi: (bi, 0, p, 0))
    tile = pl.BlockSpec((None, tq, pw), lambda bi, p, qi: (bi, qi, p))
    return pl.pallas_call(
        functools.partial(_sb_prompt_kernel, tq=tq, pairs=pairs),
        grid=(b, w // pw, t // tq),
        in_specs=[tile, seq, seq],
        out_specs=tile,
        out_shape=jax.ShapeDtypeStruct((b, t, w), BF16),
        scratch_shapes=[pltpu.VMEM((2 * pairs, tq, LANES), F32), pltpu.VMEM((2 * pairs * tq, 1), F32)],
        compiler_params=_params("parallel", "parallel", "arbitrary"),
        name="sb_prompt",
    )(q, k_t, v_t)


def _sb_sample_kernel(q_ref, k_ref, v_ref, pk_ref, pv_ref, o_ref, acc_ref, car_ref, *, tk_past):
    t = q_ref.shape[0]
    past_len = pk_ref.shape[-1]
    acc_ref[...] = jnp.zeros_like(acc_ref)
    car_ref[...] = jnp.zeros_like(car_ref)
    head_cols = [slice(h * SB_DIM, (h + 1) * SB_DIM) for h in range(SB_HEADS)]

    def fold_all_heads(logits, weighted_values, mask):
        z = jnp.concatenate([logits(h, q_ref[:, head_cols[h]]) for h in range(SB_HEADS)], axis=0)
        p, carry = _sb_weights(z, car_ref[...], mask)
        car_ref[...] = carry
        for h in range(SB_HEADS):
            acc_ref[h] += weighted_values(h, p[h * t:(h + 1) * t])
        return jnp.max(carry) > _LOG_F32_ZERO

    r = lax.broadcasted_iota(jnp.int32, (SB_HEADS * t, t), 0)
    c = lax.broadcasted_iota(jnp.int32, (SB_HEADS * t, t), 1)
    alive = fold_all_heads(lambda h, qh: _dot_nt(qh, k_ref[:, head_cols[h]]),
                           lambda h, ph: _dot(ph, v_ref[:, head_cols[h]]),
                           c < (r & (t - 1)))

    def past_block(kb_idx):
        pos = pl.ds(pl.multiple_of(kb_idx * tk_past, tk_past), tk_past)
        return fold_all_heads(lambda h, qh: _dot(qh, pk_ref[h, :, pos].astype(BF16)),
                              lambda h, ph: _dot_nt(ph, pv_ref[h, :, pos].astype(BF16)), None)

    _fold_descending(past_len // tk_past, alive, past_block)
    for h in range(SB_HEADS):
        o_ref[:, head_cols[h]] = acc_ref[h].astype(o_ref.dtype)


def _sb_sample(q, k, v, past_k, past_v, *, tk_past):
    b, t, w = q.shape
    assert t & (t - 1) == 0, "query chunk length must be a power of two"
    cur = pl.BlockSpec((None, t, w), lambda bi: (bi, 0, 0))
    past = pl.BlockSpec((None,) + past_k.shape[1:], lambda bi: (bi, 0, 0, 0))
    return pl.pallas_call(
        functools.partial(_sb_sample_kernel, tk_past=tk_past),
        grid=(b,),
        in_specs=[cur, cur, cur, past, past],
        out_specs=cur,
        out_shape=jax.ShapeDtypeStruct((b, t, w), BF16),
        scratch_shapes=[pltpu.VMEM((SB_HEADS, t, SB_DIM), F32), pltpu.VMEM((SB_HEADS * t, 1), F32)],
        compiler_params=_params("parallel"),
        name="sb_sample",
    )(q, k, v, past_k, past_v)


def _gla_kernel(gq_ref, gk_ref, gv_ref, la_ref, gr_ref, gn_ref, s0_ref, o_ref, st_ref, st_scr, *, n_chunks):
    c_len = GLA_CHUNK
    tt = n_chunks * c_len
    kw = GLA_HEADS * GLA_DK
    t = pl.program_id(1)

    @pl.when(t == 0)
    def _():
        st_scr[...] = s0_ref[...]

    r = lax.broadcasted_iota(jnp.int32, (tt, tt), 0)
    c = lax.broadcasted_iota(jnp.int32, (tt, tt), 1)
    chunk_bits = c_len.bit_length() - 1
    same_chunk = (r >> chunk_bits) == (c >> chunk_bits)
    tril = (same_chunk & (r >= c)).astype(BF16)
    ones = same_chunk.astype(BF16)
    la_hi, la_lo = _split_bf16(la_ref[...])
    b = _dot(tril, la_hi) + _dot(tril, la_lo)
    b_tot = _dot(ones, la_hi) + _dot(ones, la_lo)
    gk = gk_ref[...]
    qd = gq_ref[...] * jnp.exp(b)
    kd = (gk * jnp.exp(-b)).astype(BF16)
    ke = (gk * jnp.exp(b_tot - b)).astype(BF16)
    decay = jnp.exp(b_tot)

    rc = lax.broadcasted_iota(jnp.int32, (GLA_HEADS * c_len, c_len), 0)
    cc = lax.broadcasted_iota(jnp.int32, (GLA_HEADS * c_len, c_len), 1)
    causal = (rc & (c_len - 1)) >= cc
    lane_q = lax.broadcasted_iota(jnp.int32, (c_len, kw), 1)
    lane_s = lax.broadcasted_iota(jnp.int32, (GLA_DV, kw), 1)

    for ci in range(n_chunks):
        rows = slice(ci * c_len, (ci + 1) * c_len)
        qd_c = qd[rows]
        qs = jnp.concatenate(
            [jnp.where((lane_q >= h * GLA_DK) & (lane_q < (h + 1) * GLA_DK), qd_c, 0.0)
             for h in range(GLA_HEADS)], axis=0).astype(BF16)
        att = jnp.where(causal, _dot_nt(qs, kd[rows]), 0.0).astype(BF16)
        st = st_scr[...]
        o_state = _dot_nt(qs, st.astype(BF16))
        v = gv_ref[rows, :]
        upd = lax.dot_general(v, ke[rows], _TN, preferred_element_type=F32)
        st_new = st * decay[ci * c_len:ci * c_len + 1, :]
        for h in range(GLA_HEADS):
            hrows = slice(h * c_len, (h + 1) * c_len)
            cols = slice(h * GLA_DV, (h + 1) * GLA_DV)
            o = _dot(att[hrows], v[:, cols]) + o_state[hrows]
            on = _rmsnorm(o, gn_ref[:, cols])
            gr = gr_ref[rows, cols]
            o_ref[rows, cols] = (on * (gr * jax.nn.sigmoid(gr))).astype(o_ref.dtype)
            in_head = (lane_s >= h * GLA_DK) & (lane_s < (h + 1) * GLA_DK)
            st_new = st_new + jnp.where(in_head, upd[cols], 0.0)
        st_scr[...] = st_new

    @pl.when(t == pl.num_programs(1) - 1)
    def _():
        st_ref[...] = st_scr[...]


def _gla(gq, gk, gv, la, gr, gn, st0, *, tt):
    b, t, kw = gq.shape
    vw = gv.shape[-1]
    tok = lambda w: pl.BlockSpec((None, tt, w), lambda bi, ti: (bi, ti, 0))
    state = pl.BlockSpec((None, GLA_DV, kw), lambda bi, ti: (bi, 0, 0))
    return pl.pallas_call(
        functools.partial(_gla_kernel, n_chunks=tt // GLA_CHUNK),
        grid=(b, t // tt),
        in_specs=[tok(kw), tok(kw), tok(vw), tok(kw), tok(vw), _resident(gn), state],
        out_specs=[tok(vw), state],
        out_shape=[jax.ShapeDtypeStruct((b, t, vw), BF16), jax.ShapeDtypeStruct((b, GLA_DV, kw), F32)],
        scratch_shapes=[pltpu.VMEM((GLA_DV, kw), F32)],
        compiler_params=_params("parallel", "arbitrary"),
        name="gla",
    )(gq, gk, gv, la, gr, gn, st0)


def _mem_kernel(q_ref, mk_ref, mv_ref, o_ref):
    head_major = len(mk_ref.shape) == 3
    for h in range(MEM_HEADS):
        cols = slice(h * MEM_DIM, (h + 1) * MEM_DIM)
        kh = mk_ref[:, h, :] if head_major else mk_ref[:, cols]
        vh = mv_ref[:, h, :] if head_major else mv_ref[:, cols]
        s = _dot_nt(q_ref[:, cols], kh.astype(BF16)) * MEM_DIM ** -0.5
        e = jnp.exp(s - jnp.max(s, axis=-1, keepdims=True))
        p = e / jnp.sum(e, axis=-1, keepdims=True)
        o_ref[:, cols] = _dot(p.astype(BF16), vh.astype(BF16)).astype(o_ref.dtype)


def _mem_attention(q, mk, mv, *, tq):
    b, t, w = q.shape
    tile = pl.BlockSpec((None, tq, w), lambda bi, qi: (bi, qi, 0))
    mem = pl.BlockSpec((None,) + mk.shape[1:], lambda bi, qi: (bi,) + (0,) * (mk.ndim - 1))
    return pl.pallas_call(
        _mem_kernel,
        grid=(b, t // tq),
        in_specs=[tile, mem, mem],
        out_specs=tile,
        out_shape=jax.ShapeDtypeStruct((b, t, w), BF16),
        compiler_params=_params("parallel", "arbitrary"),
        name="mem_attention",
    )(q, mk, mv)


def _merge_kernel(x_ref, osb_ref, ogla_ref, omem_ref, gmix_ref, wgt_ref, wbr_ref, wout_ref,
                  gffn_ref, wr_hi_ref, wr_lo_ref, br_ref, h_ref, ht_ref, idx_ref, wts_ref):
    x = x_ref[...]
    d = x.shape[-1]
    xn = _rmsnorm(x, gmix_ref[...]).astype(BF16)
    mixed = None
    for n, o_ref in enumerate((osb_ref, ogla_ref, omem_ref)):
        gate = jax.nn.sigmoid(_dot(xn, wgt_ref[:, n * d:(n + 1) * d]))
        term = gate * _dot(o_ref[...], wbr_ref[n])
        mixed = term if mixed is None else mixed + term
    h = x + _dot(mixed.astype(BF16), wout_ref[...])
    h_ref[...] = h
    hn = _rmsnorm(h, gffn_ref[...])
    ht_ref[...] = hn

    hn_hi, hn_lo = _split_bf16(hn)
    logits = (_dot(hn_hi, wr_hi_ref[...]) + _dot(hn_lo, wr_hi_ref[...])
              + _dot(hn_hi, wr_lo_ref[...]) + br_ref[...])
    lane = lax.broadcasted_iota(jnp.int32, logits.shape, 1)
    rmax = lambda a: jnp.max(a, axis=-1, keepdims=True)
    rmin = lambda a: jnp.min(a, axis=-1, keepdims=True)
    rsum = lambda a: jnp.sum(a, axis=-1, keepdims=True)

    lc = jnp.where(lane < N_GROUPS, logits, NEG_BIG)
    mc = rmax(lc)
    grp = rmin(jnp.where(lc == mc, lane, LANES))
    p_grp = 1.0 / rsum(jnp.exp(lc - mc))

    lo = N_GROUPS + grp * EXPERTS_PER_GROUP
    in_grp = (lane >= lo) & (lane < lo + EXPERTS_PER_GROUP)
    lf = jnp.where(in_grp, logits, NEG_BIG)
    ef = jnp.exp(lf - rmax(lf))
    pf = jnp.where(in_grp, ef / rsum(ef), -1.0)
    v1 = rmax(pf)
    i1 = rmin(jnp.where(pf == v1, lane, LANES))
    pf2 = jnp.where(lane == i1, -1.0, pf)
    v2 = rmax(pf2)
    i2 = rmin(jnp.where(pf2 == v2, lane, LANES))
    tot = v1 + v2
    idx_ref[...] = jnp.where(lane == 0, i1 - N_GROUPS, jnp.where(lane == 1, i2 - N_GROUPS, 0))
    wts_ref[...] = jnp.where(lane == 0, p_grp * (v1 / tot), jnp.where(lane == 1, p_grp * (v2 / tot), 0.0))


def _merge(x, osb, ogla, omem, gmix, wgt, wbr, wout, gffn, wr_hi, wr_lo, br, *, tm):
    n, d = x.shape
    bw = osb.shape[1]
    row = lambda w: pl.BlockSpec((tm, w), lambda i: (i, 0))
    return pl.pallas_call(
        _merge_kernel,
        grid=(n // tm,),
        in_specs=[row(d), row(bw), row(bw), row(bw), _resident(gmix), _resident(wgt), _resident(wbr),
                  _resident(wout), _resident(gffn), _resident(wr_hi), _resident(wr_lo), _resident(br)],
        out_specs=[row(d), row(d), row(LANES), row(LANES)],
        out_shape=[jax.ShapeDtypeStruct((n, d), F32), jax.ShapeDtypeStruct((n, d), F32),
                   jax.ShapeDtypeStruct((n, LANES), jnp.int32), jax.ShapeDtypeStruct((n, LANES), F32)],
        compiler_params=_params("parallel"),
        name="merge_route",
    )(x, osb, ogla, omem, gmix, wgt, wbr, wout, gffn, wr_hi, wr_lo, br)


def _dispatch_kernel(dest_ref, x_ref, init_ref, xb_ref, sem):
    del init_ref
    tm = x_ref.shape[0]

    def row_copy(r, slot):
        return pltpu.make_async_copy(x_ref.at[pl.ds(r, 1), :], xb_ref.at[pl.ds(slot, 1), :], sem)

    def issue(r, carry):
        for k in range(TOP_K):
            row_copy(r, dest_ref[r * TOP_K + k]).start()
        return carry

    lax.fori_loop(0, tm, issue, 0, unroll=8)
    for k in range(TOP_K):
        pltpu.make_async_copy(x_ref, xb_ref.at[pl.ds(0, tm), :], sem).wait()


def _dispatch(x, dest, n_slots, *, tm):
    n, d = x.shape
    return pl.pallas_call(
        _dispatch_kernel,
        grid=(n // tm,),
        in_specs=[pl.BlockSpec((tm * TOP_K,), lambda i: (i,), memory_space=pltpu.SMEM),
                  pl.BlockSpec((tm, d), lambda i: (i, 0)),
                  pl.BlockSpec(memory_space=pl.ANY)],
        out_specs=pl.BlockSpec(memory_space=pl.ANY),
        out_shape=jax.ShapeDtypeStruct((n_slots, d), F32),
        scratch_shapes=[pltpu.SemaphoreType.DMA(())],
        input_output_aliases={2: 0},
        compiler_params=_params("arbitrary"),
        name="dispatch",
    )(dest, x, jnp.zeros((n_slots, d), F32))


def _expert_kernel(blk_e_ref, n_used_ref, x_ref, wg_ref, wu_ref, wd_ref, y_ref):
    i = pl.program_id(0)

    @pl.when(i < n_used_ref[0])
    def _():
        x = x_ref[...].astype(BF16)
        g = _dot(x, wg_ref[...])
        u = _dot(x, wu_ref[...])
        a = (g * jax.nn.sigmoid(g) * u).astype(BF16)
        y_ref[...] = _dot(a, wd_ref[...])

    @pl.when(i >= n_used_ref[0])
    def _():
        y_ref[...] = jnp.zeros_like(y_ref)


def _experts(xb, blk_e, n_used, wg, wu, wd, *, tm):
    p, d = xb.shape
    de = wg.shape[-1]
    grid_spec = pltpu.PrefetchScalarGridSpec(
        num_scalar_prefetch=2,
        grid=(p // tm,),
        in_specs=[pl.BlockSpec((tm, d), lambda i, be, nu: (i, 0)),
                  pl.BlockSpec((None, d, de), lambda i, be, nu: (be[i], 0, 0)),
                  pl.BlockSpec((None, d, de), lambda i, be, nu: (be[i], 0, 0)),
                  pl.BlockSpec((None, de, d), lambda i, be, nu: (be[i], 0, 0))],
        out_specs=pl.BlockSpec((tm, d), lambda i, be, nu: (i, 0)),
    )
    return pl.pallas_call(
        _expert_kernel,
        grid_spec=grid_spec,
        out_shape=jax.ShapeDtypeStruct((p, d), F32),
        compiler_params=_params("arbitrary"),
        name="experts",
    )(blk_e, n_used, xb, wg, wu, wd)


def _final_kernel(pos_ref, pos_next_ref, h_ref, wts_ref, g_ref, yb_ref, o_ref, rows_ref, sem):
    i = pl.program_id(0)
    tm = h_ref.shape[0]
    cur = i % 2

    def gather(slots_ref, buf):
        def issue(r, carry):
            for k in range(TOP_K):
                pltpu.make_async_copy(yb_ref.at[pl.ds(slots_ref[r * TOP_K + k], 1), :],
                                      rows_ref.at[buf, k, pl.ds(r, 1), :], sem.at[buf]).start()
            return carry

        lax.fori_loop(0, tm, issue, 0, unroll=8)

    @pl.when(i == 0)
    def _():
        gather(pos_ref, 0)

    @pl.when(i + 1 < pl.num_programs(0))
    def _():
        gather(pos_next_ref, 1 - cur)

    for k in range(TOP_K):
        pltpu.make_async_copy(yb_ref.at[pl.ds(0, tm), :], rows_ref.at[cur, k], sem.at[cur]).wait()
    w = wts_ref[...]
    h = h_ref[...] + (rows_ref[cur, 0] * w[:, 0:1] + rows_ref[cur, 1] * w[:, 1:2])
    o_ref[...] = _rmsnorm(h, g_ref[...])


def _final(h, yb, pos, wts, g, *, tm):
    n, d = h.shape
    steps = n // tm
    row = lambda w: pl.BlockSpec((tm, w), lambda i: (i, 0))
    slots = lambda nxt: pl.BlockSpec((tm * TOP_K,), lambda i: (jnp.minimum(i + nxt, steps - 1),),
                                     memory_space=pltpu.SMEM)
    return pl.pallas_call(
        _final_kernel,
        grid=(steps,),
        in_specs=[slots(0), slots(1), row(d), row(LANES), _resident(g), pl.BlockSpec(memory_space=pl.ANY)],
        out_specs=row(d),
        out_shape=jax.ShapeDtypeStruct((n, d), F32),
        scratch_shapes=[pltpu.VMEM((2, TOP_K, tm, d), F32), pltpu.SemaphoreType.DMA((2,))],
        compiler_params=_params("arbitrary"),
        name="combine_norm",
    )(pos, pos, h, wts, g, yb)


def _dispatch_plan(idx, tm):
    n = idx.shape[0]
    a = n * TOP_K
    flat_e = idx.reshape(a)
    onehot = (flat_e[:, None] == jnp.arange(N_EXPERTS, dtype=jnp.int32)[None, :]).astype(jnp.int32)
    ranks = jnp.cumsum(onehot, axis=0)
    counts = ranks[-1]
    rank = jnp.sum(onehot * ranks, axis=1) - 1
    padded = (counts + tm - 1) // tm * tm
    pend = jnp.cumsum(padded)
    pstart = pend - padded
    dest = pstart[flat_e] + rank
    n_blk = -(-(a + N_EXPERTS * (tm - 1)) // tm)
    blk_start = jnp.arange(n_blk, dtype=jnp.int32) * tm
    blk_e = jnp.minimum(jnp.sum((pend[None, :] <= blk_start[:, None]).astype(jnp.int32), axis=1),
                        N_EXPERTS - 1).astype(jnp.int32)
    n_used = (pend[-1] // tm).astype(jnp.int32).reshape(1)
    return dest.astype(jnp.int32), blk_e, n_used, n_blk


def _moe(ht, idx, wts, h, wg, wu, wd, g_final, *, tm_e, tm):
    dest, blk_e, n_used, n_blk = _dispatch_plan(idx[:, :TOP_K], tm_e)
    xb = _dispatch(ht, dest, n_blk * tm_e, tm=tm)
    yb = _experts(xb, blk_e, n_used, wg, wu, wd, tm=tm_e)
    return _final(h, yb, dest, wts, g_final, tm=tm)


def _layer(x, sb_past, gla_state, mem_kv, wts, *, tm, tq_sb, tt_gla, tq_mem, tm_e):
    b, t, d = x.shape
    n = b * t
    xf = x.reshape(n, d)
    sq, skb, svb, sk, sv, gq, gk, gv, la, gr, mq = _proj(
        xf, wts["g_mix"], wts["w_proj"], wts["wa2"], wts["ba"], wts["w_kvt"] if sb_past is None else None,
        tm=tm, seq_len=t, key_block=tq_sb)
    r3 = lambda a: a.reshape(b, t, a.shape[-1])
    if sb_past is None:
        o_sb = _sb_prompt(r3(sq), skb, svb, tq=tq_sb, pairs=4)
    else:
        o_sb = _sb_sample(r3(sq), r3(skb), r3(svb), sb_past[0], sb_past[1], tk_past=256)
    o_gla, st = _gla(r3(gq), r3(gk), r3(gv), r3(la), r3(gr), wts["g_gla"], gla_state, tt=tt_gla)
    o_mem = _mem_attention(r3(mq), mem_kv[0], mem_kv[1], tq=tq_mem)
    h, ht, idx, rw = _merge(xf, o_sb.reshape(n, -1), o_gla.reshape(n, -1), o_mem.reshape(n, -1),
                            wts["g_mix"], wts["w_gt"], wts["w_br"], wts["w_out"], wts["g_ffn"],
                            wts["wr_hi"], wts["wr_lo"], wts["b_r"], tm=tm)
    y = _moe(ht, idx, rw, h, wts["wg"], wts["wu"], wts["wd"], wts["g_final"], tm_e=tm_e, tm=tm)
    return y.reshape(b, t, d), sk, sv, st


def _state_to_t(s):
    b = s.shape[0]
    return s.transpose(0, 3, 1, 2).reshape(b, GLA_DV, GLA_HEADS * GLA_DK)


def _state_from_t(st):
    b = st.shape[0]
    return st.reshape(b, GLA_DV, GLA_HEADS, GLA_DK).transpose(0, 2, 3, 1)


def kernel(x_prompt, x_sample, mem_prompt, cache_sb_k, cache_sb_v, state_gla, cache_mem_k, cache_mem_v,
           norm_mix, w_in, w_gla_a2, b_gla_a, gla_norm, norm_mem, w_mem_kv, w_branch, w_out,
           norm_ffn, w_coarse, b_coarse, w_fine, b_fine, w_e_gate, w_e_up, w_e_down, norm_final):
    assert w_in.shape[0] == 1, "single-layer model"
    d = x_prompt.shape[-1]
    bp, tp, _ = x_prompt.shape
    bs, ts, _ = x_sample.shape
    bw = d // 2

    w = w_in[0]
    offs = [0]
    for s in (bw, bw, bw, 256, 256, bw, GLA_RANK, bw, bw, N_BRANCH * d):
        offs.append(offs[-1] + s)
    col = lambda i: w[:, offs[i]:offs[i + 1]]
    w_proj = jnp.concatenate(
        [col(0), col(1), col(2), col(3), col(4), col(5), col(7), col(8),
         jnp.pad(col(6), ((0, 0), (0, LANES - GLA_RANK)))], axis=1).astype(BF16)
    assert w_proj.shape[1] == _PROJ_W
    row2 = lambda a: a.reshape(1, -1)
    n_route = N_GROUPS + N_EXPERTS
    w_route = jnp.concatenate([w_coarse[0], w_fine[0].transpose(1, 0, 2).reshape(d, N_EXPERTS)], axis=1)
    w_route = jnp.pad(w_route, ((0, 0), (0, LANES - n_route)))
    wr_hi = w_route.astype(BF16)
    wr_lo = (w_route - wr_hi.astype(F32)).astype(BF16)
    b_route = jnp.pad(jnp.concatenate([b_coarse[0], b_fine[0].reshape(-1)]), (0, LANES - n_route)).reshape(1, LANES)
    wts = dict(
        g_mix=row2(norm_mix[0]), w_proj=w_proj,
        w_kvt=jnp.concatenate([col(1), col(2)], axis=1).T.astype(BF16),
        wa2=jnp.pad(w_gla_a2[0], ((0, LANES - GLA_RANK), (0, 0))).astype(BF16), ba=row2(b_gla_a[0]),
        g_gla=row2(gla_norm[0]),
        w_gt=col(9).astype(BF16), w_br=w_branch[0].astype(BF16), w_out=w_out[0].astype(BF16),
        g_ffn=row2(norm_ffn[0]), wr_hi=wr_hi, wr_lo=wr_lo, b_r=b_route,
        wg=w_e_gate[0].astype(BF16), wu=w_e_up[0].astype(BF16), wd=w_e_down[0].astype(BF16),
        g_final=row2(norm_final),
    )

    m = mem_prompt.shape[1]
    mk, mv = _memkv(mem_prompt.reshape(bp * m, d), row2(norm_mem[0]), w_mem_kv[0].astype(BF16), tm=min(512, bp * m))
    st0 = jnp.zeros((bp, GLA_DV, GLA_HEADS * GLA_DK), F32)
    y_p, sk_p, sv_p, st_p = _layer(x_prompt, None, st0, (mk.reshape(bp, m, bw), mv.reshape(bp, m, bw)), wts,
                                   tm=min(512, bp * tp), tq_sb=min(256, tp), tt_gla=min(256, tp),
                                   tq_mem=min(512, tp), tm_e=256)

    past = (cache_sb_k[0].transpose(0, 2, 3, 1), cache_sb_v[0].transpose(0, 2, 3, 1))
    y_s, sk_s, sv_s, st_s = _layer(x_sample, past, _state_to_t(state_gla[0]),
                                   (cache_mem_k[0], cache_mem_v[0]), wts,
                                   tm=min(512, bs * ts), tq_sb=ts, tt_gla=ts, tq_mem=ts, tm_e=256)

    hd = lambda a, bb, tt: a.reshape(1, bb, tt, SB_HEADS, SB_DIM)
    return (y_p, y_s,
            sk_p.transpose(0, 3, 1, 2)[None], sv_p.transpose(0, 3, 1, 2)[None],
            _state_from_t(st_p)[None],
            mk.reshape(1, bp, m, MEM_HEADS, MEM_DIM), mv.reshape(1, bp, m, MEM_HEADS, MEM_DIM),
            hd(sk_s, bs, ts), hd(sv_s, bs, ts),
            _state_from_t(st_s)[None])
```

```python
import functools

import jax
import jax.numpy as jnp
from jax import lax
from jax.experimental import pallas as pl
from jax.experimental.pallas import tpu as pltpu

F32 = jnp.float32
BF16 = jnp.bfloat16
EPS = 1e-6

SB_HEADS = 8
SB_DIM = 64
GLA_HEADS = 4
GLA_DK = 64
GLA_DV = 128
GLA_RANK = 16
GLA_TAU = 16.0
GLA_CHUNK = 64
MEM_HEADS = 4
MEM_DIM = 128
N_BRANCH = 3
N_GROUPS = 4
EXPERTS_PER_GROUP = 8
N_EXPERTS = N_GROUPS * EXPERTS_PER_GROUP
TOP_K = 2

LANES = 128
SUBLANES = 8
VMEM_LIMIT = 56 * 1024 * 1024
NEG_BIG = -1e30
_LOG_F32_ZERO = -110.0

_NT = (((1,), (1,)), ((), ()))
_TN = (((0,), (0,)), ((), ()))


def _dot(a, b):
    return jnp.dot(a, b, preferred_element_type=F32)


def _dot_nt(a, b):
    return lax.dot_general(a, b, _NT, preferred_element_type=F32)


def _split_bf16(x):
    hi = x.astype(BF16)
    lo = (x - hi.astype(F32)).astype(BF16)
    return hi, lo


def _rmsnorm(x, g):
    return x * lax.rsqrt(jnp.mean(x * x, axis=-1, keepdims=True) + EPS) * g


def _log_sigmoid(x):
    return jnp.minimum(x, 0.0) - jnp.log1p(jnp.exp(-jnp.abs(x)))


def _resident(a):
    return pl.BlockSpec(a.shape, lambda *_: (0,) * a.ndim, pipeline_mode=pl.Buffered(1))


def _params(*sem):
    return pltpu.CompilerParams(dimension_semantics=sem, vmem_limit_bytes=VMEM_LIMIT)


_C_SQ, _C_SK, _C_SV = 0, 512, 1024
_C_GQ, _C_GK, _C_GV = 1536, 1792, 2048
_C_GR, _C_MQ, _C_GLR = 2560, 3072, 3584
_PROJ_W = 3712


def _proj_kernel(*refs, kv_transposed, key_block):
    if kv_transposed:
        x_ref, g_ref, w_ref, wa2_ref, ba_ref, wkvt_ref = refs[:6]
    else:
        x_ref, g_ref, w_ref, wa2_ref, ba_ref = refs[:5]
    sq_ref, skb_ref, svb_ref, sk_ref, sv_ref, gq_ref, gk_ref, gv_ref, la_ref, gr_ref, mq_ref = refs[-11:]
    xn = _rmsnorm(x_ref[...], g_ref[...]).astype(BF16)

    def seg(start, width):
        return _dot(xn, w_ref[:, start:start + width])

    sq_ref[...] = (seg(_C_SQ, 512) * SB_DIM ** -0.5).astype(BF16)
    if kv_transposed:
        tm = xn.shape[0]
        for half, (bf_ref, heads_ref) in enumerate(((skb_ref, sk_ref), (svb_ref, sv_ref))):
            kv_t = _dot_nt(wkvt_ref[half * 512:(half + 1) * 512, :], xn)
            heads_ref[...] = kv_t.reshape(SB_HEADS, SB_DIM, tm)
            for j in range(tm // key_block):
                bf_ref[j] = kv_t[:, j * key_block:(j + 1) * key_block].astype(BF16)
    else:
        for c0, bf_ref, heads_ref in ((_C_SK, skb_ref, sk_ref), (_C_SV, svb_ref, sv_ref)):
            kv = seg(c0, 512)
            bf_ref[...] = kv.astype(BF16)
            for h in range(SB_HEADS):
                heads_ref[:, h, :] = kv[:, h * SB_DIM:(h + 1) * SB_DIM]
    gq_ref[...] = seg(_C_GQ, 256) * GLA_DK ** -0.5
    gk_ref[...] = seg(_C_GK, 256)
    gv_ref[...] = seg(_C_GV, 512).astype(BF16)
    gr_ref[...] = seg(_C_GR, 512)
    mq_ref[...] = seg(_C_MQ, 512).astype(BF16)
    glr = seg(_C_GLR, LANES).astype(BF16)
    la_ref[...] = _log_sigmoid(_dot(glr, wa2_ref[...]) + ba_ref[...]) * (1.0 / GLA_TAU)


def _proj(x, g, w_proj, wa2, ba, w_kvt, *, tm, seq_len, key_block):
    n, d = x.shape
    kv_transposed = w_kvt is not None
    row = lambda w: pl.BlockSpec((tm, w), lambda i: (i, 0))
    flat = lambda w, dt: (row(w), jax.ShapeDtypeStruct((n, w), dt))
    if kv_transposed:
        nb, per_b = n // seq_len, seq_len // tm
        kv_bf = (pl.BlockSpec((None, tm // key_block, 512, key_block), lambda i: (i // per_b, i % per_b, 0, 0)),
                 jax.ShapeDtypeStruct((nb, seq_len // key_block, 512, key_block), BF16))
        kv_f32 = (pl.BlockSpec((None, SB_HEADS, SB_DIM, tm), lambda i: (i // per_b, 0, 0, i % per_b)),
                  jax.ShapeDtypeStruct((nb, SB_HEADS, SB_DIM, seq_len), F32))
    else:
        kv_bf = flat(512, BF16)
        kv_f32 = (pl.BlockSpec((tm, SB_HEADS, SB_DIM), lambda i: (i, 0, 0)),
                  jax.ShapeDtypeStruct((n, SB_HEADS, SB_DIM), F32))
    outs = [flat(512, BF16), kv_bf, kv_bf, kv_f32, kv_f32,
            flat(256, F32), flat(256, F32), flat(512, BF16), flat(256, F32), flat(512, F32), flat(512, BF16)]
    args = [x, g, w_proj, wa2, ba] + ([w_kvt] if kv_transposed else [])
    return pl.pallas_call(
        functools.partial(_proj_kernel, kv_transposed=kv_transposed, key_block=key_block),
        grid=(n // tm,),
        in_specs=[row(d)] + [_resident(a) for a in args[1:]],
        out_specs=[o[0] for o in outs],
        out_shape=[o[1] for o in outs],
        compiler_params=_params("parallel"),
        name="proj",
    )(*args)


def _memkv_kernel(x_ref, g_ref, w_ref, mk_ref, mv_ref):
    xn = _rmsnorm(x_ref[...], g_ref[...]).astype(BF16)
    half = mk_ref.shape[-1]
    mk_ref[...] = _dot(xn, w_ref[:, :half])
    mv_ref[...] = _dot(xn, w_ref[:, half:])


def _memkv(x, g, w, tm):
    n, d = x.shape
    half = w.shape[1] // 2
    row = lambda wd: pl.BlockSpec((tm, wd), lambda i: (i, 0))
    return pl.pallas_call(
        _memkv_kernel,
        grid=(n // tm,),
        in_specs=[row(d), _resident(g), _resident(w)],
        out_specs=[row(half), row(half)],
        out_shape=[jax.ShapeDtypeStruct((n, half), F32)] * 2,
        compiler_params=_params("parallel"),
        name="memkv",
    )(x, g, w)


def _suffix_matrix(tk):
    j = lax.broadcasted_iota(jnp.int32, (2 * tk, tk), 0)
    s = lax.broadcasted_iota(jnp.int32, (2 * tk, tk), 1)
    return ((j > s) & ((j < tk) | (j > s + tk))).astype(BF16)


def _sb_weights(z, carry, mask):
    lz = jnp.minimum(z, 0.0) - jnp.log(1.0 + jnp.exp(-jnp.abs(z)))
    l1 = lz - z
    if mask is not None:
        l1 = jnp.where(mask, l1, 0.0)
    hi, lo = _split_bf16(l1)
    cum = _dot(jnp.concatenate([hi, lo], axis=1), _suffix_matrix(z.shape[1]))
    p = jnp.exp(lz + cum + carry)
    if mask is not None:
        p = jnp.where(mask, p, 0.0)
    return p.astype(BF16), carry + cum[:, :1] + l1[:, :1]


def _fold_descending(n_blocks, alive, fold_block):
    def cond(state):
        i, live = state
        return (i < n_blocks) & live

    def body(state):
        i, _ = state
        return i + 1, fold_block(n_blocks - 1 - i)

    return lax.while_loop(cond, body, (jnp.int32(0), alive))[1]


def _sb_prompt_kernel(q_ref, k_ref, v_ref, o_ref, acc_ref, car_ref, *, tq, pairs):
    qi = pl.program_id(2)
    heads = 2 * pairs
    lane = lax.broadcasted_iota(jnp.int32, (tq, LANES), 1)
    pair_cols = [slice(j * LANES, (j + 1) * LANES) for j in range(pairs)]
    q_heads = []
    for j in range(pairs):
        q = q_ref[:, pair_cols[j]]
        q_heads += [jnp.where(lane < SB_DIM, q, jnp.zeros_like(q)),
                    jnp.where(lane >= SB_DIM, q, jnp.zeros_like(q))]
    acc_ref[...] = jnp.zeros_like(acc_ref)
    car_ref[...] = jnp.zeros_like(car_ref)

    def block(kb, mask):
        z = jnp.concatenate([_dot(q_heads[h], k_ref[kb, pair_cols[h // 2], :]) for h in range(heads)], axis=0)
        p, carry = _sb_weights(z, car_ref[...], mask)
        car_ref[...] = carry
        for h in range(heads):
            acc_ref[h] += _dot_nt(p[h * tq:(h + 1) * tq], v_ref[kb, pair_cols[h // 2], :])
        return jnp.max(carry) > _LOG_F32_ZERO

    r = lax.broadcasted_iota(jnp.int32, (heads * tq, tq), 0)
    c = lax.broadcasted_iota(jnp.int32, (heads * tq, tq), 1)
    alive = block(qi, c < (r & (tq - 1)))
    _fold_descending(qi, alive, lambda kb: block(kb, None))
    for j in range(pairs):
        o_ref[:, pair_cols[j]] = jnp.where(lane < SB_DIM, acc_ref[2 * j], acc_ref[2 * j + 1]).astype(o_ref.dtype)


def _sb_prompt(q, k_t, v_t, *, tq, pairs):
    b, t, w = q.shape
    assert k_t.shape == (b, t // tq, w, tq) and tq & (tq - 1) == 0
    pw = pairs * LANES
    seq = pl.BlockSpec((None, t // tq, pw, tq), lambda bi, p, qi: (bi, 0, p, 0))
    tile = pl.BlockSpec((None, tq, pw), lambda bi, p, qi: (bi, qi, p))
    return pl.pallas_call(
        functools.partial(_sb_prompt_kernel, tq=tq, pairs=pairs),
        grid=(b, w // pw, t // tq),
        in_specs=[tile, seq, seq],
        out_specs=tile,
        out_shape=jax.ShapeDtypeStruct((b, t, w), BF16),
        scratch_shapes=[pltpu.VMEM((2 * pairs, tq, LANES), F32), pltpu.VMEM((2 * pairs * tq, 1), F32)],
        compiler_params=_params("parallel", "parallel", "arbitrary"),
        name="sb_prompt",
    )(q, k_t, v_t)


def _sb_sample_kernel(q_ref, k_ref, v_ref, pk_ref, pv_ref, o_ref, acc_ref, car_ref, *, tk_past):
    t = q_ref.shape[0]
    past_len = pk_ref.shape[-1]
    acc_ref[...] = jnp.zeros_like(acc_ref)
    car_ref[...] = jnp.zeros_like(car_ref)
    head_cols = [slice(h * SB_DIM, (h + 1) * SB_DIM) for h in range(SB_HEADS)]

    def fold_all_heads(logits, weighted_values, mask):
        z = jnp.concatenate([logits(h, q_ref[:, head_cols[h]]) for h in range(SB_HEADS)], axis=0)
        p, carry = _sb_weights(z, car_ref[...], mask)
        car_ref[...] = carry
        for h in range(SB_HEADS):
            acc_ref[h] += weighted_values(h, p[h * t:(h + 1) * t])
        return jnp.max(carry) > _LOG_F32_ZERO

    r = lax.broadcasted_iota(jnp.int32, (SB_HEADS * t, t), 0)
    c = lax.broadcasted_iota(jnp.int32, (SB_HEADS * t, t), 1)
    alive = fold_all_heads(lambda h, qh: _dot_nt(qh, k_ref[:, head_cols[h]]),
                           lambda h, ph: _dot(ph, v_ref[:, head_cols[h]]),
                           c < (r & (t - 1)))

    def past_block(kb_idx):
        pos = pl.ds(pl.multiple_of(kb_idx * tk_past, tk_past), tk_past)
        return fold_all_heads(lambda h, qh: _dot(qh, pk_ref[h, :, pos].astype(BF16)),
                              lambda h, ph: _dot_nt(ph, pv_ref[h, :, pos].astype(BF16)), None)

    _fold_descending(past_len // tk_past, alive, past_block)
    for h in range(SB_HEADS):
        o_ref[:, head_cols[h]] = acc_ref[h].astype(o_ref.dtype)


def _sb_sample(q, k, v, past_k, past_v, *, tk_past):
    b, t, w = q.shape
    assert t & (t - 1) == 0, "query chunk length must be a power of two"
    cur = pl.BlockSpec((None, t, w), lambda bi: (bi, 0, 0))
    past = pl.BlockSpec((None,) + past_k.shape[1:], lambda bi: (bi, 0, 0, 0))
    return pl.pallas_call(
        functools.partial(_sb_sample_kernel, tk_past=tk_past),
        grid=(b,),
        in_specs=[cur, cur, cur, past, past],
        out_specs=cur,
        out_shape=jax.ShapeDtypeStruct((b, t, w), BF16),
        scratch_shapes=[pltpu.VMEM((SB_HEADS, t, SB_DIM), F32), pltpu.VMEM((SB_HEADS * t, 1), F32)],
        compiler_params=_params("parallel"),
        name="sb_sample",
    )(q, k, v, past_k, past_v)


def _gla_kernel(gq_ref, gk_ref, gv_ref, la_ref, gr_ref, gn_ref, s0_ref, o_ref, st_ref, st_scr, *, n_chunks):
    c_len = GLA_CHUNK
    tt = n_chunks * c_len
    kw = GLA_HEADS * GLA_DK
    t = pl.program_id(1)

    @pl.when(t == 0)
    def _():
        st_scr[...] = s0_ref[...]

    r = lax.broadcasted_iota(jnp.int32, (tt, tt), 0)
    c = lax.broadcasted_iota(jnp.int32, (tt, tt), 1)
    chunk_bits = c_len.bit_length() - 1
    same_chunk = (r >> chunk_bits) == (c >> chunk_bits)
    tril = (same_chunk & (r >= c)).astype(BF16)
    ones = same_chunk.astype(BF16)
    la_hi, la_lo = _split_bf16(la_ref[...])
    b = _dot(tril, la_hi) + _dot(tril, la_lo)
    b_tot = _dot(ones, la_hi) + _dot(ones, la_lo)
    gk = gk_ref[...]
    qd = gq_ref[...] * jnp.exp(b)
    kd = (gk * jnp.exp(-b)).astype(BF16)
    ke = (gk * jnp.exp(b_tot - b)).astype(BF16)
    decay = jnp.exp(b_tot)

    rc = lax.broadcasted_iota(jnp.int32, (GLA_HEADS * c_len, c_len), 0)
    cc = lax.broadcasted_iota(jnp.int32, (GLA_HEADS * c_len, c_len), 1)
    causal = (rc & (c_len - 1)) >= cc
    lane_q = lax.broadcasted_iota(jnp.int32, (c_len, kw), 1)
    lane_s = lax.broadcasted_iota(jnp.int32, (GLA_DV, kw), 1)

    for ci in range(n_chunks):
        rows = slice(ci * c_len, (ci + 1) * c_len)
        qd_c = qd[rows]
        qs = jnp.concatenate(
            [jnp.where((lane_q >= h * GLA_DK) & (lane_q < (h + 1) * GLA_DK), qd_c, 0.0)
             for h in range(GLA_HEADS)], axis=0).astype(BF16)
        att = jnp.where(causal, _dot_nt(qs, kd[rows]), 0.0).astype(BF16)
        st = st_scr[...]
        o_state = _dot_nt(qs, st.astype(BF16))
        v = gv_ref[rows, :]
        upd = lax.dot_general(v, ke[rows], _TN, preferred_element_type=F32)
        st_new = st * decay[ci * c_len:ci * c_len + 1, :]
        for h in range(GLA_HEADS):
            hrows = slice(h * c_len, (h + 1) * c_len)
            cols = slice(h * GLA_DV, (h + 1) * GLA_DV)
            o = _dot(att[hrows], v[:, cols]) + o_state[hrows]
            on = _rmsnorm(o, gn_ref[:, cols])
            gr = gr_ref[rows, cols]
            o_ref[rows, cols] = (on * (gr * jax.nn.sigmoid(gr))).astype(o_ref.dtype)
            in_head = (lane_s >= h * GLA_DK) & (lane_s < (h + 1) * GLA_DK)
            st_new = st_new + jnp.where(in_head, upd[cols], 0.0)
        st_scr[...] = st_new

    @pl.when(t == pl.num_programs(1) - 1)
    def _():
        st_ref[...] = st_scr[...]


def _gla(gq, gk, gv, la, gr, gn, st0, *, tt):
    b, t, kw = gq.shape
    vw = gv.shape[-1]
    tok = lambda w: pl.BlockSpec((None, tt, w), lambda bi, ti: (bi, ti, 0))
    state = pl.BlockSpec((None, GLA_DV, kw), lambda bi, ti: (bi, 0, 0))
    return pl.pallas_call(
        functools.partial(_gla_kernel, n_chunks=tt // GLA_CHUNK),
        grid=(b, t // tt),
        in_specs=[tok(kw), tok(kw), tok(vw), tok(kw), tok(vw), _resident(gn), state],
        out_specs=[tok(vw), state],
        out_shape=[jax.ShapeDtypeStruct((b, t, vw), BF16), jax.ShapeDtypeStruct((b, GLA_DV, kw), F32)],
        scratch_shapes=[pltpu.VMEM((GLA_DV, kw), F32)],
        compiler_params=_params("parallel", "arbitrary"),
        name="gla",
    )(gq, gk, gv, la, gr, gn, st0)


def _mem_kernel(q_ref, mk_ref, mv_ref, o_ref):
    head_major = len(mk_ref.shape) == 3
    for h in range(MEM_HEADS):
        cols = slice(h * MEM_DIM, (h + 1) * MEM_DIM)
        kh = mk_ref[:, h, :] if head_major else mk_ref[:, cols]
        vh = mv_ref[:, h, :] if head_major else mv_ref[:, cols]
        s = _dot_nt(q_ref[:, cols], kh.astype(BF16)) * MEM_DIM ** -0.5
        e = jnp.exp(s - jnp.max(s, axis=-1, keepdims=True))
        p = e / jnp.sum(e, axis=-1, keepdims=True)
        o_ref[:, cols] = _dot(p.astype(BF16), vh.astype(BF16)).astype(o_ref.dtype)


def _mem_attention(q, mk, mv, *, tq):
    b, t, w = q.shape
    tile = pl.BlockSpec((None, tq, w), lambda bi, qi: (bi, qi, 0))
    mem = pl.BlockSpec((None,) + mk.shape[1:], lambda bi, qi: (bi,) + (0,) * (mk.ndim - 1))
    return pl.pallas_call(
        _mem_kernel,
        grid=(b, t // tq),
        in_specs=[tile, mem, mem],
        out_specs=tile,
        out_shape=jax.ShapeDtypeStruct((b, t, w), BF16),
        compiler_params=_params("parallel", "arbitrary"),
        name="mem_attention",
    )(q, mk, mv)


def _merge_kernel(x_ref, osb_ref, ogla_ref, omem_ref, gmix_ref, wgt_ref, wbr_ref, wout_ref,
                  gffn_ref, wr_hi_ref, wr_lo_ref, br_ref, h_ref, ht_ref, idx_ref, wts_ref, cnt_ref, run_ref):
    x = x_ref[...]
    d = x.shape[-1]
    xn = _rmsnorm(x, gmix_ref[...]).astype(BF16)
    mixed = None
    for n, o_ref in enumerate((osb_ref, ogla_ref, omem_ref)):
        gate = jax.nn.sigmoid(_dot(xn, wgt_ref[:, n * d:(n + 1) * d]))
        term = gate * _dot(o_ref[...], wbr_ref[n])
        mixed = term if mixed is None else mixed + term
    h = x + _dot(mixed.astype(BF16), wout_ref[...])
    h_ref[...] = h
    hn = _rmsnorm(h, gffn_ref[...])
    ht_ref[...] = hn

    hn_hi, hn_lo = _split_bf16(hn)
    logits = (_dot(hn_hi, wr_hi_ref[...]) + _dot(hn_lo, wr_hi_ref[...])
              + _dot(hn_hi, wr_lo_ref[...]) + br_ref[...])
    lane = lax.broadcasted_iota(jnp.int32, logits.shape, 1)
    rmax = lambda a: jnp.max(a, axis=-1, keepdims=True)
    rmin = lambda a: jnp.min(a, axis=-1, keepdims=True)
    rsum = lambda a: jnp.sum(a, axis=-1, keepdims=True)

    lc = jnp.where(lane < N_GROUPS, logits, NEG_BIG)
    mc = rmax(lc)
    grp = rmin(jnp.where(lc == mc, lane, LANES))
    p_grp = 1.0 / rsum(jnp.exp(lc - mc))

    lo = N_GROUPS + grp * EXPERTS_PER_GROUP
    in_grp = (lane >= lo) & (lane < lo + EXPERTS_PER_GROUP)
    lf = jnp.where(in_grp, logits, NEG_BIG)
    ef = jnp.exp(lf - rmax(lf))
    pf = jnp.where(in_grp, ef / rsum(ef), -1.0)
    v1 = rmax(pf)
    i1 = rmin(jnp.where(pf == v1, lane, LANES))
    pf2 = jnp.where(lane == i1, -1.0, pf)
    v2 = rmax(pf2)
    i2 = rmin(jnp.where(pf2 == v2, lane, LANES))
    tot = v1 + v2
    e1 = i1 - N_GROUPS
    e2 = i2 - N_GROUPS
    wts_ref[...] = jnp.where(lane == 0, p_grp * (v1 / tot), jnp.where(lane == 1, p_grp * (v2 / tot), 0.0))

    @pl.when(pl.program_id(0) == 0)
    def _():
        run_ref[...] = jnp.zeros_like(run_ref)

    tm = x.shape[0]
    hit1 = lane == e1
    hit2 = lane == e2
    both = (hit1 | hit2).astype(BF16)
    earlier = (lax.broadcasted_iota(jnp.int32, (tm, tm), 1)
               < lax.broadcasted_iota(jnp.int32, (tm, tm), 0)).astype(BF16)
    before = _dot(earlier, both) + run_ref[...]
    r1 = rsum(jnp.where(hit1, before, 0.0)).astype(jnp.int32)
    r2 = rsum(jnp.where(hit2, before, 0.0)).astype(jnp.int32)
    run = before[tm - 1:tm, :] + both[tm - 1:tm, :].astype(F32)
    run_ref[...] = run
    cnt_ref[...] = run
    idx_ref[...] = jnp.where(lane == 0, e1, jnp.where(lane == 1, e2,
                             jnp.where(lane == 2, r1, jnp.where(lane == 3, r2, 0))))


def _merge(x, osb, ogla, omem, gmix, wgt, wbr, wout, gffn, wr_hi, wr_lo, br, *, tm):
    n, d = x.shape
    bw = osb.shape[1]
    row = lambda w: pl.BlockSpec((tm, w), lambda i: (i, 0))
    return pl.pallas_call(
        _merge_kernel,
        grid=(n // tm,),
        in_specs=[row(d), row(bw), row(bw), row(bw), _resident(gmix), _resident(wgt), _resident(wbr),
                  _resident(wout), _resident(gffn), _resident(wr_hi), _resident(wr_lo), _resident(br)],
        out_specs=[row(d), row(d), row(LANES), row(LANES), pl.BlockSpec((1, LANES), lambda i: (0, 0))],
        out_shape=[jax.ShapeDtypeStruct((n, d), F32), jax.ShapeDtypeStruct((n, d), F32),
                   jax.ShapeDtypeStruct((n, LANES), jnp.int32), jax.ShapeDtypeStruct((n, LANES), F32),
                   jax.ShapeDtypeStruct((1, LANES), F32)],
        scratch_shapes=[pltpu.VMEM((1, LANES), F32)],
        compiler_params=_params("arbitrary"),
        name="merge_route",
    )(x, osb, ogla, omem, gmix, wgt, wbr, wout, gffn, wr_hi, wr_lo, br)


def _dispatch_kernel(dest_ref, x_ref, init_ref, xb_ref, sem):
    del init_ref
    tm = x_ref.shape[0]

    def row_copy(r, slot):
        return pltpu.make_async_copy(x_ref.at[pl.ds(r, 1), :], xb_ref.at[pl.ds(slot, 1), :], sem)

    def issue(g, carry):
        base = pl.multiple_of(g * SUBLANES, SUBLANES)
        for u in range(SUBLANES):
            for k in range(TOP_K):
                row_copy(base + u, dest_ref[(base + u) * TOP_K + k]).start()
        return carry

    lax.fori_loop(0, tm // SUBLANES, issue, 0)
    for k in range(TOP_K):
        pltpu.make_async_copy(x_ref, xb_ref.at[pl.ds(0, tm), :], sem).wait()


def _dispatch(x, dest, n_slots, *, tm):
    n, d = x.shape
    return pl.pallas_call(
        _dispatch_kernel,
        grid=(n // tm,),
        in_specs=[pl.BlockSpec((tm * TOP_K,), lambda i: (i,), memory_space=pltpu.SMEM),
                  pl.BlockSpec((tm, d), lambda i: (i, 0)),
                  pl.BlockSpec(memory_space=pl.ANY)],
        out_specs=pl.BlockSpec(memory_space=pl.ANY),
        out_shape=jax.ShapeDtypeStruct((n_slots, d), F32),
        scratch_shapes=[pltpu.SemaphoreType.DMA(())],
        input_output_aliases={2: 0},
        compiler_params=_params("arbitrary"),
        name="dispatch",
    )(dest, x, jnp.zeros((n_slots, d), F32))


def _expert_kernel(blk_e_ref, n_used_ref, x_ref, wg_ref, wu_ref, wd_ref, y_ref):
    i = pl.program_id(0)

    @pl.when(i < n_used_ref[0])
    def _():
        x = x_ref[...].astype(BF16)
        g = _dot(x, wg_ref[...])
        u = _dot(x, wu_ref[...])
        a = (g * jax.nn.sigmoid(g) * u).astype(BF16)
        y_ref[...] = _dot(a, wd_ref[...])

    @pl.when(i >= n_used_ref[0])
    def _():
        y_ref[...] = jnp.zeros_like(y_ref)


def _experts(xb, blk_e, n_used, wg, wu, wd, *, tm):
    p, d = xb.shape
    de = wg.shape[-1]
    grid_spec = pltpu.PrefetchScalarGridSpec(
        num_scalar_prefetch=2,
        grid=(p // tm,),
        in_specs=[pl.BlockSpec((tm, d), lambda i, be, nu: (i, 0)),
                  pl.BlockSpec((None, d, de), lambda i, be, nu: (be[i], 0, 0)),
                  pl.BlockSpec((None, d, de), lambda i, be, nu: (be[i], 0, 0)),
                  pl.BlockSpec((None, de, d), lambda i, be, nu: (be[i], 0, 0))],
        out_specs=pl.BlockSpec((tm, d), lambda i, be, nu: (i, 0)),
    )
    return pl.pallas_call(
        _expert_kernel,
        grid_spec=grid_spec,
        out_shape=jax.ShapeDtypeStruct((p, d), F32),
        compiler_params=_params("arbitrary"),
        name="experts",
    )(blk_e, n_used, xb, wg, wu, wd)


def _final_kernel(pos_ref, pos_next_ref, h_ref, wts_ref, g_ref, yb_ref, o_ref, rows_ref, sem):
    i = pl.program_id(0)
    tm = h_ref.shape[0]
    cur = i % 2

    def gather(slots_ref, buf):
        def issue(g, carry):
            base = pl.multiple_of(g * SUBLANES, SUBLANES)
            for u in range(SUBLANES):
                for k in range(TOP_K):
                    pltpu.make_async_copy(yb_ref.at[pl.ds(slots_ref[(base + u) * TOP_K + k], 1), :],
                                          rows_ref.at[buf, k, pl.ds(base + u, 1), :], sem.at[buf]).start()
            return carry

        lax.fori_loop(0, tm // SUBLANES, issue, 0)

    @pl.when(i == 0)
    def _():
        gather(pos_ref, 0)

    @pl.when(i + 1 < pl.num_programs(0))
    def _():
        gather(pos_next_ref, 1 - cur)

    for k in range(TOP_K):
        pltpu.make_async_copy(yb_ref.at[pl.ds(0, tm), :], rows_ref.at[cur, k], sem.at[cur]).wait()
    w = wts_ref[...]
    h = h_ref[...] + (rows_ref[cur, 0] * w[:, 0:1] + rows_ref[cur, 1] * w[:, 1:2])
    o_ref[...] = _rmsnorm(h, g_ref[...])


def _final(h, yb, pos, wts, g, *, tm):
    n, d = h.shape
    steps = n // tm
    row = lambda w: pl.BlockSpec((tm, w), lambda i: (i, 0))
    slots = lambda nxt: pl.BlockSpec((tm * TOP_K,), lambda i: (jnp.minimum(i + nxt, steps - 1),),
                                     memory_space=pltpu.SMEM)
    return pl.pallas_call(
        _final_kernel,
        grid=(steps,),
        in_specs=[slots(0), slots(1), row(d), row(LANES), _resident(g), pl.BlockSpec(memory_space=pl.ANY)],
        out_specs=row(d),
        out_shape=jax.ShapeDtypeStruct((n, d), F32),
        scratch_shapes=[pltpu.VMEM((2, TOP_K, tm, d), F32), pltpu.SemaphoreType.DMA((2,))],
        compiler_params=_params("arbitrary"),
        name="combine_norm",
    )(pos, pos, h, wts, g, yb)


def _dispatch_plan(route, counts, tm):
    n = route.shape[0]
    a = n * TOP_K
    flat_e = route[:, :TOP_K].reshape(a)
    rank = route[:, TOP_K:2 * TOP_K].reshape(a)
    counts = counts[0, :N_EXPERTS].astype(jnp.int32)
    padded = (counts + tm - 1) // tm * tm
    pend = jnp.cumsum(padded)
    pstart = pend - padded
    experts = jnp.arange(N_EXPERTS, dtype=jnp.int32)
    dest = jnp.sum(jnp.where(flat_e[:, None] == experts[None, :], pstart[None, :], 0), axis=1) + rank
    n_blk = -(-(a + N_EXPERTS * (tm - 1)) // tm)
    blk_start = jnp.arange(n_blk, dtype=jnp.int32) * tm
    blk_e = jnp.minimum(jnp.sum((pend[None, :] <= blk_start[:, None]).astype(jnp.int32), axis=1),
                        N_EXPERTS - 1).astype(jnp.int32)
    n_used = (pend[-1] // tm).astype(jnp.int32).reshape(1)
    return dest.astype(jnp.int32), blk_e, n_used, n_blk


def _moe(ht, route, counts, wts, h, wg, wu, wd, g_final, *, tm_e, tm):
    dest, blk_e, n_used, n_blk = _dispatch_plan(route, counts, tm_e)
    xb = _dispatch(ht, dest, n_blk * tm_e, tm=tm)
    yb = _experts(xb, blk_e, n_used, wg, wu, wd, tm=tm_e)
    return _final(h, yb, dest, wts, g_final, tm=tm)


def _layer(x, sb_past, gla_state, mem_kv, wts, *, tm, tq_sb, tt_gla, tq_mem, tm_e):
    b, t, d = x.shape
    n = b * t
    xf = x.reshape(n, d)
    sq, skb, svb, sk, sv, gq, gk, gv, la, gr, mq = _proj(
        xf, wts["g_mix"], wts["w_proj"], wts["wa2"], wts["ba"], wts["w_kvt"] if sb_past is None else None,
        tm=tm, seq_len=t, key_block=tq_sb)
    r3 = lambda a: a.reshape(b, t, a.shape[-1])
    if sb_past is None:
        o_sb = _sb_prompt(r3(sq), skb, svb, tq=tq_sb, pairs=4)
    else:
        o_sb = _sb_sample(r3(sq), r3(skb), r3(svb), sb_past[0], sb_past[1], tk_past=256)
    o_gla, st = _gla(r3(gq), r3(gk), r3(gv), r3(la), r3(gr), wts["g_gla"], gla_state, tt=tt_gla)
    o_mem = _mem_attention(r3(mq), mem_kv[0], mem_kv[1], tq=tq_mem)
    h, ht, route, rw, counts = _merge(xf, o_sb.reshape(n, -1), o_gla.reshape(n, -1), o_mem.reshape(n, -1),
                                      wts["g_mix"], wts["w_gt"], wts["w_br"], wts["w_out"], wts["g_ffn"],
                                      wts["wr_hi"], wts["wr_lo"], wts["b_r"], tm=tm)
    y = _moe(ht, route, counts, rw, h, wts["wg"], wts["wu"], wts["wd"], wts["g_final"], tm_e=tm_e, tm=tm)
    return y.reshape(b, t, d), sk, sv, st


def _state_to_t(s):
    b = s.shape[0]
    return s.transpose(0, 3, 1, 2).reshape(b, GLA_DV, GLA_HEADS * GLA_DK)


def _state_from_t(st):
    b = st.shape[0]
    return st.reshape(b, GLA_DV, GLA_HEADS, GLA_DK).transpose(0, 2, 3, 1)


def kernel(x_prompt, x_sample, mem_prompt, cache_sb_k, cache_sb_v, state_gla, cache_mem_k, cache_mem_v,
           norm_mix, w_in, w_gla_a2, b_gla_a, gla_norm, norm_mem, w_mem_kv, w_branch, w_out,
           norm_ffn, w_coarse, b_coarse, w_fine, b_fine, w_e_gate, w_e_up, w_e_down, norm_final):
    assert w_in.shape[0] == 1, "single-layer model"
    d = x_prompt.shape[-1]
    bp, tp, _ = x_prompt.shape
    bs, ts, _ = x_sample.shape
    bw = d // 2

    w = w_in[0]
    offs = [0]
    for s in (bw, bw, bw, 256, 256, bw, GLA_RANK, bw, bw, N_BRANCH * d):
        offs.append(offs[-1] + s)
    col = lambda i: w[:, offs[i]:offs[i + 1]]
    w_proj = jnp.concatenate(
        [col(0), col(1), col(2), col(3), col(4), col(5), col(7), col(8),
         jnp.pad(col(6), ((0, 0), (0, LANES - GLA_RANK)))], axis=1).astype(BF16)
    assert w_proj.shape[1] == _PROJ_W
    row2 = lambda a: a.reshape(1, -1)
    n_route = N_GROUPS + N_EXPERTS
    w_route = jnp.concatenate([w_coarse[0], w_fine[0].transpose(1, 0, 2).reshape(d, N_EXPERTS)], axis=1)
    w_route = jnp.pad(w_route, ((0, 0), (0, LANES - n_route)))
    wr_hi = w_route.astype(BF16)
    wr_lo = (w_route - wr_hi.astype(F32)).astype(BF16)
    b_route = jnp.pad(jnp.concatenate([b_coarse[0], b_fine[0].reshape(-1)]), (0, LANES - n_route)).reshape(1, LANES)
    wts = dict(
        g_mix=row2(norm_mix[0]), w_proj=w_proj,
        w_kvt=jnp.concatenate([col(1), col(2)], axis=1).T.astype(BF16),
        wa2=jnp.pad(w_gla_a2[0], ((0, LANES - GLA_RANK), (0, 0))).astype(BF16), ba=row2(b_gla_a[0]),
        g_gla=row2(gla_norm[0]),
        w_gt=col(9).astype(BF16), w_br=w_branch[0].astype(BF16), w_out=w_out[0].astype(BF16),
        g_ffn=row2(norm_ffn[0]), wr_hi=wr_hi, wr_lo=wr_lo, b_r=b_route,
        wg=w_e_gate[0].astype(BF16), wu=w_e_up[0].astype(BF16), wd=w_e_down[0].astype(BF16),
        g_final=row2(norm_final),
    )

    m = mem_prompt.shape[1]
    mk, mv = _memkv(mem_prompt.reshape(bp * m, d), row2(norm_mem[0]), w_mem_kv[0].astype(BF16), tm=min(512, bp * m))
    st0 = jnp.zeros((bp, GLA_DV, GLA_HEADS * GLA_DK), F32)
    y_p, sk_p, sv_p, st_p = _layer(x_prompt, None, st0, (mk.reshape(bp, m, bw), mv.reshape(bp, m, bw)), wts,
                                   tm=min(512, bp * tp), tq_sb=min(256, tp), tt_gla=min(256, tp),
                                   tq_mem=min(512, tp), tm_e=256)

    past = (cache_sb_k[0].transpose(0, 2, 3, 1), cache_sb_v[0].transpose(0, 2, 3, 1))
    y_s, sk_s, sv_s, st_s = _layer(x_sample, past, _state_to_t(state_gla[0]),
                                   (cache_mem_k[0], cache_mem_v[0]), wts,
                                   tm=min(512, bs * ts), tq_sb=ts, tt_gla=ts, tq_mem=ts, tm_e=256)

    hd = lambda a, bb, tt: a.reshape(1, bb, tt, SB_HEADS, SB_DIM)
    return (y_p, y_s,
            sk_p.transpose(0, 3, 1, 2)[None], sv_p.transpose(0, 3, 1, 2)[None],
            _state_from_t(st_p)[None],
            mk.reshape(1, bp, m, MEM_HEADS, MEM_DIM), mv.reshape(1, bp, m, MEM_HEADS, MEM_DIM),
            hd(sk_s, bs, ts), hd(sv_s, bs, ts),
            _state_from_t(st_s)[None])
```

```python
import functools

import jax
import jax.numpy as jnp
from jax import lax
from jax.experimental import pallas as pl
from jax.experimental.pallas import tpu as pltpu

F32 = jnp.float32
BF16 = jnp.bfloat16
EPS = 1e-6

SB_HEADS = 8
SB_DIM = 64
GLA_HEADS = 4
GLA_DK = 64
GLA_DV = 128
GLA_RANK = 16
GLA_TAU = 16.0
GLA_CHUNK = 64
MEM_HEADS = 4
MEM_DIM = 128
N_BRANCH = 3
N_GROUPS = 4
EXPERTS_PER_GROUP = 8
N_EXPERTS = N_GROUPS * EXPERTS_PER_GROUP
TOP_K = 2

LANES = 128
SUBLANES = 8
VMEM_LIMIT = 56 * 1024 * 1024
NEG_BIG = -1e30
_LOG_F32_ZERO = -110.0

_NT = (((1,), (1,)), ((), ()))
_TN = (((0,), (0,)), ((), ()))


def _dot(a, b):
    return jnp.dot(a, b, preferred_element_type=F32)


def _dot_nt(a, b):
    return lax.dot_general(a, b, _NT, preferred_element_type=F32)


def _split_bf16(x):
    hi = x.astype(BF16)
    lo = (x - hi.astype(F32)).astype(BF16)
    return hi, lo


def _rmsnorm(x, g):
    return x * lax.rsqrt(jnp.mean(x * x, axis=-1, keepdims=True) + EPS) * g


def _log_sigmoid(x):
    return jnp.minimum(x, 0.0) - jnp.log1p(jnp.exp(-jnp.abs(x)))


def _resident(a):
    return pl.BlockSpec(a.shape, lambda *_: (0,) * a.ndim, pipeline_mode=pl.Buffered(1))


def _params(*sem):
    return pltpu.CompilerParams(dimension_semantics=sem, vmem_limit_bytes=VMEM_LIMIT)


_C_SQ, _C_SK, _C_SV = 0, 512, 1024
_C_GQ, _C_GK, _C_GV = 1536, 1792, 2048
_C_GR, _C_MQ, _C_GLR = 2560, 3072, 3584
_PROJ_W = 3712


def _proj_kernel(*refs, kv_transposed, key_block):
    if kv_transposed:
        x_ref, g_ref, w_ref, wa2_ref, ba_ref, wkvt_ref = refs[:6]
    else:
        x_ref, g_ref, w_ref, wa2_ref, ba_ref = refs[:5]
    sq_ref, skb_ref, svb_ref, sk_ref, sv_ref, gq_ref, gk_ref, gv_ref, la_ref, gr_ref, mq_ref = refs[-11:]
    xn = _rmsnorm(x_ref[...], g_ref[...]).astype(BF16)

    def seg(start, width):
        return _dot(xn, w_ref[:, start:start + width])

    sq_ref[...] = (seg(_C_SQ, 512) * SB_DIM ** -0.5).astype(BF16)
    if kv_transposed:
        tm = xn.shape[0]
        for half, (bf_ref, heads_ref) in enumerate(((skb_ref, sk_ref), (svb_ref, sv_ref))):
            kv_t = _dot_nt(wkvt_ref[half * 512:(half + 1) * 512, :], xn)
            heads_ref[...] = kv_t.reshape(SB_HEADS, SB_DIM, tm)
            for j in range(tm // key_block):
                bf_ref[j] = kv_t[:, j * key_block:(j + 1) * key_block].astype(BF16)
    else:
        for c0, bf_ref, heads_ref in ((_C_SK, skb_ref, sk_ref), (_C_SV, svb_ref, sv_ref)):
            kv = seg(c0, 512)
            bf_ref[...] = kv.astype(BF16)
            for h in range(SB_HEADS):
                heads_ref[:, h, :] = kv[:, h * SB_DIM:(h + 1) * SB_DIM]
    gq_ref[...] = seg(_C_GQ, 256) * GLA_DK ** -0.5
    gk_ref[...] = seg(_C_GK, 256)
    gv_ref[...] = seg(_C_GV, 512).astype(BF16)
    gr_ref[...] = seg(_C_GR, 512)
    mq_ref[...] = seg(_C_MQ, 512).astype(BF16)
    glr = seg(_C_GLR, LANES).astype(BF16)
    la_ref[...] = _log_sigmoid(_dot(glr, wa2_ref[...]) + ba_ref[...]) * (1.0 / GLA_TAU)


def _proj(x, g, w_proj, wa2, ba, w_kvt, *, tm, seq_len, key_block):
    n, d = x.shape
    kv_transposed = w_kvt is not None
    row = lambda w: pl.BlockSpec((tm, w), lambda i: (i, 0))
    flat = lambda w, dt: (row(w), jax.ShapeDtypeStruct((n, w), dt))
    if kv_transposed:
        nb, per_b = n // seq_len, seq_len // tm
        kv_bf = (pl.BlockSpec((None, tm // key_block, 512, key_block), lambda i: (i // per_b, i % per_b, 0, 0)),
                 jax.ShapeDtypeStruct((nb, seq_len // key_block, 512, key_block), BF16))
        kv_f32 = (pl.BlockSpec((None, SB_HEADS, SB_DIM, tm), lambda i: (i // per_b, 0, 0, i % per_b)),
                  jax.ShapeDtypeStruct((nb, SB_HEADS, SB_DIM, seq_len), F32))
    else:
        kv_bf = flat(512, BF16)
        kv_f32 = (pl.BlockSpec((tm, SB_HEADS, SB_DIM), lambda i: (i, 0, 0)),
                  jax.ShapeDtypeStruct((n, SB_HEADS, SB_DIM), F32))
    outs = [flat(512, BF16), kv_bf, kv_bf, kv_f32, kv_f32,
            flat(256, F32), flat(256, F32), flat(512, BF16), flat(256, F32), flat(512, F32), flat(512, BF16)]
    args = [x, g, w_proj, wa2, ba] + ([w_kvt] if kv_transposed else [])
    return pl.pallas_call(
        functools.partial(_proj_kernel, kv_transposed=kv_transposed, key_block=key_block),
        grid=(n // tm,),
        in_specs=[row(d)] + [_resident(a) for a in args[1:]],
        out_specs=[o[0] for o in outs],
        out_shape=[o[1] for o in outs],
        compiler_params=_params("parallel"),
        name="proj",
    )(*args)


def _memkv_kernel(x_ref, g_ref, w_ref, mk_ref, mv_ref):
    xn = _rmsnorm(x_ref[...], g_ref[...]).astype(BF16)
    half = mk_ref.shape[-1]
    mk_ref[...] = _dot(xn, w_ref[:, :half])
    mv_ref[...] = _dot(xn, w_ref[:, half:])


def _memkv(x, g, w, tm):
    n, d = x.shape
    half = w.shape[1] // 2
    row = lambda wd: pl.BlockSpec((tm, wd), lambda i: (i, 0))
    return pl.pallas_call(
        _memkv_kernel,
        grid=(n // tm,),
        in_specs=[row(d), _resident(g), _resident(w)],
        out_specs=[row(half), row(half)],
        out_shape=[jax.ShapeDtypeStruct((n, half), F32)] * 2,
        compiler_params=_params("parallel"),
        name="memkv",
    )(x, g, w)


def _suffix_matrix(tk):
    j = lax.broadcasted_iota(jnp.int32, (2 * tk, tk), 0)
    s = lax.broadcasted_iota(jnp.int32, (2 * tk, tk), 1)
    return ((j > s) & ((j < tk) | (j > s + tk))).astype(BF16)


def _sb_weights(z, carry, mask):
    lz = jnp.minimum(z, 0.0) - jnp.log(1.0 + jnp.exp(-jnp.abs(z)))
    l1 = lz - z
    if mask is not None:
        l1 = jnp.where(mask, l1, 0.0)
    hi, lo = _split_bf16(l1)
    cum = _dot(jnp.concatenate([hi, lo], axis=1), _suffix_matrix(z.shape[1]))
    p = jnp.exp(lz + cum + carry)
    if mask is not None:
        p = jnp.where(mask, p, 0.0)
    return p.astype(BF16), carry + cum[:, :1] + l1[:, :1]


def _fold_descending(n_blocks, alive, fold_block):
    def cond(state):
        i, live = state
        return (i < n_blocks) & live

    def body(state):
        i, _ = state
        return i + 1, fold_block(n_blocks - 1 - i)

    return lax.while_loop(cond, body, (jnp.int32(0), alive))[1]


def _sb_prompt_kernel(q_ref, k_ref, v_ref, o_ref, acc_ref, car_ref, *, tq, pairs):
    qi = pl.program_id(2)
    heads = 2 * pairs
    lane = lax.broadcasted_iota(jnp.int32, (tq, LANES), 1)
    pair_cols = [slice(j * LANES, (j + 1) * LANES) for j in range(pairs)]
    q_heads = []
    for j in range(pairs):
        q = q_ref[:, pair_cols[j]]
        q_heads += [jnp.where(lane < SB_DIM, q, jnp.zeros_like(q)),
                    jnp.where(lane >= SB_DIM, q, jnp.zeros_like(q))]
    acc_ref[...] = jnp.zeros_like(acc_ref)
    car_ref[...] = jnp.zeros_like(car_ref)

    def block(kb, mask):
        z = jnp.concatenate([_dot(q_heads[h], k_ref[kb, pair_cols[h // 2], :]) for h in range(heads)], axis=0)
        p, carry = _sb_weights(z, car_ref[...], mask)
        car_ref[...] = carry
        for h in range(heads):
            acc_ref[h] += _dot_nt(p[h * tq:(h + 1) * tq], v_ref[kb, pair_cols[h // 2], :])
        return jnp.max(carry) > _LOG_F32_ZERO

    r = lax.broadcasted_iota(jnp.int32, (heads * tq, tq), 0)
    c = lax.broadcasted_iota(jnp.int32, (heads * tq, tq), 1)
    alive = block(qi, c < (r & (tq - 1)))
    _fold_descending(qi, alive, lambda kb: block(kb, None))
    for j in range(pairs):
        o_ref[:, pair_cols[j]] = jnp.where(lane < SB_DIM, acc_ref[2 * j], acc_ref[2 * j + 1]).astype(o_ref.dtype)


def _sb_prompt(q, k_t, v_t, *, tq, pairs):
    b, t, w = q.shape
    assert k_t.shape == (b, t // tq, w, tq) and tq & (tq - 1) == 0
    pw = pairs * LANES
    seq = pl.BlockSpec((None, t // tq, pw, tq), lambda bi, p, qi: (bi, 0, p, 0))
    tile = pl.BlockSpec((None, tq, pw), lambda bi, p, qi: (bi, qi, p))
    return pl.pallas_call(
        functools.partial(_sb_prompt_kernel, tq=tq, pairs=pairs),
        grid=(b, w // pw, t // tq),
        in_specs=[tile, seq, seq],
        out_specs=tile,
        out_shape=jax.ShapeDtypeStruct((b, t, w), BF16),
        scratch_shapes=[pltpu.VMEM((2 * pairs, tq, LANES), F32), pltpu.VMEM((2 * pairs * tq, 1), F32)],
        compiler_params=_params("parallel", "parallel", "arbitrary"),
        name="sb_prompt",
    )(q, k_t, v_t)


def _sb_sample_kernel(q_ref, k_ref, v_ref, pk_ref, pv_ref, o_ref, acc_ref, car_ref, *, tk_past):
    t = q_ref.shape[0]
    past_len = pk_ref.shape[-1]
    acc_ref[...] = jnp.zeros_like(acc_ref)
    car_ref[...] = jnp.zeros_like(car_ref)
    head_cols = [slice(h * SB_DIM, (h + 1) * SB_DIM) for h in range(SB_HEADS)]

    def fold_all_heads(logits, weighted_values, mask):
        z = jnp.concatenate([logits(h, q_ref[:, head_cols[h]]) for h in range(SB_HEADS)], axis=0)
        p, carry = _sb_weights(z, car_ref[...], mask)
        car_ref[...] = carry
        for h in range(SB_HEADS):
            acc_ref[h] += weighted_values(h, p[h * t:(h + 1) * t])
        return jnp.max(carry) > _LOG_F32_ZERO

    r = lax.broadcasted_iota(jnp.int32, (SB_HEADS * t, t), 0)
    c = lax.broadcasted_iota(jnp.int32, (SB_HEADS * t, t), 1)
    alive = fold_all_heads(lambda h, qh: _dot_nt(qh, k_ref[:, head_cols[h]]),
                           lambda h, ph: _dot(ph, v_ref[:, head_cols[h]]),
                           c < (r & (t - 1)))

    def past_block(kb_idx):
        pos = pl.ds(pl.multiple_of(kb_idx * tk_past, tk_past), tk_past)
        return fold_all_heads(lambda h, qh: _dot(qh, pk_ref[h, :, pos].astype(BF16)),
                              lambda h, ph: _dot_nt(ph, pv_ref[h, :, pos].astype(BF16)), None)

    _fold_descending(past_len // tk_past, alive, past_block)
    for h in range(SB_HEADS):
        o_ref[:, head_cols[h]] = acc_ref[h].astype(o_ref.dtype)


def _sb_sample(q, k, v, past_k, past_v, *, tk_past):
    b, t, w = q.shape
    assert t & (t - 1) == 0, "query chunk length must be a power of two"
    cur = pl.BlockSpec((None, t, w), lambda bi: (bi, 0, 0))
    past = pl.BlockSpec((None,) + past_k.shape[1:], lambda bi: (bi, 0, 0, 0))
    return pl.pallas_call(
        functools.partial(_sb_sample_kernel, tk_past=tk_past),
        grid=(b,),
        in_specs=[cur, cur, cur, past, past],
        out_specs=cur,
        out_shape=jax.ShapeDtypeStruct((b, t, w), BF16),
        scratch_shapes=[pltpu.VMEM((SB_HEADS, t, SB_DIM), F32), pltpu.VMEM((SB_HEADS * t, 1), F32)],
        compiler_params=_params("parallel"),
        name="sb_sample",
    )(q, k, v, past_k, past_v)


def _gla_kernel(gq_ref, gk_ref, gv_ref, la_ref, gr_ref, gn_ref, s0_ref, o_ref, st_ref, st_scr, *, n_chunks):
    c_len = GLA_CHUNK
    tt = n_chunks * c_len
    kw = GLA_HEADS * GLA_DK
    t = pl.program_id(1)

    @pl.when(t == 0)
    def _():
        st_scr[...] = s0_ref[...]

    r = lax.broadcasted_iota(jnp.int32, (tt, tt), 0)
    c = lax.broadcasted_iota(jnp.int32, (tt, tt), 1)
    chunk_bits = c_len.bit_length() - 1
    same_chunk = (r >> chunk_bits) == (c >> chunk_bits)
    tril = (same_chunk & (r >= c)).astype(BF16)
    ones = same_chunk.astype(BF16)
    la_hi, la_lo = _split_bf16(la_ref[...])
    b = _dot(tril, la_hi) + _dot(tril, la_lo)
    b_tot = _dot(ones, la_hi) + _dot(ones, la_lo)
    gk = gk_ref[...]
    qd = gq_ref[...] * jnp.exp(b)
    kd = (gk * jnp.exp(-b)).astype(BF16)
    ke = (gk * jnp.exp(b_tot - b)).astype(BF16)
    decay = jnp.exp(b_tot)

    rc = lax.broadcasted_iota(jnp.int32, (GLA_HEADS * c_len, c_len), 0)
    cc = lax.broadcasted_iota(jnp.int32, (GLA_HEADS * c_len, c_len), 1)
    causal = (rc & (c_len - 1)) >= cc
    lane_q = lax.broadcasted_iota(jnp.int32, (c_len, kw), 1)
    lane_s = lax.broadcasted_iota(jnp.int32, (GLA_DV, kw), 1)

    for ci in range(n_chunks):
        rows = slice(ci * c_len, (ci + 1) * c_len)
        qd_c = qd[rows]
        qs = jnp.concatenate(
            [jnp.where((lane_q >= h * GLA_DK) & (lane_q < (h + 1) * GLA_DK), qd_c, 0.0)
             for h in range(GLA_HEADS)], axis=0).astype(BF16)
        att = jnp.where(causal, _dot_nt(qs, kd[rows]), 0.0).astype(BF16)
        st = st_scr[...]
        o_state = _dot_nt(qs, st.astype(BF16))
        v = gv_ref[rows, :]
        upd = lax.dot_general(v, ke[rows], _TN, preferred_element_type=F32)
        st_new = st * decay[ci * c_len:ci * c_len + 1, :]
        for h in range(GLA_HEADS):
            hrows = slice(h * c_len, (h + 1) * c_len)
            cols = slice(h * GLA_DV, (h + 1) * GLA_DV)
            o = _dot(att[hrows], v[:, cols]) + o_state[hrows]
            on = _rmsnorm(o, gn_ref[:, cols])
            gr = gr_ref[rows, cols]
            o_ref[rows, cols] = (on * (gr * jax.nn.sigmoid(gr))).astype(o_ref.dtype)
            in_head = (lane_s >= h * GLA_DK) & (lane_s < (h + 1) * GLA_DK)
            st_new = st_new + jnp.where(in_head, upd[cols], 0.0)
        st_scr[...] = st_new

    @pl.when(t == pl.num_programs(1) - 1)
    def _():
        st_ref[...] = st_scr[...]


def _gla(gq, gk, gv, la, gr, gn, st0, *, tt):
    b, t, kw = gq.shape
    vw = gv.shape[-1]
    tok = lambda w: pl.BlockSpec((None, tt, w), lambda bi, ti: (bi, ti, 0))
    state = pl.BlockSpec((None, GLA_DV, kw), lambda bi, ti: (bi, 0, 0))
    return pl.pallas_call(
        functools.partial(_gla_kernel, n_chunks=tt // GLA_CHUNK),
        grid=(b, t // tt),
        in_specs=[tok(kw), tok(kw), tok(vw), tok(kw), tok(vw), _resident(gn), state],
        out_specs=[tok(vw), state],
        out_shape=[jax.ShapeDtypeStruct((b, t, vw), BF16), jax.ShapeDtypeStruct((b, GLA_DV, kw), F32)],
        scratch_shapes=[pltpu.VMEM((GLA_DV, kw), F32)],
        compiler_params=_params("parallel", "arbitrary"),
        name="gla",
    )(gq, gk, gv, la, gr, gn, st0)


def _mem_kernel(q_ref, mk_ref, mv_ref, o_ref):
    head_major = len(mk_ref.shape) == 3
    for h in range(MEM_HEADS):
        cols = slice(h * MEM_DIM, (h + 1) * MEM_DIM)
        kh = mk_ref[:, h, :] if head_major else mk_ref[:, cols]
        vh = mv_ref[:, h, :] if head_major else mv_ref[:, cols]
        s = _dot_nt(q_ref[:, cols], kh.astype(BF16)) * MEM_DIM ** -0.5
        e = jnp.exp(s - jnp.max(s, axis=-1, keepdims=True))
        p = e / jnp.sum(e, axis=-1, keepdims=True)
        o_ref[:, cols] = _dot(p.astype(BF16), vh.astype(BF16)).astype(o_ref.dtype)


def _mem_attention(q, mk, mv, *, tq):
    b, t, w = q.shape
    tile = pl.BlockSpec((None, tq, w), lambda bi, qi: (bi, qi, 0))
    mem = pl.BlockSpec((None,) + mk.shape[1:], lambda bi, qi: (bi,) + (0,) * (mk.ndim - 1))
    return pl.pallas_call(
        _mem_kernel,
        grid=(b, t // tq),
        in_specs=[tile, mem, mem],
        out_specs=tile,
        out_shape=jax.ShapeDtypeStruct((b, t, w), BF16),
        compiler_params=_params("parallel", "arbitrary"),
        name="mem_attention",
    )(q, mk, mv)


def _merge_kernel(x_ref, osb_ref, ogla_ref, omem_ref, gmix_ref, wgt_ref, wbr_ref, wout_ref,
                  gffn_ref, wr_hi_ref, wr_lo_ref, br_ref, h_ref, ht_ref, idx_ref, wts_ref, cnt_ref, run_ref):
    x = x_ref[...]
    d = x.shape[-1]
    xn = _rmsnorm(x, gmix_ref[...]).astype(BF16)
    mixed = None
    for n, o_ref in enumerate((osb_ref, ogla_ref, omem_ref)):
        gate = jax.nn.sigmoid(_dot(xn, wgt_ref[:, n * d:(n + 1) * d]))
        term = gate * _dot(o_ref[...], wbr_ref[n])
        mixed = term if mixed is None else mixed + term
    h = x + _dot(mixed.astype(BF16), wout_ref[...])
    h_ref[...] = h
    hn = _rmsnorm(h, gffn_ref[...])
    ht_ref[...] = hn

    hn_hi, hn_lo = _split_bf16(hn)
    logits = (_dot(hn_hi, wr_hi_ref[...]) + _dot(hn_lo, wr_hi_ref[...])
              + _dot(hn_hi, wr_lo_ref[...]) + br_ref[...])
    lane = lax.broadcasted_iota(jnp.int32, logits.shape, 1)
    rmax = lambda a: jnp.max(a, axis=-1, keepdims=True)
    rmin = lambda a: jnp.min(a, axis=-1, keepdims=True)
    rsum = lambda a: jnp.sum(a, axis=-1, keepdims=True)

    lc = jnp.where(lane < N_GROUPS, logits, NEG_BIG)
    mc = rmax(lc)
    grp = rmin(jnp.where(lc == mc, lane, LANES))
    p_grp = 1.0 / rsum(jnp.exp(lc - mc))

    lo = N_GROUPS + grp * EXPERTS_PER_GROUP
    in_grp = (lane >= lo) & (lane < lo + EXPERTS_PER_GROUP)
    lf = jnp.where(in_grp, logits, NEG_BIG)
    ef = jnp.exp(lf - rmax(lf))
    pf = jnp.where(in_grp, ef / rsum(ef), -1.0)
    v1 = rmax(pf)
    i1 = rmin(jnp.where(pf == v1, lane, LANES))
    pf2 = jnp.where(lane == i1, -1.0, pf)
    v2 = rmax(pf2)
    i2 = rmin(jnp.where(pf2 == v2, lane, LANES))
    tot = v1 + v2
    e1 = i1 - N_GROUPS
    e2 = i2 - N_GROUPS
    wts_ref[...] = jnp.where(lane == 0, p_grp * (v1 / tot), jnp.where(lane == 1, p_grp * (v2 / tot), 0.0))

    @pl.when(pl.program_id(0) == 0)
    def _():
        run_ref[...] = jnp.zeros_like(run_ref)

    tm = x.shape[0]
    hit1 = lane == e1
    hit2 = lane == e2
    both = (hit1 | hit2).astype(BF16)
    earlier = (lax.broadcasted_iota(jnp.int32, (tm, tm), 1)
               < lax.broadcasted_iota(jnp.int32, (tm, tm), 0)).astype(BF16)
    before = _dot(earlier, both) + run_ref[...]
    r1 = rsum(jnp.where(hit1, before, 0.0)).astype(jnp.int32)
    r2 = rsum(jnp.where(hit2, before, 0.0)).astype(jnp.int32)
    run = before[tm - 1:tm, :] + both[tm - 1:tm, :].astype(F32)
    run_ref[...] = run
    cnt_ref[...] = run
    idx_ref[...] = jnp.where(lane == 0, e1, jnp.where(lane == 1, e2,
                             jnp.where(lane == 2, r1, jnp.where(lane == 3, r2, 0))))


def _merge(x, osb, ogla, omem, gmix, wgt, wbr, wout, gffn, wr_hi, wr_lo, br, *, tm):
    n, d = x.shape
    bw = osb.shape[1]
    row = lambda w: pl.BlockSpec((tm, w), lambda i: (i, 0))
    return pl.pallas_call(
        _merge_kernel,
        grid=(n // tm,),
        in_specs=[row(d), row(bw), row(bw), row(bw), _resident(gmix), _resident(wgt), _resident(wbr),
                  _resident(wout), _resident(gffn), _resident(wr_hi), _resident(wr_lo), _resident(br)],
        out_specs=[row(d), row(d), row(LANES), row(LANES), pl.BlockSpec((1, LANES), lambda i: (0, 0))],
        out_shape=[jax.ShapeDtypeStruct((n, d), F32), jax.ShapeDtypeStruct((n, d), F32),
                   jax.ShapeDtypeStruct((n, LANES), jnp.int32), jax.ShapeDtypeStruct((n, LANES), F32),
                   jax.ShapeDtypeStruct((1, LANES), F32)],
        scratch_shapes=[pltpu.VMEM((1, LANES), F32)],
        compiler_params=_params("arbitrary"),
        name="merge_route",
    )(x, osb, ogla, omem, gmix, wgt, wbr, wout, gffn, wr_hi, wr_lo, br)


def _dispatch_kernel(pend_ref, dest_ref, x_ref, xb_ref, zero_ref, sem, *, tm_e):
    groups = x_ref.shape[0]

    @pl.when(pl.program_id(0) == 0)
    def _():
        zero_ref[...] = jnp.zeros_like(zero_ref)

        def segment_end(e):
            end = pend_ref[e]
            prev = jnp.where(e > 0, pend_ref[jnp.maximum(e - 1, 0)], 0)
            return end, end > prev

        def zero_copy(end):
            first = pl.multiple_of(end - tm_e, tm_e)
            return pltpu.make_async_copy(zero_ref, xb_ref.at[pl.ds(first, tm_e), :], sem)

        def start(e, carry):
            end, used = segment_end(e)

            @pl.when(used)
            def _():
                zero_copy(end).start()
            return carry

        def wait(e, carry):
            end, used = segment_end(e)

            @pl.when(used)
            def _():
                zero_copy(end).wait()
            return carry

        lax.fori_loop(0, N_EXPERTS, start, 0)
        lax.fori_loop(0, N_EXPERTS, wait, 0)

        first_free = pend_ref[N_EXPERTS - 1] // tm_e
        n_blocks = xb_ref.shape[0] // tm_e
        lax.fori_loop(first_free, n_blocks, lambda j, c: (zero_copy((j + 1) * tm_e).start(), c)[1], 0)
        lax.fori_loop(first_free, n_blocks, lambda j, c: (zero_copy((j + 1) * tm_e).wait(), c)[1], 0)

    def issue(g, carry):
        for u in range(SUBLANES):
            for k in range(TOP_K):
                slot = dest_ref[(g * SUBLANES + u) * TOP_K + k]
                pltpu.make_async_copy(x_ref.at[g, pl.ds(u, 1), :], xb_ref.at[pl.ds(slot, 1), :], sem).start()
        return carry

    lax.fori_loop(0, groups, issue, 0)
    tile_rows = xb_ref.at[pl.ds(0, groups * SUBLANES), :]
    for k in range(TOP_K):
        pltpu.make_async_copy(tile_rows, tile_rows, sem).wait()


def _dispatch(x, dest, pend, n_slots, *, tm, tm_e):
    n, d = x.shape
    grid_spec = pltpu.PrefetchScalarGridSpec(
        num_scalar_prefetch=1,
        grid=(n // tm,),
        in_specs=[pl.BlockSpec((tm * TOP_K,), lambda i, pe: (i,), memory_space=pltpu.SMEM),
                  pl.BlockSpec((tm // SUBLANES, SUBLANES, d), lambda i, pe: (i, 0, 0))],
        out_specs=pl.BlockSpec(memory_space=pl.ANY),
        scratch_shapes=[pltpu.VMEM((tm_e, d), F32), pltpu.SemaphoreType.DMA(())],
    )
    return pl.pallas_call(
        functools.partial(_dispatch_kernel, tm_e=tm_e),
        grid_spec=grid_spec,
        out_shape=jax.ShapeDtypeStruct((n_slots, d), F32),
        compiler_params=_params("arbitrary"),
        name="dispatch",
    )(pend, dest, x.reshape(n // SUBLANES, SUBLANES, d))


def _expert_kernel(blk_e_ref, n_used_ref, x_ref, wg_ref, wu_ref, wd_ref, y_ref):
    i = pl.program_id(0)

    @pl.when(i < n_used_ref[0])
    def _():
        x = x_ref[...].astype(BF16)
        g = _dot(x, wg_ref[...])
        u = _dot(x, wu_ref[...])
        a = (g * jax.nn.sigmoid(g) * u).astype(BF16)
        y_ref[...] = _dot(a, wd_ref[...])

    @pl.when(i >= n_used_ref[0])
    def _():
        y_ref[...] = jnp.zeros_like(y_ref)


def _experts(xb, blk_e, n_used, wg, wu, wd, *, tm):
    p, d = xb.shape
    de = wg.shape[-1]
    grid_spec = pltpu.PrefetchScalarGridSpec(
        num_scalar_prefetch=2,
        grid=(p // tm,),
        in_specs=[pl.BlockSpec((tm, d), lambda i, be, nu: (jnp.minimum(i, nu[0] - 1), 0)),
                  pl.BlockSpec((None, d, de), lambda i, be, nu: (be[i], 0, 0)),
                  pl.BlockSpec((None, d, de), lambda i, be, nu: (be[i], 0, 0)),
                  pl.BlockSpec((None, de, d), lambda i, be, nu: (be[i], 0, 0))],
        out_specs=pl.BlockSpec((tm, d), lambda i, be, nu: (i, 0)),
    )
    return pl.pallas_call(
        _expert_kernel,
        grid_spec=grid_spec,
        out_shape=jax.ShapeDtypeStruct((p, d), F32),
        compiler_params=_params("arbitrary"),
        name="experts",
    )(blk_e, n_used, xb, wg, wu, wd)


def _final_kernel(pos_ref, pos_next_ref, h_ref, wts_ref, g_ref, yb_ref, o_ref, rows_ref, sem):
    i = pl.program_id(0)
    tm, d = h_ref.shape
    cur = i % 2

    def gather(slots_ref, buf):
        def issue(g, carry):
            for u in range(SUBLANES):
                for k in range(TOP_K):
                    slot = slots_ref[(g * SUBLANES + u) * TOP_K + k]
                    pltpu.make_async_copy(yb_ref.at[pl.ds(slot, 1), :],
                                          rows_ref.at[buf, k, g, pl.ds(u, 1), :], sem.at[buf]).start()
            return carry

        lax.fori_loop(0, tm // SUBLANES, issue, 0)

    @pl.when(i == 0)
    def _():
        gather(pos_ref, 0)

    @pl.when(i + 1 < pl.num_programs(0))
    def _():
        gather(pos_next_ref, 1 - cur)

    for k in range(TOP_K):
        pltpu.make_async_copy(rows_ref.at[cur, k], rows_ref.at[cur, k], sem.at[cur]).wait()
    w = wts_ref[...]
    y0 = rows_ref[cur, 0].reshape(tm, d)
    y1 = rows_ref[cur, 1].reshape(tm, d)
    o_ref[...] = _rmsnorm(h_ref[...] + (y0 * w[:, 0:1] + y1 * w[:, 1:2]), g_ref[...])


def _final(h, yb, pos, wts, g, *, tm):
    n, d = h.shape
    steps = n // tm
    row = lambda w: pl.BlockSpec((tm, w), lambda i: (i, 0))
    slots = lambda nxt: pl.BlockSpec((tm * TOP_K,), lambda i: (jnp.minimum(i + nxt, steps - 1),),
                                     memory_space=pltpu.SMEM)
    return pl.pallas_call(
        _final_kernel,
        grid=(steps,),
        in_specs=[slots(0), slots(1), row(d), row(LANES), _resident(g), pl.BlockSpec(memory_space=pl.ANY)],
        out_specs=row(d),
        out_shape=jax.ShapeDtypeStruct((n, d), F32),
        scratch_shapes=[pltpu.VMEM((2, TOP_K, tm // SUBLANES, SUBLANES, d), F32),
                        pltpu.SemaphoreType.DMA((2,))],
        compiler_params=_params("arbitrary"),
        name="combine_norm",
    )(pos, pos, h, wts, g, yb)


def _dispatch_plan(route, counts, tm):
    n = route.shape[0]
    a = n * TOP_K
    flat_e = route[:, :TOP_K].reshape(a)
    rank = route[:, TOP_K:2 * TOP_K].reshape(a)
    counts = counts[0, :N_EXPERTS].astype(jnp.int32)
    padded = (counts + tm - 1) // tm * tm
    pend = jnp.cumsum(padded)
    pstart = pend - padded
    experts = jnp.arange(N_EXPERTS, dtype=jnp.int32)
    dest = jnp.sum(jnp.where(flat_e[:, None] == experts[None, :], pstart[None, :], 0), axis=1) + rank
    n_blk = -(-(a + N_EXPERTS * (tm - 1)) // tm)
    blk_start = jnp.arange(n_blk, dtype=jnp.int32) * tm
    blk_e = jnp.minimum(jnp.sum((pend[None, :] <= blk_start[:, None]).astype(jnp.int32), axis=1),
                        N_EXPERTS - 1).astype(jnp.int32)
    n_used = (pend[-1] // tm).astype(jnp.int32).reshape(1)
    return dest.astype(jnp.int32), pend.astype(jnp.int32), blk_e, n_used, n_blk


def _moe(ht, route, counts, wts, h, wg, wu, wd, g_final, *, tm_e, tm):
    dest, pend, blk_e, n_used, n_blk = _dispatch_plan(route, counts, tm_e)
    xb = _dispatch(ht, dest, pend, n_blk * tm_e, tm=tm, tm_e=tm_e)
    yb = _experts(xb, blk_e, n_used, wg, wu, wd, tm=tm_e)
    return _final(h, yb, dest, wts, g_final, tm=tm)


def _layer(x, sb_past, gla_state, mem_kv, wts, *, tm, tq_sb, tt_gla, tq_mem, tm_e):
    b, t, d = x.shape
    n = b * t
    xf = x.reshape(n, d)
    sq, skb, svb, sk, sv, gq, gk, gv, la, gr, mq = _proj(
        xf, wts["g_mix"], wts["w_proj"], wts["wa2"], wts["ba"], wts["w_kvt"] if sb_past is None else None,
        tm=tm, seq_len=t, key_block=tq_sb)
    r3 = lambda a: a.reshape(b, t, a.shape[-1])
    if sb_past is None:
        o_sb = _sb_prompt(r3(sq), skb, svb, tq=tq_sb, pairs=4)
    else:
        o_sb = _sb_sample(r3(sq), r3(skb), r3(svb), sb_past[0], sb_past[1], tk_past=256)
    o_gla, st = _gla(r3(gq), r3(gk), r3(gv), r3(la), r3(gr), wts["g_gla"], gla_state, tt=tt_gla)
    o_mem = _mem_attention(r3(mq), mem_kv[0], mem_kv[1], tq=tq_mem)
    h, ht, route, rw, counts = _merge(xf, o_sb.reshape(n, -1), o_gla.reshape(n, -1), o_mem.reshape(n, -1),
                                      wts["g_mix"], wts["w_gt"], wts["w_br"], wts["w_out"], wts["g_ffn"],
                                      wts["wr_hi"], wts["wr_lo"], wts["b_r"], tm=tm)
    y = _moe(ht, route, counts, rw, h, wts["wg"], wts["wu"], wts["wd"], wts["g_final"], tm_e=tm_e, tm=tm)
    return y.reshape(b, t, d), sk, sv, st


def _state_to_t(s):
    b = s.shape[0]
    return s.transpose(0, 3, 1, 2).reshape(b, GLA_DV, GLA_HEADS * GLA_DK)


def _state_from_t(st):
    b = st.shape[0]
    return st.reshape(b, GLA_DV, GLA_HEADS, GLA_DK).transpose(0, 2, 3, 1)


def kernel(x_prompt, x_sample, mem_prompt, cache_sb_k, cache_sb_v, state_gla, cache_mem_k, cache_mem_v,
           norm_mix, w_in, w_gla_a2, b_gla_a, gla_norm, norm_mem, w_mem_kv, w_branch, w_out,
           norm_ffn, w_coarse, b_coarse, w_fine, b_fine, w_e_gate, w_e_up, w_e_down, norm_final):
    assert w_in.shape[0] == 1, "single-layer model"
    d = x_prompt.shape[-1]
    bp, tp, _ = x_prompt.shape
    bs, ts, _ = x_sample.shape
    bw = d // 2

    w = w_in[0]
    offs = [0]
    for s in (bw, bw, bw, 256, 256, bw, GLA_RANK, bw, bw, N_BRANCH * d):
        offs.append(offs[-1] + s)
    col = lambda i: w[:, offs[i]:offs[i + 1]]
    w_proj = jnp.concatenate(
        [col(0), col(1), col(2), col(3), col(4), col(5), col(7), col(8),
         jnp.pad(col(6), ((0, 0), (0, LANES - GLA_RANK)))], axis=1).astype(BF16)
    assert w_proj.shape[1] == _PROJ_W
    row2 = lambda a: a.reshape(1, -1)
    n_route = N_GROUPS + N_EXPERTS
    w_route = jnp.concatenate([w_coarse[0], w_fine[0].transpose(1, 0, 2).reshape(d, N_EXPERTS)], axis=1)
    w_route = jnp.pad(w_route, ((0, 0), (0, LANES - n_route)))
    wr_hi = w_route.astype(BF16)
    wr_lo = (w_route - wr_hi.astype(F32)).astype(BF16)
    b_route = jnp.pad(jnp.concatenate([b_coarse[0], b_fine[0].reshape(-1)]), (0, LANES - n_route)).reshape(1, LANES)
    wts = dict(
        g_mix=row2(norm_mix[0]), w_proj=w_proj,
        w_kvt=jnp.concatenate([col(1), col(2)], axis=1).T.astype(BF16),
        wa2=jnp.pad(w_gla_a2[0], ((0, LANES - GLA_RANK), (0, 0))).astype(BF16), ba=row2(b_gla_a[0]),
        g_gla=row2(gla_norm[0]),
        w_gt=col(9).astype(BF16), w_br=w_branch[0].astype(BF16), w_out=w_out[0].astype(BF16),
        g_ffn=row2(norm_ffn[0]), wr_hi=wr_hi, wr_lo=wr_lo, b_r=b_route,
        wg=w_e_gate[0].astype(BF16), wu=w_e_up[0].astype(BF16), wd=w_e_down[0].astype(BF16),
        g_final=row2(norm_final),
    )

    m = mem_prompt.shape[1]
    mk, mv = _memkv(mem_prompt.reshape(bp * m, d), row2(norm_mem[0]), w_mem_kv[0].astype(BF16), tm=min(512, bp * m))
    st0 = jnp.zeros((bp, GLA_DV, GLA_HEADS * GLA_DK), F32)
    y_p, sk_p, sv_p, st_p = _layer(x_prompt, None, st0, (mk.reshape(bp, m, bw), mv.reshape(bp, m, bw)), wts,
                                   tm=min(512, bp * tp), tq_sb=min(256, tp), tt_gla=min(256, tp),
                                   tq_mem=min(512, tp), tm_e=512)

    past = (cache_sb_k[0].transpose(0, 2, 3, 1), cache_sb_v[0].transpose(0, 2, 3, 1))
    y_s, sk_s, sv_s, st_s = _layer(x_sample, past, _state_to_t(state_gla[0]),
                                   (cache_mem_k[0], cache_mem_v[0]), wts,
                                   tm=min(512, bs * ts), tq_sb=ts, tt_gla=ts, tq_mem=ts, tm_e=256)

    hd = lambda a, bb, tt: a.reshape(1, bb, tt, SB_HEADS, SB_DIM)
    return (y_p, y_s,
            sk_p.transpose(0, 3, 1, 2)[None], sv_p.transpose(0, 3, 1, 2)[None],
            _state_from_t(st_p)[None],
            mk.reshape(1, bp, m, MEM_HEADS, MEM_DIM), mv.reshape(1, bp, m, MEM_HEADS, MEM_DIM),
            hd(sk_s, bs, ts), hd(sv_s, bs, ts),
            _state_from_t(st_s)[None])
```

```python
import functools

import jax
import jax.numpy as jnp
from jax import lax
from jax.experimental import pallas as pl
from jax.experimental.pallas import tpu as pltpu

F32 = jnp.float32
BF16 = jnp.bfloat16
EPS = 1e-6

SB_HEADS = 8
SB_DIM = 64
GLA_HEADS = 4
GLA_DK = 64
GLA_DV = 128
GLA_RANK = 16
GLA_TAU = 16.0
GLA_CHUNK = 64
MEM_HEADS = 4
MEM_DIM = 128
N_BRANCH = 3
N_GROUPS = 4
EXPERTS_PER_GROUP = 8
N_EXPERTS = N_GROUPS * EXPERTS_PER_GROUP
TOP_K = 2

LANES = 128
SUBLANES = 8
VMEM_LIMIT = 56 * 1024 * 1024
NEG_BIG = -1e30
_LOG_F32_ZERO = -110.0

_NT = (((1,), (1,)), ((), ()))
_TN = (((0,), (0,)), ((), ()))


def _dot(a, b):
    return jnp.dot(a, b, preferred_element_type=F32)


def _dot_nt(a, b):
    return lax.dot_general(a, b, _NT, preferred_element_type=F32)


def _split_bf16(x):
    hi = x.astype(BF16)
    lo = (x - hi.astype(F32)).astype(BF16)
    return hi, lo


def _rmsnorm(x, g):
    return x * lax.rsqrt(jnp.mean(x * x, axis=-1, keepdims=True) + EPS) * g


def _log_sigmoid(x):
    return jnp.minimum(x, 0.0) - jnp.log1p(jnp.exp(-jnp.abs(x)))


def _resident(a):
    return pl.BlockSpec(a.shape, lambda *_: (0,) * a.ndim, pipeline_mode=pl.Buffered(1))


def _params(*sem):
    return pltpu.CompilerParams(dimension_semantics=sem, vmem_limit_bytes=VMEM_LIMIT)


_C_SQ, _C_SK, _C_SV = 0, 512, 1024
_C_GQ, _C_GK, _C_GV = 1536, 1792, 2048
_C_GR, _C_MQ, _C_GLR = 2560, 3072, 3584
_PROJ_W = 3712


def _proj_kernel(*refs, kv_transposed, key_block):
    if kv_transposed:
        x_ref, g_ref, w_ref, wa2_ref, ba_ref, wkvt_ref = refs[:6]
    else:
        x_ref, g_ref, w_ref, wa2_ref, ba_ref = refs[:5]
    sq_ref, skb_ref, svb_ref, sk_ref, sv_ref, gq_ref, gk_ref, gv_ref, la_ref, gr_ref, mq_ref = refs[-11:]
    xn = _rmsnorm(x_ref[...], g_ref[...]).astype(BF16)

    def seg(start, width):
        return _dot(xn, w_ref[:, start:start + width])

    sq_ref[...] = (seg(_C_SQ, 512) * SB_DIM ** -0.5).astype(BF16)
    if kv_transposed:
        tm = xn.shape[0]
        for half, (bf_ref, heads_ref) in enumerate(((skb_ref, sk_ref), (svb_ref, sv_ref))):
            kv_t = _dot_nt(wkvt_ref[half * 512:(half + 1) * 512, :], xn)
            heads_ref[...] = kv_t.reshape(SB_HEADS, SB_DIM, tm)
            for j in range(tm // key_block):
                bf_ref[j] = kv_t[:, j * key_block:(j + 1) * key_block].astype(BF16)
    else:
        for c0, bf_ref, heads_ref in ((_C_SK, skb_ref, sk_ref), (_C_SV, svb_ref, sv_ref)):
            kv = seg(c0, 512)
            bf_ref[...] = kv.astype(BF16)
            for h in range(SB_HEADS):
                heads_ref[:, h, :] = kv[:, h * SB_DIM:(h + 1) * SB_DIM]
    gq_ref[...] = seg(_C_GQ, 256) * GLA_DK ** -0.5
    gk_ref[...] = seg(_C_GK, 256)
    gv_ref[...] = seg(_C_GV, 512).astype(BF16)
    gr_ref[...] = seg(_C_GR, 512)
    mq_ref[...] = seg(_C_MQ, 512).astype(BF16)
    glr = seg(_C_GLR, LANES).astype(BF16)
    la_ref[...] = _log_sigmoid(_dot(glr, wa2_ref[...]) + ba_ref[...]) * (1.0 / GLA_TAU)


def _proj(x, g, w_proj, wa2, ba, w_kvt, *, tm, seq_len, key_block):
    n, d = x.shape
    kv_transposed = w_kvt is not None
    row = lambda w: pl.BlockSpec((tm, w), lambda i: (i, 0))
    flat = lambda w, dt: (row(w), jax.ShapeDtypeStruct((n, w), dt))
    if kv_transposed:
        nb, per_b = n // seq_len, seq_len // tm
        kv_bf = (pl.BlockSpec((None, tm // key_block, 512, key_block), lambda i: (i // per_b, i % per_b, 0, 0)),
                 jax.ShapeDtypeStruct((nb, seq_len // key_block, 512, key_block), BF16))
        kv_f32 = (pl.BlockSpec((None, SB_HEADS, SB_DIM, tm), lambda i: (i // per_b, 0, 0, i % per_b)),
                  jax.ShapeDtypeStruct((nb, SB_HEADS, SB_DIM, seq_len), F32))
    else:
        kv_bf = flat(512, BF16)
        kv_f32 = (pl.BlockSpec((tm, SB_HEADS, SB_DIM), lambda i: (i, 0, 0)),
                  jax.ShapeDtypeStruct((n, SB_HEADS, SB_DIM), F32))
    outs = [flat(512, BF16), kv_bf, kv_bf, kv_f32, kv_f32,
            flat(256, F32), flat(256, F32), flat(512, BF16), flat(256, F32), flat(512, F32), flat(512, BF16)]
    args = [x, g, w_proj, wa2, ba] + ([w_kvt] if kv_transposed else [])
    return pl.pallas_call(
        functools.partial(_proj_kernel, kv_transposed=kv_transposed, key_block=key_block),
        grid=(n // tm,),
        in_specs=[row(d)] + [_resident(a) for a in args[1:]],
        out_specs=[o[0] for o in outs],
        out_shape=[o[1] for o in outs],
        compiler_params=_params("parallel"),
        name="proj",
    )(*args)


def _memkv_kernel(x_ref, g_ref, w_ref, mk_ref, mv_ref):
    xn = _rmsnorm(x_ref[...], g_ref[...]).astype(BF16)
    half = mk_ref.shape[-1]
    mk_ref[...] = _dot(xn, w_ref[:, :half])
    mv_ref[...] = _dot(xn, w_ref[:, half:])


def _memkv(x, g, w, tm):
    n, d = x.shape
    half = w.shape[1] // 2
    row = lambda wd: pl.BlockSpec((tm, wd), lambda i: (i, 0))
    return pl.pallas_call(
        _memkv_kernel,
        grid=(n // tm,),
        in_specs=[row(d), _resident(g), _resident(w)],
        out_specs=[row(half), row(half)],
        out_shape=[jax.ShapeDtypeStruct((n, half), F32)] * 2,
        compiler_params=_params("parallel"),
        name="memkv",
    )(x, g, w)


def _suffix_matrix(tk):
    j = lax.broadcasted_iota(jnp.int32, (2 * tk, tk), 0)
    s = lax.broadcasted_iota(jnp.int32, (2 * tk, tk), 1)
    return ((j > s) & ((j < tk) | (j > s + tk))).astype(BF16)


def _sb_weights(z, carry, mask):
    lz = jnp.minimum(z, 0.0) - jnp.log(1.0 + jnp.exp(-jnp.abs(z)))
    l1 = lz - z
    if mask is not None:
        l1 = jnp.where(mask, l1, 0.0)
    hi, lo = _split_bf16(l1)
    cum = _dot(jnp.concatenate([hi, lo], axis=1), _suffix_matrix(z.shape[1]))
    p = jnp.exp(lz + cum + carry)
    if mask is not None:
        p = jnp.where(mask, p, 0.0)
    return p.astype(BF16), carry + cum[:, :1] + l1[:, :1]


def _fold_descending(n_blocks, alive, fold_block):
    def cond(state):
        i, live = state
        return (i < n_blocks) & live

    def body(state):
        i, _ = state
        return i + 1, fold_block(n_blocks - 1 - i)

    return lax.while_loop(cond, body, (jnp.int32(0), alive))[1]


def _sb_prompt_kernel(q_ref, k_ref, v_ref, o_ref, acc_ref, car_ref, *, tq, pairs):
    qi = pl.program_id(2)
    heads = 2 * pairs
    lane = lax.broadcasted_iota(jnp.int32, (tq, LANES), 1)
    pair_cols = [slice(j * LANES, (j + 1) * LANES) for j in range(pairs)]
    q_heads = []
    for j in range(pairs):
        q = q_ref[:, pair_cols[j]]
        q_heads += [jnp.where(lane < SB_DIM, q, jnp.zeros_like(q)),
                    jnp.where(lane >= SB_DIM, q, jnp.zeros_like(q))]
    acc_ref[...] = jnp.zeros_like(acc_ref)
    car_ref[...] = jnp.zeros_like(car_ref)

    def block(kb, mask):
        z = jnp.concatenate([_dot(q_heads[h], k_ref[kb, pair_cols[h // 2], :]) for h in range(heads)], axis=0)
        p, carry = _sb_weights(z, car_ref[...], mask)
        car_ref[...] = carry
        for h in range(heads):
            acc_ref[h] += _dot_nt(p[h * tq:(h + 1) * tq], v_ref[kb, pair_cols[h // 2], :])
        return jnp.max(carry) > _LOG_F32_ZERO

    r = lax.broadcasted_iota(jnp.int32, (heads * tq, tq), 0)
    c = lax.broadcasted_iota(jnp.int32, (heads * tq, tq), 1)
    alive = block(qi, c < (r & (tq - 1)))
    _fold_descending(qi, alive, lambda kb: block(kb, None))
    for j in range(pairs):
        o_ref[:, pair_cols[j]] = jnp.where(lane < SB_DIM, acc_ref[2 * j], acc_ref[2 * j + 1]).astype(o_ref.dtype)


def _sb_prompt(q, k_t, v_t, *, tq, pairs):
    b, t, w = q.shape
    assert k_t.shape == (b, t // tq, w, tq) and tq & (tq - 1) == 0
    pw = pairs * LANES
    seq = pl.BlockSpec((None, t // tq, pw, tq), lambda bi, p, qi: (bi, 0, p, 0))
    tile = pl.BlockSpec((None, tq, pw), lambda bi, p, qi: (bi, qi, p))
    return pl.pallas_call(
        functools.partial(_sb_prompt_kernel, tq=tq, pairs=pairs),
        grid=(b, w // pw, t // tq),
        in_specs=[tile, seq, seq],
        out_specs=tile,
        out_shape=jax.ShapeDtypeStruct((b, t, w), BF16),
        scratch_shapes=[pltpu.VMEM((2 * pairs, tq, LANES), F32), pltpu.VMEM((2 * pairs * tq, 1), F32)],
        compiler_params=_params("parallel", "parallel", "arbitrary"),
        name="sb_prompt",
    )(q, k_t, v_t)


def _sb_sample_kernel(q_ref, k_ref, v_ref, pk_ref, pv_ref, o_ref, acc_ref, car_ref, *, tk_past):
    t = q_ref.shape[0]
    past_len = pk_ref.shape[-1]
    acc_ref[...] = jnp.zeros_like(acc_ref)
    car_ref[...] = jnp.zeros_like(car_ref)
    head_cols = [slice(h * SB_DIM, (h + 1) * SB_DIM) for h in range(SB_HEADS)]

    def fold_all_heads(logits, weighted_values, mask):
        z = jnp.concatenate([logits(h, q_ref[:, head_cols[h]]) for h in range(SB_HEADS)], axis=0)
        p, carry = _sb_weights(z, car_ref[...], mask)
        car_ref[...] = carry
        for h in range(SB_HEADS):
            acc_ref[h] += weighted_values(h, p[h * t:(h + 1) * t])
        return jnp.max(carry) > _LOG_F32_ZERO

    r = lax.broadcasted_iota(jnp.int32, (SB_HEADS * t, t), 0)
    c = lax.broadcasted_iota(jnp.int32, (SB_HEADS * t, t), 1)
    alive = fold_all_heads(lambda h, qh: _dot_nt(qh, k_ref[:, head_cols[h]]),
                           lambda h, ph: _dot(ph, v_ref[:, head_cols[h]]),
                           c < (r & (t - 1)))

    def past_block(kb_idx):
        pos = pl.ds(pl.multiple_of(kb_idx * tk_past, tk_past), tk_past)
        return fold_all_heads(lambda h, qh: _dot(qh, pk_ref[h, :, pos].astype(BF16)),
                              lambda h, ph: _dot_nt(ph, pv_ref[h, :, pos].astype(BF16)), None)

    _fold_descending(past_len // tk_past, alive, past_block)
    for h in range(SB_HEADS):
        o_ref[:, head_cols[h]] = acc_ref[h].astype(o_ref.dtype)


def _sb_sample(q, k, v, past_k, past_v, *, tk_past):
    b, t, w = q.shape
    assert t & (t - 1) == 0, "query chunk length must be a power of two"
    cur = pl.BlockSpec((None, t, w), lambda bi: (bi, 0, 0))
    past = pl.BlockSpec((None,) + past_k.shape[1:], lambda bi: (bi, 0, 0, 0))
    return pl.pallas_call(
        functools.partial(_sb_sample_kernel, tk_past=tk_past),
        grid=(b,),
        in_specs=[cur, cur, cur, past, past],
        out_specs=cur,
        out_shape=jax.ShapeDtypeStruct((b, t, w), BF16),
        scratch_shapes=[pltpu.VMEM((SB_HEADS, t, SB_DIM), F32), pltpu.VMEM((SB_HEADS * t, 1), F32)],
        compiler_params=_params("parallel"),
        name="sb_sample",
    )(q, k, v, past_k, past_v)


def _gla_kernel(gq_ref, gk_ref, gv_ref, la_ref, gr_ref, gn_ref, s0_ref, o_ref, st_ref, st_scr, *, n_chunks):
    c_len = GLA_CHUNK
    tt = n_chunks * c_len
    kw = GLA_HEADS * GLA_DK
    t = pl.program_id(1)

    @pl.when(t == 0)
    def _():
        st_scr[...] = s0_ref[...]

    r = lax.broadcasted_iota(jnp.int32, (tt, tt), 0)
    c = lax.broadcasted_iota(jnp.int32, (tt, tt), 1)
    chunk_bits = c_len.bit_length() - 1
    same_chunk = (r >> chunk_bits) == (c >> chunk_bits)
    tril = (same_chunk & (r >= c)).astype(BF16)
    ones = same_chunk.astype(BF16)
    la_hi, la_lo = _split_bf16(la_ref[...])
    b = _dot(tril, la_hi) + _dot(tril, la_lo)
    b_tot = _dot(ones, la_hi) + _dot(ones, la_lo)
    gk = gk_ref[...]
    qd = gq_ref[...] * jnp.exp(b)
    kd = (gk * jnp.exp(-b)).astype(BF16)
    ke = (gk * jnp.exp(b_tot - b)).astype(BF16)
    decay = jnp.exp(b_tot)

    rc = lax.broadcasted_iota(jnp.int32, (GLA_HEADS * c_len, c_len), 0)
    cc = lax.broadcasted_iota(jnp.int32, (GLA_HEADS * c_len, c_len), 1)
    causal = (rc & (c_len - 1)) >= cc
    lane_q = lax.broadcasted_iota(jnp.int32, (c_len, kw), 1)
    lane_s = lax.broadcasted_iota(jnp.int32, (GLA_DV, kw), 1)

    for ci in range(n_chunks):
        rows = slice(ci * c_len, (ci + 1) * c_len)
        qd_c = qd[rows]
        qs = jnp.concatenate(
            [jnp.where((lane_q >= h * GLA_DK) & (lane_q < (h + 1) * GLA_DK), qd_c, 0.0)
             for h in range(GLA_HEADS)], axis=0).astype(BF16)
        att = jnp.where(causal, _dot_nt(qs, kd[rows]), 0.0).astype(BF16)
        st = st_scr[...]
        o_state = _dot_nt(qs, st.astype(BF16))
        v = gv_ref[rows, :]
        upd = lax.dot_general(v, ke[rows], _TN, preferred_element_type=F32)
        st_new = st * decay[ci * c_len:ci * c_len + 1, :]
        for h in range(GLA_HEADS):
            hrows = slice(h * c_len, (h + 1) * c_len)
            cols = slice(h * GLA_DV, (h + 1) * GLA_DV)
            o = _dot(att[hrows], v[:, cols]) + o_state[hrows]
            on = _rmsnorm(o, gn_ref[:, cols])
            gr = gr_ref[rows, cols]
            o_ref[rows, cols] = (on * (gr * jax.nn.sigmoid(gr))).astype(o_ref.dtype)
            in_head = (lane_s >= h * GLA_DK) & (lane_s < (h + 1) * GLA_DK)
            st_new = st_new + jnp.where(in_head, upd[cols], 0.0)
        st_scr[...] = st_new

    @pl.when(t == pl.num_programs(1) - 1)
    def _():
        st_ref[...] = st_scr[...]


def _gla(gq, gk, gv, la, gr, gn, st0, *, tt):
    b, t, kw = gq.shape
    vw = gv.shape[-1]
    tok = lambda w: pl.BlockSpec((None, tt, w), lambda bi, ti: (bi, ti, 0))
    state = pl.BlockSpec((None, GLA_DV, kw), lambda bi, ti: (bi, 0, 0))
    return pl.pallas_call(
        functools.partial(_gla_kernel, n_chunks=tt // GLA_CHUNK),
        grid=(b, t // tt),
        in_specs=[tok(kw), tok(kw), tok(vw), tok(kw), tok(vw), _resident(gn), state],
        out_specs=[tok(vw), state],
        out_shape=[jax.ShapeDtypeStruct((b, t, vw), BF16), jax.ShapeDtypeStruct((b, GLA_DV, kw), F32)],
        scratch_shapes=[pltpu.VMEM((GLA_DV, kw), F32)],
        compiler_params=_params("parallel", "arbitrary"),
        name="gla",
    )(gq, gk, gv, la, gr, gn, st0)


def _mem_kernel(q_ref, mk_ref, mv_ref, o_ref):
    head_major = len(mk_ref.shape) == 3
    for h in range(MEM_HEADS):
        cols = slice(h * MEM_DIM, (h + 1) * MEM_DIM)
        kh = mk_ref[:, h, :] if head_major else mk_ref[:, cols]
        vh = mv_ref[:, h, :] if head_major else mv_ref[:, cols]
        s = _dot_nt(q_ref[:, cols], kh.astype(BF16)) * MEM_DIM ** -0.5
        e = jnp.exp(s - jnp.max(s, axis=-1, keepdims=True))
        p = e * (1.0 / jnp.sum(e, axis=-1, keepdims=True))
        o_ref[:, cols] = _dot(p.astype(BF16), vh.astype(BF16)).astype(o_ref.dtype)


def _mem_attention(q, mk, mv, *, tq):
    b, t, w = q.shape
    tile = pl.BlockSpec((None, tq, w), lambda bi, qi: (bi, qi, 0))
    mem = pl.BlockSpec((None,) + mk.shape[1:], lambda bi, qi: (bi,) + (0,) * (mk.ndim - 1))
    return pl.pallas_call(
        _mem_kernel,
        grid=(b, t // tq),
        in_specs=[tile, mem, mem],
        out_specs=tile,
        out_shape=jax.ShapeDtypeStruct((b, t, w), BF16),
        compiler_params=_params("parallel", "arbitrary"),
        name="mem_attention",
    )(q, mk, mv)


def _merge_kernel(x_ref, osb_ref, ogla_ref, omem_ref, gmix_ref, wgt_ref, wbr_ref, wout_ref,
                  gffn_ref, wr_hi_ref, wr_lo_ref, br_ref, h_ref, ht_ref, idx_ref, wts_ref, cnt_ref, run_ref):
    x = x_ref[...]
    d = x.shape[-1]
    xn = _rmsnorm(x, gmix_ref[...]).astype(BF16)
    mixed = None
    for n, o_ref in enumerate((osb_ref, ogla_ref, omem_ref)):
        gate = jax.nn.sigmoid(_dot(xn, wgt_ref[:, n * d:(n + 1) * d]))
        term = gate * _dot(o_ref[...], wbr_ref[n])
        mixed = term if mixed is None else mixed + term
    h = x + _dot(mixed.astype(BF16), wout_ref[...])
    h_ref[...] = h
    hn = _rmsnorm(h, gffn_ref[...])
    ht_ref[...] = hn

    hn_hi, hn_lo = _split_bf16(hn)
    logits = (_dot(hn_hi, wr_hi_ref[...]) + _dot(hn_lo, wr_hi_ref[...])
              + _dot(hn_hi, wr_lo_ref[...]) + br_ref[...])
    lane = lax.broadcasted_iota(jnp.int32, logits.shape, 1)
    rmax = lambda a: jnp.max(a, axis=-1, keepdims=True)
    rmin = lambda a: jnp.min(a, axis=-1, keepdims=True)
    rsum = lambda a: jnp.sum(a, axis=-1, keepdims=True)

    lc = jnp.where(lane < N_GROUPS, logits, NEG_BIG)
    mc = rmax(lc)
    grp = rmin(jnp.where(lc == mc, lane, LANES))
    p_grp = 1.0 / rsum(jnp.exp(lc - mc))

    lo = N_GROUPS + grp * EXPERTS_PER_GROUP
    in_grp = (lane >= lo) & (lane < lo + EXPERTS_PER_GROUP)
    lf = jnp.where(in_grp, logits, NEG_BIG)
    ef = jnp.exp(lf - rmax(lf))
    pf = jnp.where(in_grp, ef / rsum(ef), -1.0)
    v1 = rmax(pf)
    i1 = rmin(jnp.where(pf == v1, lane, LANES))
    pf2 = jnp.where(lane == i1, -1.0, pf)
    v2 = rmax(pf2)
    i2 = rmin(jnp.where(pf2 == v2, lane, LANES))
    tot = v1 + v2
    e1 = i1 - N_GROUPS
    e2 = i2 - N_GROUPS
    wts_ref[...] = jnp.where(lane == 0, p_grp * (v1 / tot), jnp.where(lane == 1, p_grp * (v2 / tot), 0.0))

    @pl.when(pl.program_id(0) == 0)
    def _():
        run_ref[...] = jnp.zeros_like(run_ref)

    tm = x.shape[0]
    hit1 = lane == e1
    hit2 = lane == e2
    both = (hit1 | hit2).astype(BF16)
    earlier = (lax.broadcasted_iota(jnp.int32, (tm, tm), 1)
               < lax.broadcasted_iota(jnp.int32, (tm, tm), 0)).astype(BF16)
    before = _dot(earlier, both) + run_ref[...]
    r1 = rsum(jnp.where(hit1, before, 0.0)).astype(jnp.int32)
    r2 = rsum(jnp.where(hit2, before, 0.0)).astype(jnp.int32)
    run = before[tm - 1:tm, :] + both[tm - 1:tm, :].astype(F32)
    run_ref[...] = run
    cnt_ref[...] = run
    idx_ref[...] = jnp.where(lane == 0, e1, jnp.where(lane == 1, e2,
                             jnp.where(lane == 2, r1, jnp.where(lane == 3, r2, 0))))


def _merge(x, osb, ogla, omem, gmix, wgt, wbr, wout, gffn, wr_hi, wr_lo, br, *, tm):
    n, d = x.shape
    bw = osb.shape[1]
    row = lambda w: pl.BlockSpec((tm, w), lambda i: (i, 0))
    return pl.pallas_call(
        _merge_kernel,
        grid=(n // tm,),
        in_specs=[row(d), row(bw), row(bw), row(bw), _resident(gmix), _resident(wgt), _resident(wbr),
                  _resident(wout), _resident(gffn), _resident(wr_hi), _resident(wr_lo), _resident(br)],
        out_specs=[row(d), row(d), row(LANES), row(LANES), pl.BlockSpec((1, LANES), lambda i: (0, 0))],
        out_shape=[jax.ShapeDtypeStruct((n, d), F32), jax.ShapeDtypeStruct((n, d), F32),
                   jax.ShapeDtypeStruct((n, LANES), jnp.int32), jax.ShapeDtypeStruct((n, LANES), F32),
                   jax.ShapeDtypeStruct((1, LANES), F32)],
        scratch_shapes=[pltpu.VMEM((1, LANES), F32)],
        compiler_params=_params("arbitrary"),
        name="merge_route",
    )(x, osb, ogla, omem, gmix, wgt, wbr, wout, gffn, wr_hi, wr_lo, br)


def _dispatch_kernel(pend_ref, dest_ref, x_ref, xb_ref, zero_ref, sem, *, tm_e):
    groups = x_ref.shape[0]

    @pl.when(pl.program_id(0) == 0)
    def _():
        zero_ref[...] = jnp.zeros_like(zero_ref)

        def segment_end(e):
            end = pend_ref[e]
            prev = jnp.where(e > 0, pend_ref[jnp.maximum(e - 1, 0)], 0)
            return end, end > prev

        def zero_copy(end):
            first = pl.multiple_of(end - tm_e, tm_e)
            return pltpu.make_async_copy(zero_ref, xb_ref.at[pl.ds(first, tm_e), :], sem)

        def start(e, carry):
            end, used = segment_end(e)

            @pl.when(used)
            def _():
                zero_copy(end).start()
            return carry

        def wait(e, carry):
            end, used = segment_end(e)

            @pl.when(used)
            def _():
                zero_copy(end).wait()
            return carry

        lax.fori_loop(0, N_EXPERTS, start, 0)
        lax.fori_loop(0, N_EXPERTS, wait, 0)

        first_free = pend_ref[N_EXPERTS - 1] // tm_e
        n_blocks = xb_ref.shape[0] // tm_e
        lax.fori_loop(first_free, n_blocks, lambda j, c: (zero_copy((j + 1) * tm_e).start(), c)[1], 0)
        lax.fori_loop(first_free, n_blocks, lambda j, c: (zero_copy((j + 1) * tm_e).wait(), c)[1], 0)

    def issue(g, carry):
        for u in range(SUBLANES):
            for k in range(TOP_K):
                slot = dest_ref[(g * SUBLANES + u) * TOP_K + k]
                pltpu.make_async_copy(x_ref.at[g, pl.ds(u, 1), :], xb_ref.at[pl.ds(slot, 1), :], sem).start()
        return carry

    lax.fori_loop(0, groups, issue, 0)
    tile_rows = xb_ref.at[pl.ds(0, groups * SUBLANES), :]
    for k in range(TOP_K):
        pltpu.make_async_copy(tile_rows, tile_rows, sem).wait()


def _dispatch(x, dest, pend, n_slots, *, tm, tm_e):
    n, d = x.shape
    grid_spec = pltpu.PrefetchScalarGridSpec(
        num_scalar_prefetch=1,
        grid=(n // tm,),
        in_specs=[pl.BlockSpec((tm * TOP_K,), lambda i, pe: (i,), memory_space=pltpu.SMEM),
                  pl.BlockSpec((tm // SUBLANES, SUBLANES, d), lambda i, pe: (i, 0, 0))],
        out_specs=pl.BlockSpec(memory_space=pl.ANY),
        scratch_shapes=[pltpu.VMEM((tm_e, d), F32), pltpu.SemaphoreType.DMA(())],
    )
    return pl.pallas_call(
        functools.partial(_dispatch_kernel, tm_e=tm_e),
        grid_spec=grid_spec,
        out_shape=jax.ShapeDtypeStruct((n_slots, d), F32),
        compiler_params=_params("arbitrary"),
        name="dispatch",
    )(pend, dest, x.reshape(n // SUBLANES, SUBLANES, d))


def _expert_kernel(blk_e_ref, n_used_ref, x_ref, wg_ref, wu_ref, wd_ref, y_ref, wg_bf, wu_bf, wd_bf):
    i = pl.program_id(0)

    @pl.when((i == 0) | (blk_e_ref[i] != blk_e_ref[jnp.maximum(i - 1, 0)]))
    def _():
        wg_bf[...] = wg_ref[...].astype(BF16)
        wu_bf[...] = wu_ref[...].astype(BF16)
        wd_bf[...] = wd_ref[...].astype(BF16)

    @pl.when(i < n_used_ref[0])
    def _():
        x = x_ref[...].astype(BF16)
        g = _dot(x, wg_bf[...])
        u = _dot(x, wu_bf[...])
        a = (g * jax.nn.sigmoid(g) * u).astype(BF16)
        y_ref[...] = _dot(a, wd_bf[...])

    @pl.when(i >= n_used_ref[0])
    def _():
        y_ref[...] = jnp.zeros_like(y_ref)


def _experts(xb, blk_e, n_used, wg, wu, wd, *, tm):
    p, d = xb.shape
    de = wg.shape[-1]
    grid_spec = pltpu.PrefetchScalarGridSpec(
        num_scalar_prefetch=2,
        grid=(p // tm,),
        in_specs=[pl.BlockSpec((tm, d), lambda i, be, nu: (jnp.minimum(i, nu[0] - 1), 0)),
                  pl.BlockSpec((None, d, de), lambda i, be, nu: (be[i], 0, 0)),
                  pl.BlockSpec((None, d, de), lambda i, be, nu: (be[i], 0, 0)),
                  pl.BlockSpec((None, de, d), lambda i, be, nu: (be[i], 0, 0))],
        out_specs=pl.BlockSpec((tm, d), lambda i, be, nu: (i, 0)),
        scratch_shapes=[pltpu.VMEM((d, de), BF16), pltpu.VMEM((d, de), BF16), pltpu.VMEM((de, d), BF16)],
    )
    return pl.pallas_call(
        _expert_kernel,
        grid_spec=grid_spec,
        out_shape=jax.ShapeDtypeStruct((p, d), F32),
        compiler_params=_params("arbitrary"),
        name="experts",
    )(blk_e, n_used, xb, wg, wu, wd)


def _final_kernel(pos_ref, pos_next_ref, h_ref, wts_ref, g_ref, yb_ref, o_ref, rows_ref, sem):
    i = pl.program_id(0)
    tm, d = h_ref.shape
    cur = i % 2

    def gather(slots_ref, buf):
        def issue(g, carry):
            for u in range(SUBLANES):
                for k in range(TOP_K):
                    slot = slots_ref[(g * SUBLANES + u) * TOP_K + k]
                    pltpu.make_async_copy(yb_ref.at[pl.ds(slot, 1), :],
                                          rows_ref.at[buf, k, g, pl.ds(u, 1), :], sem.at[buf]).start()
            return carry

        lax.fori_loop(0, tm // SUBLANES, issue, 0)

    @pl.when(i == 0)
    def _():
        gather(pos_ref, 0)

    @pl.when(i + 1 < pl.num_programs(0))
    def _():
        gather(pos_next_ref, 1 - cur)

    for k in range(TOP_K):
        pltpu.make_async_copy(rows_ref.at[cur, k], rows_ref.at[cur, k], sem.at[cur]).wait()
    w = wts_ref[...]
    y0 = rows_ref[cur, 0].reshape(tm, d)
    y1 = rows_ref[cur, 1].reshape(tm, d)
    o_ref[...] = _rmsnorm(h_ref[...] + (y0 * w[:, 0:1] + y1 * w[:, 1:2]), g_ref[...])


def _final(h, yb, pos, wts, g, *, tm):
    n, d = h.shape
    steps = n // tm
    row = lambda w: pl.BlockSpec((tm, w), lambda i: (i, 0))
    slots = lambda nxt: pl.BlockSpec((tm * TOP_K,), lambda i: (jnp.minimum(i + nxt, steps - 1),),
                                     memory_space=pltpu.SMEM)
    return pl.pallas_call(
        _final_kernel,
        grid=(steps,),
        in_specs=[slots(0), slots(1), row(d), row(LANES), _resident(g), pl.BlockSpec(memory_space=pl.ANY)],
        out_specs=row(d),
        out_shape=jax.ShapeDtypeStruct((n, d), F32),
        scratch_shapes=[pltpu.VMEM((2, TOP_K, tm // SUBLANES, SUBLANES, d), F32),
                        pltpu.SemaphoreType.DMA((2,))],
        compiler_params=_params("arbitrary"),
        name="combine_norm",
    )(pos, pos, h, wts, g, yb)


def _dispatch_plan(route, counts, tm):
    n = route.shape[0]
    a = n * TOP_K
    flat_e = route[:, :TOP_K].reshape(a)
    rank = route[:, TOP_K:2 * TOP_K].reshape(a)
    counts = counts[0, :N_EXPERTS].astype(jnp.int32)
    padded = (counts + tm - 1) // tm * tm
    pend = jnp.cumsum(padded)
    pstart = pend - padded
    experts = jnp.arange(N_EXPERTS, dtype=jnp.int32)
    dest = jnp.sum(jnp.where(flat_e[:, None] == experts[None, :], pstart[None, :], 0), axis=1) + rank
    n_blk = -(-(a + N_EXPERTS * (tm - 1)) // tm)
    blk_start = jnp.arange(n_blk, dtype=jnp.int32) * tm
    blk_e = jnp.minimum(jnp.sum((pend[None, :] <= blk_start[:, None]).astype(jnp.int32), axis=1),
                        N_EXPERTS - 1).astype(jnp.int32)
    n_used = (pend[-1] // tm).astype(jnp.int32).reshape(1)
    return dest.astype(jnp.int32), pend.astype(jnp.int32), blk_e, n_used, n_blk


def _moe(ht, route, counts, wts, h, wg, wu, wd, g_final, *, tm_e, tm):
    dest, pend, blk_e, n_used, n_blk = _dispatch_plan(route, counts, tm_e)
    xb = _dispatch(ht, dest, pend, n_blk * tm_e, tm=tm, tm_e=tm_e)
    yb = _experts(xb, blk_e, n_used, wg, wu, wd, tm=tm_e)
    return _final(h, yb, dest, wts, g_final, tm=tm)


def _layer(x, sb_past, gla_state, mem_kv, wts, *, tm, tq_sb, tt_gla, tq_mem, tm_e):
    b, t, d = x.shape
    n = b * t
    xf = x.reshape(n, d)
    sq, skb, svb, sk, sv, gq, gk, gv, la, gr, mq = _proj(
        xf, wts["g_mix"], wts["w_proj"], wts["wa2"], wts["ba"], wts["w_kvt"] if sb_past is None else None,
        tm=tm, seq_len=t, key_block=tq_sb)
    r3 = lambda a: a.reshape(b, t, a.shape[-1])
    if sb_past is None:
        o_sb = _sb_prompt(r3(sq), skb, svb, tq=tq_sb, pairs=4)
    else:
        o_sb = _sb_sample(r3(sq), r3(skb), r3(svb), sb_past[0], sb_past[1], tk_past=256)
    o_gla, st = _gla(r3(gq), r3(gk), r3(gv), r3(la), r3(gr), wts["g_gla"], gla_state, tt=tt_gla)
    o_mem = _mem_attention(r3(mq), mem_kv[0], mem_kv[1], tq=tq_mem)
    h, ht, route, rw, counts = _merge(xf, o_sb.reshape(n, -1), o_gla.reshape(n, -1), o_mem.reshape(n, -1),
                                      wts["g_mix"], wts["w_gt"], wts["w_br"], wts["w_out"], wts["g_ffn"],
                                      wts["wr_hi"], wts["wr_lo"], wts["b_r"], tm=tm)
    y = _moe(ht, route, counts, rw, h, wts["wg"], wts["wu"], wts["wd"], wts["g_final"], tm_e=tm_e, tm=tm)
    return y.reshape(b, t, d), sk, sv, st


def _state_to_t(s):
    b = s.shape[0]
    return s.transpose(0, 3, 1, 2).reshape(b, GLA_DV, GLA_HEADS * GLA_DK)


def _state_from_t(st):
    b = st.shape[0]
    return st.reshape(b, GLA_DV, GLA_HEADS, GLA_DK).transpose(0, 2, 3, 1)


def kernel(x_prompt, x_sample, mem_prompt, cache_sb_k, cache_sb_v, state_gla, cache_mem_k, cache_mem_v,
           norm_mix, w_in, w_gla_a2, b_gla_a, gla_norm, norm_mem, w_mem_kv, w_branch, w_out,
           norm_ffn, w_coarse, b_coarse, w_fine, b_fine, w_e_gate, w_e_up, w_e_down, norm_final):
    assert w_in.shape[0] == 1, "single-layer model"
    d = x_prompt.shape[-1]
    bp, tp, _ = x_prompt.shape
    bs, ts, _ = x_sample.shape
    bw = d // 2

    w = w_in[0]
    offs = [0]
    for s in (bw, bw, bw, 256, 256, bw, GLA_RANK, bw, bw, N_BRANCH * d):
        offs.append(offs[-1] + s)
    col = lambda i: w[:, offs[i]:offs[i + 1]]
    w_proj = jnp.concatenate(
        [col(0), col(1), col(2), col(3), col(4), col(5), col(7), col(8),
         jnp.pad(col(6), ((0, 0), (0, LANES - GLA_RANK)))], axis=1).astype(BF16)
    assert w_proj.shape[1] == _PROJ_W
    row2 = lambda a: a.reshape(1, -1)
    n_route = N_GROUPS + N_EXPERTS
    w_route = jnp.concatenate([w_coarse[0], w_fine[0].transpose(1, 0, 2).reshape(d, N_EXPERTS)], axis=1)
    w_route = jnp.pad(w_route, ((0, 0), (0, LANES - n_route)))
    wr_hi = w_route.astype(BF16)
    wr_lo = (w_route - wr_hi.astype(F32)).astype(BF16)
    b_route = jnp.pad(jnp.concatenate([b_coarse[0], b_fine[0].reshape(-1)]), (0, LANES - n_route)).reshape(1, LANES)
    wts = dict(
        g_mix=row2(norm_mix[0]), w_proj=w_proj,
        w_kvt=jnp.concatenate([col(1), col(2)], axis=1).T.astype(BF16),
        wa2=jnp.pad(w_gla_a2[0], ((0, LANES - GLA_RANK), (0, 0))).astype(BF16), ba=row2(b_gla_a[0]),
        g_gla=row2(gla_norm[0]),
        w_gt=col(9).astype(BF16), w_br=w_branch[0].astype(BF16), w_out=w_out[0].astype(BF16),
        g_ffn=row2(norm_ffn[0]), wr_hi=wr_hi, wr_lo=wr_lo, b_r=b_route,
        wg=w_e_gate[0], wu=w_e_up[0], wd=w_e_down[0],
        g_final=row2(norm_final),
    )

    m = mem_prompt.shape[1]
    mk, mv = _memkv(mem_prompt.reshape(bp * m, d), row2(norm_mem[0]), w_mem_kv[0].astype(BF16), tm=min(512, bp * m))
    st0 = jnp.zeros((bp, GLA_DV, GLA_HEADS * GLA_DK), F32)
    y_p, sk_p, sv_p, st_p = _layer(x_prompt, None, st0, (mk.reshape(bp, m, bw), mv.reshape(bp, m, bw)), wts,
                                   tm=min(512, bp * tp), tq_sb=min(256, tp), tt_gla=min(256, tp),
                                   tq_mem=min(512, tp), tm_e=512)

    past = (cache_sb_k[0].transpose(0, 2, 3, 1), cache_sb_v[0].transpose(0, 2, 3, 1))
    y_s, sk_s, sv_s, st_s = _layer(x_sample, past, _state_to_t(state_gla[0]),
                                   (cache_mem_k[0], cache_mem_v[0]), wts,
                                   tm=min(512, bs * ts), tq_sb=ts, tt_gla=ts, tq_mem=ts, tm_e=256)

    hd = lambda a, bb, tt: a.reshape(1, bb, tt, SB_HEADS, SB_DIM)
    return (y_p, y_s,
            sk_p.transpose(0, 3, 1, 2)[None], sv_p.transpose(0, 3, 1, 2)[None],
            _state_from_t(st_p)[None],
            mk.reshape(1, bp, m, MEM_HEADS, MEM_DIM), mv.reshape(1, bp, m, MEM_HEADS, MEM_DIM),
            hd(sk_s, bs, ts), hd(sv_s, bs, ts),
            _state_from_t(st_s)[None])
```

```python
import functools

import jax
import jax.numpy as jnp
from jax import lax
from jax.experimental import pallas as pl
from jax.experimental.pallas import tpu as pltpu

F32 = jnp.float32
BF16 = jnp.bfloat16
EPS = 1e-6

SB_HEADS = 8
SB_DIM = 64
GLA_HEADS = 4
GLA_DK = 64
GLA_DV = 128
GLA_RANK = 16
GLA_TAU = 16.0
GLA_CHUNK = 64
MEM_HEADS = 4
MEM_DIM = 128
N_BRANCH = 3
N_GROUPS = 4
EXPERTS_PER_GROUP = 8
N_EXPERTS = N_GROUPS * EXPERTS_PER_GROUP
TOP_K = 2

LANES = 128
SUBLANES = 8
VMEM_LIMIT = 56 * 1024 * 1024
NEG_BIG = -1e30
_LOG_F32_ZERO = -110.0

_NT = (((1,), (1,)), ((), ()))
_TN = (((0,), (0,)), ((), ()))


def _dot(a, b):
    return jnp.dot(a, b, preferred_element_type=F32)


def _dot_nt(a, b):
    return lax.dot_general(a, b, _NT, preferred_element_type=F32)


def _split_bf16(x):
    hi = x.astype(BF16)
    lo = (x - hi.astype(F32)).astype(BF16)
    return hi, lo


def _rmsnorm(x, g):
    return x * lax.rsqrt(jnp.mean(x * x, axis=-1, keepdims=True) + EPS) * g


def _log_sigmoid(x):
    return jnp.minimum(x, 0.0) - jnp.log1p(jnp.exp(-jnp.abs(x)))


def _resident(a):
    return pl.BlockSpec(a.shape, lambda *_: (0,) * a.ndim, pipeline_mode=pl.Buffered(1))


def _params(*sem):
    return pltpu.CompilerParams(dimension_semantics=sem, vmem_limit_bytes=VMEM_LIMIT)


SB_W = SB_HEADS * SB_DIM
GLA_KW = GLA_HEADS * GLA_DK
GLA_VW = GLA_HEADS * GLA_DV
MEM_W = MEM_HEADS * MEM_DIM
_C_SQ = 0
_C_SK = _C_SQ + SB_W
_C_SV = _C_SK + SB_W
_C_GQ = _C_SV + SB_W
_C_GK = _C_GQ + GLA_KW
_C_GV = _C_GK + GLA_KW
_C_GR = _C_GV + GLA_VW
_C_MQ = _C_GR + GLA_VW
_C_GLR = _C_MQ + MEM_W
_PROJ_W = _C_GLR + LANES


def _proj_kernel(*refs, kv_transposed, key_block):
    if kv_transposed:
        x_ref, g_ref, w_ref, wa2_ref, ba_ref, wkvt_ref = refs[:6]
    else:
        x_ref, g_ref, w_ref, wa2_ref, ba_ref = refs[:5]
    sq_ref, skb_ref, svb_ref, sk_ref, sv_ref, gq_ref, gk_ref, gv_ref, la_ref, gr_ref, mq_ref = refs[-11:]
    xn = _rmsnorm(x_ref[...], g_ref[...]).astype(BF16)

    def seg(start, width):
        return _dot(xn, w_ref[:, start:start + width])

    sq_ref[...] = (seg(_C_SQ, SB_W) * SB_DIM ** -0.5).astype(BF16)
    if kv_transposed:
        tm = xn.shape[0]
        for half, (bf_ref, heads_ref) in enumerate(((skb_ref, sk_ref), (svb_ref, sv_ref))):
            kv_t = _dot_nt(wkvt_ref[half * SB_W:(half + 1) * SB_W, :], xn)
            heads_ref[...] = kv_t.reshape(SB_HEADS, SB_DIM, tm)
            for j in range(tm // key_block):
                bf_ref[j] = kv_t[:, j * key_block:(j + 1) * key_block].astype(BF16)
    else:
        for c0, bf_ref, heads_ref in ((_C_SK, skb_ref, sk_ref), (_C_SV, svb_ref, sv_ref)):
            kv = seg(c0, SB_W)
            bf_ref[...] = kv.astype(BF16)
            for h in range(SB_HEADS):
                heads_ref[:, h, :] = kv[:, h * SB_DIM:(h + 1) * SB_DIM]
    gq_ref[...] = seg(_C_GQ, GLA_KW) * GLA_DK ** -0.5
    gk_ref[...] = seg(_C_GK, GLA_KW)
    gv_ref[...] = seg(_C_GV, GLA_VW).astype(BF16)
    gr_ref[...] = seg(_C_GR, GLA_VW)
    mq_ref[...] = seg(_C_MQ, MEM_W).astype(BF16)
    glr = seg(_C_GLR, LANES).astype(BF16)
    la_ref[...] = _log_sigmoid(_dot(glr, wa2_ref[...]) + ba_ref[...]) * (1.0 / GLA_TAU)


def _proj(x, g, w_proj, wa2, ba, w_kvt, *, tm, seq_len, key_block):
    n, d = x.shape
    kv_transposed = w_kvt is not None
    row = lambda w: pl.BlockSpec((tm, w), lambda i: (i, 0))
    flat = lambda w, dt: (row(w), jax.ShapeDtypeStruct((n, w), dt))
    if kv_transposed:
        assert seq_len % tm == 0 and tm % key_block == 0
        nb, per_b = n // seq_len, seq_len // tm
        kv_bf = (pl.BlockSpec((None, tm // key_block, SB_W, key_block), lambda i: (i // per_b, i % per_b, 0, 0)),
                 jax.ShapeDtypeStruct((nb, seq_len // key_block, SB_W, key_block), BF16))
        kv_f32 = (pl.BlockSpec((None, SB_HEADS, SB_DIM, tm), lambda i: (i // per_b, 0, 0, i % per_b)),
                  jax.ShapeDtypeStruct((nb, SB_HEADS, SB_DIM, seq_len), F32))
    else:
        kv_bf = flat(SB_W, BF16)
        kv_f32 = (pl.BlockSpec((tm, SB_HEADS, SB_DIM), lambda i: (i, 0, 0)),
                  jax.ShapeDtypeStruct((n, SB_HEADS, SB_DIM), F32))
    outs = [flat(SB_W, BF16), kv_bf, kv_bf, kv_f32, kv_f32,
            flat(GLA_KW, F32), flat(GLA_KW, F32), flat(GLA_VW, BF16), flat(GLA_KW, F32), flat(GLA_VW, F32),
            flat(MEM_W, BF16)]
    args = [x, g, w_proj, wa2, ba] + ([w_kvt] if kv_transposed else [])
    return pl.pallas_call(
        functools.partial(_proj_kernel, kv_transposed=kv_transposed, key_block=key_block),
        grid=(n // tm,),
        in_specs=[row(d)] + [_resident(a) for a in args[1:]],
        out_specs=[o[0] for o in outs],
        out_shape=[o[1] for o in outs],
        compiler_params=_params("parallel"),
        name="proj",
    )(*args)


def _memkv_kernel(x_ref, g_ref, w_ref, mk_ref, mv_ref):
    xn = _rmsnorm(x_ref[...], g_ref[...]).astype(BF16)
    half = mk_ref.shape[-1]
    mk_ref[...] = _dot(xn, w_ref[:, :half])
    mv_ref[...] = _dot(xn, w_ref[:, half:])


def _memkv(x, g, w, tm):
    n, d = x.shape
    half = w.shape[1] // 2
    row = lambda wd: pl.BlockSpec((tm, wd), lambda i: (i, 0))
    return pl.pallas_call(
        _memkv_kernel,
        grid=(n // tm,),
        in_specs=[row(d), _resident(g), _resident(w)],
        out_specs=[row(half), row(half)],
        out_shape=[jax.ShapeDtypeStruct((n, half), F32)] * 2,
        compiler_params=_params("parallel"),
        name="memkv",
    )(x, g, w)


def _suffix_matrix(tk):
    j = lax.broadcasted_iota(jnp.int32, (2 * tk, tk), 0)
    s = lax.broadcasted_iota(jnp.int32, (2 * tk, tk), 1)
    return ((j > s) & ((j < tk) | (j > s + tk))).astype(BF16)


def _sb_weights(z, carry, mask):
    lz = jnp.minimum(z, 0.0) - jnp.log(1.0 + jnp.exp(-jnp.abs(z)))
    l1 = lz - z
    if mask is not None:
        l1 = jnp.where(mask, l1, 0.0)
    hi, lo = _split_bf16(l1)
    cum = _dot(jnp.concatenate([hi, lo], axis=1), _suffix_matrix(z.shape[1]))
    p = jnp.exp(lz + cum + carry)
    if mask is not None:
        p = jnp.where(mask, p, 0.0)
    return p.astype(BF16), carry + cum[:, :1] + l1[:, :1]


def _fold_descending(n_blocks, alive, fold_block):
    def cond(state):
        i, live = state
        return (i < n_blocks) & live

    def body(state):
        i, _ = state
        return i + 1, fold_block(n_blocks - 1 - i)

    return lax.while_loop(cond, body, (jnp.int32(0), alive))[1]


def _sb_prompt_kernel(q_ref, k_ref, v_ref, o_ref, acc_ref, car_ref, *, tq, pairs):
    qi = pl.program_id(2)
    heads = 2 * pairs
    lane = lax.broadcasted_iota(jnp.int32, (tq, LANES), 1)
    pair_cols = [slice(j * LANES, (j + 1) * LANES) for j in range(pairs)]
    q_heads = []
    for j in range(pairs):
        q = q_ref[:, pair_cols[j]]
        q_heads += [jnp.where(lane < SB_DIM, q, jnp.zeros_like(q)),
                    jnp.where(lane >= SB_DIM, q, jnp.zeros_like(q))]
    acc_ref[...] = jnp.zeros_like(acc_ref)
    car_ref[...] = jnp.zeros_like(car_ref)

    def block(kb, mask):
        z = jnp.concatenate([_dot(q_heads[h], k_ref[kb, pair_cols[h // 2], :]) for h in range(heads)], axis=0)
        p, carry = _sb_weights(z, car_ref[...], mask)
        car_ref[...] = carry
        for h in range(heads):
            acc_ref[h] += _dot_nt(p[h * tq:(h + 1) * tq], v_ref[kb, pair_cols[h // 2], :])
        return jnp.max(carry) > _LOG_F32_ZERO

    r = lax.broadcasted_iota(jnp.int32, (heads * tq, tq), 0)
    c = lax.broadcasted_iota(jnp.int32, (heads * tq, tq), 1)
    alive = block(qi, c < (r & (tq - 1)))
    _fold_descending(qi, alive, lambda kb: block(kb, None))
    for j in range(pairs):
        o_ref[:, pair_cols[j]] = jnp.where(lane < SB_DIM, acc_ref[2 * j], acc_ref[2 * j + 1]).astype(o_ref.dtype)


def _sb_prompt(q, k_t, v_t, *, tq, pairs):
    b, t, w = q.shape
    assert k_t.shape == (b, t // tq, w, tq) and tq & (tq - 1) == 0
    pw = pairs * LANES
    seq = pl.BlockSpec((None, t // tq, pw, tq), lambda bi, p, qi: (bi, 0, p, 0))
    tile = pl.BlockSpec((None, tq, pw), lambda bi, p, qi: (bi, qi, p))
    return pl.pallas_call(
        functools.partial(_sb_prompt_kernel, tq=tq, pairs=pairs),
        grid=(b, w // pw, t // tq),
        in_specs=[tile, seq, seq],
        out_specs=tile,
        out_shape=jax.ShapeDtypeStruct((b, t, w), BF16),
        scratch_shapes=[pltpu.VMEM((2 * pairs, tq, LANES), F32), pltpu.VMEM((2 * pairs * tq, 1), F32)],
        compiler_params=_params("parallel", "parallel", "arbitrary"),
        name="sb_prompt",
    )(q, k_t, v_t)


def _sb_sample_kernel(q_ref, k_ref, v_ref, pk_ref, pv_ref, o_ref, acc_ref, car_ref, *, tk_past):
    t = q_ref.shape[0]
    past_len = pk_ref.shape[-1]
    acc_ref[...] = jnp.zeros_like(acc_ref)
    car_ref[...] = jnp.zeros_like(car_ref)
    head_cols = [slice(h * SB_DIM, (h + 1) * SB_DIM) for h in range(SB_HEADS)]

    def fold_all_heads(logits, weighted_values, mask):
        z = jnp.concatenate([logits(h, q_ref[:, head_cols[h]]) for h in range(SB_HEADS)], axis=0)
        p, carry = _sb_weights(z, car_ref[...], mask)
        car_ref[...] = carry
        for h in range(SB_HEADS):
            acc_ref[h] += weighted_values(h, p[h * t:(h + 1) * t])
        return jnp.max(carry) > _LOG_F32_ZERO

    r = lax.broadcasted_iota(jnp.int32, (SB_HEADS * t, t), 0)
    c = lax.broadcasted_iota(jnp.int32, (SB_HEADS * t, t), 1)
    alive = fold_all_heads(lambda h, qh: _dot_nt(qh, k_ref[:, head_cols[h]]),
                           lambda h, ph: _dot(ph, v_ref[:, head_cols[h]]),
                           c < (r & (t - 1)))

    def past_block(kb_idx):
        pos = pl.ds(pl.multiple_of(kb_idx * tk_past, tk_past), tk_past)
        return fold_all_heads(lambda h, qh: _dot(qh, pk_ref[h, :, pos].astype(BF16)),
                              lambda h, ph: _dot_nt(ph, pv_ref[h, :, pos].astype(BF16)), None)

    _fold_descending(past_len // tk_past, alive, past_block)
    for h in range(SB_HEADS):
        o_ref[:, head_cols[h]] = acc_ref[h].astype(o_ref.dtype)


def _sb_sample(q, k, v, past_k, past_v, *, tk_past):
    b, t, w = q.shape
    assert t & (t - 1) == 0, "query chunk length must be a power of two"
    cur = pl.BlockSpec((None, t, w), lambda bi: (bi, 0, 0))
    past = pl.BlockSpec((None,) + past_k.shape[1:], lambda bi: (bi, 0, 0, 0))
    return pl.pallas_call(
        functools.partial(_sb_sample_kernel, tk_past=tk_past),
        grid=(b,),
        in_specs=[cur, cur, cur, past, past],
        out_specs=cur,
        out_shape=jax.ShapeDtypeStruct((b, t, w), BF16),
        scratch_shapes=[pltpu.VMEM((SB_HEADS, t, SB_DIM), F32), pltpu.VMEM((SB_HEADS * t, 1), F32)],
        compiler_params=_params("parallel"),
        name="sb_sample",
    )(q, k, v, past_k, past_v)


def _gla_kernel(gq_ref, gk_ref, gv_ref, la_ref, gr_ref, gn_ref, s0_ref, o_ref, st_ref, st_scr, *, n_chunks):
    c_len = GLA_CHUNK
    tt = n_chunks * c_len
    kw = GLA_HEADS * GLA_DK
    t = pl.program_id(1)

    @pl.when(t == 0)
    def _():
        st_scr[...] = s0_ref[...]

    r = lax.broadcasted_iota(jnp.int32, (tt, tt), 0)
    c = lax.broadcasted_iota(jnp.int32, (tt, tt), 1)
    chunk_bits = c_len.bit_length() - 1
    same_chunk = (r >> chunk_bits) == (c >> chunk_bits)
    tril = (same_chunk & (r >= c)).astype(BF16)
    ones = same_chunk.astype(BF16)
    la_hi, la_lo = _split_bf16(la_ref[...])
    b = _dot(tril, la_hi) + _dot(tril, la_lo)
    b_tot = _dot(ones, la_hi) + _dot(ones, la_lo)
    gk = gk_ref[...]
    qd = gq_ref[...] * jnp.exp(b)
    kd = (gk * jnp.exp(-b)).astype(BF16)
    ke = (gk * jnp.exp(b_tot - b)).astype(BF16)
    decay = jnp.exp(b_tot)

    rc = lax.broadcasted_iota(jnp.int32, (GLA_HEADS * c_len, c_len), 0)
    cc = lax.broadcasted_iota(jnp.int32, (GLA_HEADS * c_len, c_len), 1)
    causal = (rc & (c_len - 1)) >= cc
    lane_q = lax.broadcasted_iota(jnp.int32, (c_len, kw), 1)
    lane_s = lax.broadcasted_iota(jnp.int32, (GLA_DV, kw), 1)

    for ci in range(n_chunks):
        rows = slice(ci * c_len, (ci + 1) * c_len)
        qd_c = qd[rows]
        qs = jnp.concatenate(
            [jnp.where((lane_q >= h * GLA_DK) & (lane_q < (h + 1) * GLA_DK), qd_c, 0.0)
             for h in range(GLA_HEADS)], axis=0).astype(BF16)
        att = jnp.where(causal, _dot_nt(qs, kd[rows]), 0.0).astype(BF16)
        st = st_scr[...]
        o_state = _dot_nt(qs, st.astype(BF16))
        v = gv_ref[rows, :]
        upd = lax.dot_general(v, ke[rows], _TN, preferred_element_type=F32)
        st_new = st * decay[ci * c_len:ci * c_len + 1, :]
        for h in range(GLA_HEADS):
            hrows = slice(h * c_len, (h + 1) * c_len)
            cols = slice(h * GLA_DV, (h + 1) * GLA_DV)
            o = _dot(att[hrows], v[:, cols]) + o_state[hrows]
            on = _rmsnorm(o, gn_ref[:, cols])
            gr = gr_ref[rows, cols]
            o_ref[rows, cols] = (on * (gr * jax.nn.sigmoid(gr))).astype(o_ref.dtype)
            in_head = (lane_s >= h * GLA_DK) & (lane_s < (h + 1) * GLA_DK)
            st_new = st_new + jnp.where(in_head, upd[cols], 0.0)
        st_scr[...] = st_new

    @pl.when(t == pl.num_programs(1) - 1)
    def _():
        st_ref[...] = st_scr[...]


def _gla(gq, gk, gv, la, gr, gn, st0, *, tt):
    b, t, kw = gq.shape
    vw = gv.shape[-1]
    tok = lambda w: pl.BlockSpec((None, tt, w), lambda bi, ti: (bi, ti, 0))
    state = pl.BlockSpec((None, GLA_DV, kw), lambda bi, ti: (bi, 0, 0))
    return pl.pallas_call(
        functools.partial(_gla_kernel, n_chunks=tt // GLA_CHUNK),
        grid=(b, t // tt),
        in_specs=[tok(kw), tok(kw), tok(vw), tok(kw), tok(vw), _resident(gn), state],
        out_specs=[tok(vw), state],
        out_shape=[jax.ShapeDtypeStruct((b, t, vw), BF16), jax.ShapeDtypeStruct((b, GLA_DV, kw), F32)],
        scratch_shapes=[pltpu.VMEM((GLA_DV, kw), F32)],
        compiler_params=_params("parallel", "arbitrary"),
        name="gla",
    )(gq, gk, gv, la, gr, gn, st0)


def _mem_kernel(q_ref, mk_ref, mv_ref, o_ref):
    head_major = len(mk_ref.shape) == 3
    for h in range(MEM_HEADS):
        cols = slice(h * MEM_DIM, (h + 1) * MEM_DIM)
        kh = mk_ref[:, h, :] if head_major else mk_ref[:, cols]
        vh = mv_ref[:, h, :] if head_major else mv_ref[:, cols]
        s = _dot_nt(q_ref[:, cols], kh.astype(BF16)) * MEM_DIM ** -0.5
        e = jnp.exp(s - jnp.max(s, axis=-1, keepdims=True))
        p = e * (1.0 / jnp.sum(e, axis=-1, keepdims=True))
        o_ref[:, cols] = _dot(p.astype(BF16), vh.astype(BF16)).astype(o_ref.dtype)


def _mem_attention(q, mk, mv, *, tq):
    b, t, w = q.shape
    tile = pl.BlockSpec((None, tq, w), lambda bi, qi: (bi, qi, 0))
    mem = pl.BlockSpec((None,) + mk.shape[1:], lambda bi, qi: (bi,) + (0,) * (mk.ndim - 1))
    return pl.pallas_call(
        _mem_kernel,
        grid=(b, t // tq),
        in_specs=[tile, mem, mem],
        out_specs=tile,
        out_shape=jax.ShapeDtypeStruct((b, t, w), BF16),
        compiler_params=_params("parallel", "arbitrary"),
        name="mem_attention",
    )(q, mk, mv)


def _merge_kernel(x_ref, osb_ref, ogla_ref, omem_ref, gmix_ref, wgt_ref, wbr_ref, wout_ref,
                  gffn_ref, wr_hi_ref, wr_lo_ref, br_ref, h_ref, ht_ref, idx_ref, wts_ref, cnt_ref, run_ref):
    x = x_ref[...]
    d = x.shape[-1]
    xn = _rmsnorm(x, gmix_ref[...]).astype(BF16)
    mixed = None
    for n, o_ref in enumerate((osb_ref, ogla_ref, omem_ref)):
        gate = jax.nn.sigmoid(_dot(xn, wgt_ref[:, n * d:(n + 1) * d]))
        term = gate * _dot(o_ref[...], wbr_ref[n])
        mixed = term if mixed is None else mixed + term
    h = x + _dot(mixed.astype(BF16), wout_ref[...])
    h_ref[...] = h
    hn = _rmsnorm(h, gffn_ref[...])
    ht_ref[...] = hn

    hn_hi, hn_lo = _split_bf16(hn)
    logits = (_dot(hn_hi, wr_hi_ref[...]) + _dot(hn_lo, wr_hi_ref[...])
              + _dot(hn_hi, wr_lo_ref[...]) + br_ref[...])
    lane = lax.broadcasted_iota(jnp.int32, logits.shape, 1)
    rmax = lambda a: jnp.max(a, axis=-1, keepdims=True)
    rmin = lambda a: jnp.min(a, axis=-1, keepdims=True)
    rsum = lambda a: jnp.sum(a, axis=-1, keepdims=True)

    lc = jnp.where(lane < N_GROUPS, logits, NEG_BIG)
    mc = rmax(lc)
    grp = rmin(jnp.where(lc == mc, lane, LANES))
    p_grp = 1.0 / rsum(jnp.exp(lc - mc))

    lo = N_GROUPS + grp * EXPERTS_PER_GROUP
    in_grp = (lane >= lo) & (lane < lo + EXPERTS_PER_GROUP)
    lf = jnp.where(in_grp, logits, NEG_BIG)
    ef = jnp.exp(lf - rmax(lf))
    pf = jnp.where(in_grp, ef / rsum(ef), -1.0)
    v1 = rmax(pf)
    i1 = rmin(jnp.where(pf == v1, lane, LANES))
    pf2 = jnp.where(lane == i1, -1.0, pf)
    v2 = rmax(pf2)
    i2 = rmin(jnp.where(pf2 == v2, lane, LANES))
    tot = v1 + v2
    e1 = i1 - N_GROUPS
    e2 = i2 - N_GROUPS
    wts_ref[...] = jnp.where(lane == 0, p_grp * (v1 / tot), jnp.where(lane == 1, p_grp * (v2 / tot), 0.0))

    @pl.when(pl.program_id(0) == 0)
    def _():
        run_ref[...] = jnp.zeros_like(run_ref)

    tm = x.shape[0]
    hit1 = lane == e1
    hit2 = lane == e2
    both = (hit1 | hit2).astype(BF16)
    earlier = (lax.broadcasted_iota(jnp.int32, (tm, tm), 1)
               < lax.broadcasted_iota(jnp.int32, (tm, tm), 0)).astype(BF16)
    before = _dot(earlier, both) + run_ref[...]
    r1 = rsum(jnp.where(hit1, before, 0.0)).astype(jnp.int32)
    r2 = rsum(jnp.where(hit2, before, 0.0)).astype(jnp.int32)
    run = before[tm - 1:tm, :] + both[tm - 1:tm, :].astype(F32)
    run_ref[...] = run
    cnt_ref[...] = run
    idx_ref[...] = jnp.where(lane == 0, e1, jnp.where(lane == 1, e2,
                             jnp.where(lane == 2, r1, jnp.where(lane == 3, r2, 0))))


def _merge(x, osb, ogla, omem, gmix, wgt, wbr, wout, gffn, wr_hi, wr_lo, br, *, tm):
    n, d = x.shape
    bw = osb.shape[1]
    row = lambda w: pl.BlockSpec((tm, w), lambda i: (i, 0))
    return pl.pallas_call(
        _merge_kernel,
        grid=(n // tm,),
        in_specs=[row(d), row(bw), row(bw), row(bw), _resident(gmix), _resident(wgt), _resident(wbr),
                  _resident(wout), _resident(gffn), _resident(wr_hi), _resident(wr_lo), _resident(br)],
        out_specs=[row(d), row(d), row(LANES), row(LANES), pl.BlockSpec((1, LANES), lambda i: (0, 0))],
        out_shape=[jax.ShapeDtypeStruct((n, d), F32), jax.ShapeDtypeStruct((n, d), F32),
                   jax.ShapeDtypeStruct((n, LANES), jnp.int32), jax.ShapeDtypeStruct((n, LANES), F32),
                   jax.ShapeDtypeStruct((1, LANES), F32)],
        scratch_shapes=[pltpu.VMEM((1, LANES), F32)],
        compiler_params=_params("arbitrary"),
        name="merge_route",
    )(x, osb, ogla, omem, gmix, wgt, wbr, wout, gffn, wr_hi, wr_lo, br)


def _dispatch_kernel(pend_ref, dest_ref, x_ref, xb_ref, zero_ref, sem, *, tm_e):
    groups = x_ref.shape[0]

    @pl.when(pl.program_id(0) == 0)
    def _():
        zero_ref[...] = jnp.zeros_like(zero_ref)

        def segment_end(e):
            end = pend_ref[e]
            prev = jnp.where(e > 0, pend_ref[jnp.maximum(e - 1, 0)], 0)
            return end, end > prev

        def zero_copy(end):
            first = pl.multiple_of(end - tm_e, tm_e)
            return pltpu.make_async_copy(zero_ref, xb_ref.at[pl.ds(first, tm_e), :], sem)

        def start(e, carry):
            end, used = segment_end(e)

            @pl.when(used)
            def _():
                zero_copy(end).start()
            return carry

        def wait(e, carry):
            end, used = segment_end(e)

            @pl.when(used)
            def _():
                zero_copy(end).wait()
            return carry

        lax.fori_loop(0, N_EXPERTS, start, 0)
        lax.fori_loop(0, N_EXPERTS, wait, 0)

        first_free = pend_ref[N_EXPERTS - 1] // tm_e
        n_blocks = xb_ref.shape[0] // tm_e
        lax.fori_loop(first_free, n_blocks, lambda j, c: (zero_copy((j + 1) * tm_e).start(), c)[1], 0)
        lax.fori_loop(first_free, n_blocks, lambda j, c: (zero_copy((j + 1) * tm_e).wait(), c)[1], 0)

    def issue(g, carry):
        for u in range(SUBLANES):
            for k in range(TOP_K):
                slot = dest_ref[(g * SUBLANES + u) * TOP_K + k]
                pltpu.make_async_copy(x_ref.at[g, pl.ds(u, 1), :], xb_ref.at[pl.ds(slot, 1), :], sem).start()
        return carry

    lax.fori_loop(0, groups, issue, 0)
    tile_rows = xb_ref.at[pl.ds(0, groups * SUBLANES), :]
    for k in range(TOP_K):
        pltpu.make_async_copy(tile_rows, tile_rows, sem).wait()


def _dispatch(x, dest, pend, n_slots, *, tm, tm_e):
    n, d = x.shape
    grid_spec = pltpu.PrefetchScalarGridSpec(
        num_scalar_prefetch=1,
        grid=(n // tm,),
        in_specs=[pl.BlockSpec((tm * TOP_K,), lambda i, pe: (i,), memory_space=pltpu.SMEM),
                  pl.BlockSpec((tm // SUBLANES, SUBLANES, d), lambda i, pe: (i, 0, 0))],
        out_specs=pl.BlockSpec(memory_space=pl.ANY),
        scratch_shapes=[pltpu.VMEM((tm_e, d), F32), pltpu.SemaphoreType.DMA(())],
    )
    return pl.pallas_call(
        functools.partial(_dispatch_kernel, tm_e=tm_e),
        grid_spec=grid_spec,
        out_shape=jax.ShapeDtypeStruct((n_slots, d), F32),
        compiler_params=_params("arbitrary"),
        name="dispatch",
    )(pend, dest, x.reshape(n // SUBLANES, SUBLANES, d))


def _expert_kernel(blk_e_ref, n_used_ref, x_ref, wg_ref, wu_ref, wd_ref, y_ref, wg_bf, wu_bf, wd_bf):
    i = pl.program_id(0)

    @pl.when((i == 0) | (blk_e_ref[i] != blk_e_ref[jnp.maximum(i - 1, 0)]))
    def _():
        wg_bf[...] = wg_ref[...].astype(BF16)
        wu_bf[...] = wu_ref[...].astype(BF16)
        wd_bf[...] = wd_ref[...].astype(BF16)

    @pl.when(i < n_used_ref[0])
    def _():
        x = x_ref[...].astype(BF16)
        g = _dot(x, wg_bf[...])
        u = _dot(x, wu_bf[...])
        a = (g * jax.nn.sigmoid(g) * u).astype(BF16)
        y_ref[...] = _dot(a, wd_bf[...])

    @pl.when(i >= n_used_ref[0])
    def _():
        y_ref[...] = jnp.zeros_like(y_ref)


def _experts(xb, blk_e, n_used, wg, wu, wd, *, tm):
    p, d = xb.shape
    de = wg.shape[-1]
    grid_spec = pltpu.PrefetchScalarGridSpec(
        num_scalar_prefetch=2,
        grid=(p // tm,),
        in_specs=[pl.BlockSpec((tm, d), lambda i, be, nu: (jnp.minimum(i, nu[0] - 1), 0)),
                  pl.BlockSpec((None, d, de), lambda i, be, nu: (be[i], 0, 0)),
                  pl.BlockSpec((None, d, de), lambda i, be, nu: (be[i], 0, 0)),
                  pl.BlockSpec((None, de, d), lambda i, be, nu: (be[i], 0, 0))],
        out_specs=pl.BlockSpec((tm, d), lambda i, be, nu: (i, 0)),
        scratch_shapes=[pltpu.VMEM((d, de), BF16), pltpu.VMEM((d, de), BF16), pltpu.VMEM((de, d), BF16)],
    )
    return pl.pallas_call(
        _expert_kernel,
        grid_spec=grid_spec,
        out_shape=jax.ShapeDtypeStruct((p, d), F32),
        compiler_params=_params("arbitrary"),
        name="experts",
    )(blk_e, n_used, xb, wg, wu, wd)


def _final_kernel(pos_ref, pos_next_ref, h_ref, wts_ref, g_ref, yb_ref, o_ref, rows_ref, sem):
    i = pl.program_id(0)
    tm, d = h_ref.shape
    cur = i % 2

    def gather(slots_ref, buf):
        def issue(g, carry):
            for u in range(SUBLANES):
                for k in range(TOP_K):
                    slot = slots_ref[(g * SUBLANES + u) * TOP_K + k]
                    pltpu.make_async_copy(yb_ref.at[pl.ds(slot, 1), :],
                                          rows_ref.at[buf, k, g, pl.ds(u, 1), :], sem.at[buf]).start()
            return carry

        lax.fori_loop(0, tm // SUBLANES, issue, 0)

    @pl.when(i == 0)
    def _():
        gather(pos_ref, 0)

    @pl.when(i + 1 < pl.num_programs(0))
    def _():
        gather(pos_next_ref, 1 - cur)

    for k in range(TOP_K):
        pltpu.make_async_copy(rows_ref.at[cur, k], rows_ref.at[cur, k], sem.at[cur]).wait()
    w = wts_ref[...]
    y0 = rows_ref[cur, 0].reshape(tm, d)
    y1 = rows_ref[cur, 1].reshape(tm, d)
    o_ref[...] = _rmsnorm(h_ref[...] + (y0 * w[:, 0:1] + y1 * w[:, 1:2]), g_ref[...])


def _final(h, yb, pos, wts, g, *, tm):
    n, d = h.shape
    steps = n // tm
    row = lambda w: pl.BlockSpec((tm, w), lambda i: (i, 0))
    slots = lambda nxt: pl.BlockSpec((tm * TOP_K,), lambda i: (jnp.minimum(i + nxt, steps - 1),),
                                     memory_space=pltpu.SMEM)
    return pl.pallas_call(
        _final_kernel,
        grid=(steps,),
        in_specs=[slots(0), slots(1), row(d), row(LANES), _resident(g), pl.BlockSpec(memory_space=pl.ANY)],
        out_specs=row(d),
        out_shape=jax.ShapeDtypeStruct((n, d), F32),
        scratch_shapes=[pltpu.VMEM((2, TOP_K, tm // SUBLANES, SUBLANES, d), F32),
                        pltpu.SemaphoreType.DMA((2,))],
        compiler_params=_params("arbitrary"),
        name="combine_norm",
    )(pos, pos, h, wts, g, yb)


def _dispatch_plan(route, counts, tm):
    n = route.shape[0]
    a = n * TOP_K
    flat_e = route[:, :TOP_K].reshape(a)
    rank = route[:, TOP_K:2 * TOP_K].reshape(a)
    counts = counts[0, :N_EXPERTS].astype(jnp.int32)
    padded = (counts + tm - 1) // tm * tm
    pend = jnp.cumsum(padded)
    pstart = pend - padded
    experts = jnp.arange(N_EXPERTS, dtype=jnp.int32)
    dest = jnp.sum(jnp.where(flat_e[:, None] == experts[None, :], pstart[None, :], 0), axis=1) + rank
    n_blk = -(-(a + N_EXPERTS * (tm - 1)) // tm)
    blk_start = jnp.arange(n_blk, dtype=jnp.int32) * tm
    blk_e = jnp.minimum(jnp.sum((pend[None, :] <= blk_start[:, None]).astype(jnp.int32), axis=1),
                        N_EXPERTS - 1).astype(jnp.int32)
    n_used = (pend[-1] // tm).astype(jnp.int32).reshape(1)
    return dest.astype(jnp.int32), pend.astype(jnp.int32), blk_e, n_used, n_blk


def _moe(ht, route, counts, wts, h, wg, wu, wd, g_final, *, tm_e, tm):
    dest, pend, blk_e, n_used, n_blk = _dispatch_plan(route, counts, tm_e)
    tm_dispatch = 2 * tm if ht.shape[0] % (2 * tm) == 0 else tm
    xb = _dispatch(ht, dest, pend, n_blk * tm_e, tm=tm_dispatch, tm_e=tm_e)
    yb = _experts(xb, blk_e, n_used, wg, wu, wd, tm=tm_e)
    return _final(h, yb, dest, wts, g_final, tm=tm)


def _layer(x, sb_past, gla_state, mem_kv, wts, *, tm, tq_sb, tt_gla, tq_mem, tm_e):
    b, t, d = x.shape
    n = b * t
    xf = x.reshape(n, d)
    sq, skb, svb, sk, sv, gq, gk, gv, la, gr, mq = _proj(
        xf, wts["g_mix"], wts["w_proj"], wts["wa2"], wts["ba"], wts["w_kvt"] if sb_past is None else None,
        tm=tm, seq_len=t, key_block=tq_sb)
    r3 = lambda a: a.reshape(b, t, a.shape[-1])
    if sb_past is None:
        o_sb = _sb_prompt(r3(sq), skb, svb, tq=tq_sb, pairs=4)
    else:
        o_sb = _sb_sample(r3(sq), r3(skb), r3(svb), sb_past[0], sb_past[1], tk_past=SB_TILE)
    o_gla, st = _gla(r3(gq), r3(gk), r3(gv), r3(la), r3(gr), wts["g_gla"], gla_state, tt=tt_gla)
    o_mem = _mem_attention(r3(mq), mem_kv[0], mem_kv[1], tq=tq_mem)
    h, ht, route, rw, counts = _merge(xf, o_sb.reshape(n, -1), o_gla.reshape(n, -1), o_mem.reshape(n, -1),
                                      wts["g_mix"], wts["w_gt"], wts["w_br"], wts["w_out"], wts["g_ffn"],
                                      wts["wr_hi"], wts["wr_lo"], wts["b_r"], tm=tm)
    y = _moe(ht, route, counts, rw, h, wts["wg"], wts["wu"], wts["wd"], wts["g_final"], tm_e=tm_e, tm=tm)
    return y.reshape(b, t, d), sk, sv, st


ROW_TILE = 512
SB_TILE = 256
GLA_TILE = 256


def _tiles(b, t, *, chunked):
    if chunked:
        return dict(tm=min(ROW_TILE, b * t), tq_sb=t, tt_gla=t, tq_mem=t, tm_e=ROW_TILE // 2)
    return dict(tm=min(ROW_TILE, b * t), tq_sb=min(SB_TILE, t), tt_gla=min(GLA_TILE, t),
                tq_mem=min(ROW_TILE, t), tm_e=ROW_TILE)


def _state_to_t(s):
    b = s.shape[0]
    return s.transpose(0, 3, 1, 2).reshape(b, GLA_DV, GLA_HEADS * GLA_DK)


def _state_from_t(st):
    b = st.shape[0]
    return st.reshape(b, GLA_DV, GLA_HEADS, GLA_DK).transpose(0, 2, 3, 1)


def kernel(x_prompt, x_sample, mem_prompt, cache_sb_k, cache_sb_v, state_gla, cache_mem_k, cache_mem_v,
           norm_mix, w_in, w_gla_a2, b_gla_a, gla_norm, norm_mem, w_mem_kv, w_branch, w_out,
           norm_ffn, w_coarse, b_coarse, w_fine, b_fine, w_e_gate, w_e_up, w_e_down, norm_final):
    assert w_in.shape[0] == 1, "single-layer model"
    d = x_prompt.shape[-1]
    bp, tp, _ = x_prompt.shape
    bs, ts, _ = x_sample.shape
    bw = d // 2
    assert bw == SB_W == GLA_VW == MEM_W

    w = w_in[0]
    offs = [0]
    for s in (SB_W, SB_W, SB_W, GLA_KW, GLA_KW, GLA_VW, GLA_RANK, GLA_VW, MEM_W, N_BRANCH * d):
        offs.append(offs[-1] + s)
    col = lambda i: w[:, offs[i]:offs[i + 1]]
    w_proj = jnp.concatenate(
        [col(0), col(1), col(2), col(3), col(4), col(5), col(7), col(8),
         jnp.pad(col(6), ((0, 0), (0, LANES - GLA_RANK)))], axis=1).astype(BF16)
    assert w_proj.shape[1] == _PROJ_W
    row2 = lambda a: a.reshape(1, -1)
    n_route = N_GROUPS + N_EXPERTS
    w_route = jnp.concatenate([w_coarse[0], w_fine[0].transpose(1, 0, 2).reshape(d, N_EXPERTS)], axis=1)
    w_route = jnp.pad(w_route, ((0, 0), (0, LANES - n_route)))
    wr_hi = w_route.astype(BF16)
    wr_lo = (w_route - wr_hi.astype(F32)).astype(BF16)
    b_route = jnp.pad(jnp.concatenate([b_coarse[0], b_fine[0].reshape(-1)]), (0, LANES - n_route)).reshape(1, LANES)
    wts = dict(
        g_mix=row2(norm_mix[0]), w_proj=w_proj,
        w_kvt=jnp.concatenate([col(1), col(2)], axis=1).T.astype(BF16),
        wa2=jnp.pad(w_gla_a2[0], ((0, LANES - GLA_RANK), (0, 0))).astype(BF16), ba=row2(b_gla_a[0]),
        g_gla=row2(gla_norm[0]),
        w_gt=col(9).astype(BF16), w_br=w_branch[0].astype(BF16), w_out=w_out[0].astype(BF16),
        g_ffn=row2(norm_ffn[0]), wr_hi=wr_hi, wr_lo=wr_lo, b_r=b_route,
        wg=w_e_gate[0], wu=w_e_up[0], wd=w_e_down[0],
        g_final=row2(norm_final),
    )

    m = mem_prompt.shape[1]
    mk, mv = _memkv(mem_prompt.reshape(bp * m, d), row2(norm_mem[0]), w_mem_kv[0].astype(BF16),
                    tm=min(ROW_TILE, bp * m))
    st0 = jnp.zeros((bp, GLA_DV, GLA_HEADS * GLA_DK), F32)
    y_p, sk_p, sv_p, st_p = _layer(x_prompt, None, st0, (mk.reshape(bp, m, bw), mv.reshape(bp, m, bw)), wts,
                                   **_tiles(bp, tp, chunked=False))

    past = (cache_sb_k[0].transpose(0, 2, 3, 1), cache_sb_v[0].transpose(0, 2, 3, 1))
    y_s, sk_s, sv_s, st_s = _layer(x_sample, past, _state_to_t(state_gla[0]),
                                   (cache_mem_k[0], cache_mem_v[0]), wts, **_tiles(bs, ts, chunked=True))

    hd = lambda a, bb, tt: a.reshape(1, bb, tt, SB_HEADS, SB_DIM)
    return (y_p, y_s,
            sk_p.transpose(0, 3, 1, 2)[None], sv_p.transpose(0, 3, 1, 2)[None],
            _state_from_t(st_p)[None],
            mk.reshape(1, bp, m, MEM_HEADS, MEM_DIM), mv.reshape(1, bp, m, MEM_HEADS, MEM_DIM),
            hd(sk_s, bs, ts), hd(sv_s, bs, ts),
            _state_from_t(st_s)[None])
```

```python
import functools

import jax
import jax.numpy as jnp
from jax import lax
from jax.experimental import pallas as pl
from jax.experimental.pallas import tpu as pltpu

F32 = jnp.float32
BF16 = jnp.bfloat16
EPS = 1e-6

SB_HEADS = 8
SB_DIM = 64
GLA_HEADS = 4
GLA_DK = 64
GLA_DV = 128
GLA_RANK = 16
GLA_TAU = 16.0
GLA_CHUNK = 64
MEM_HEADS = 4
MEM_DIM = 128
N_BRANCH = 3
N_GROUPS = 4
EXPERTS_PER_GROUP = 8
N_EXPERTS = N_GROUPS * EXPERTS_PER_GROUP
TOP_K = 2

LANES = 128
SUBLANES = 8
VMEM_LIMIT = 56 * 1024 * 1024
NEG_BIG = -1e30
_LOG_F32_ZERO = -110.0

_NT = (((1,), (1,)), ((), ()))
_TN = (((0,), (0,)), ((), ()))


def _dot(a, b):
    return jnp.dot(a, b, preferred_element_type=F32)


def _dot_nt(a, b):
    return lax.dot_general(a, b, _NT, preferred_element_type=F32)


def _split_bf16(x):
    hi = x.astype(BF16)
    lo = (x - hi.astype(F32)).astype(BF16)
    return hi, lo


def _rmsnorm(x, g):
    return x * lax.rsqrt(jnp.mean(x * x, axis=-1, keepdims=True) + EPS) * g


def _log_sigmoid(x):
    return jnp.minimum(x, 0.0) - jnp.log1p(jnp.exp(-jnp.abs(x)))


def _resident(a):
    return pl.BlockSpec(a.shape, lambda *_: (0,) * a.ndim, pipeline_mode=pl.Buffered(1))


def _params(*sem):
    return pltpu.CompilerParams(dimension_semantics=sem, vmem_limit_bytes=VMEM_LIMIT)


SB_W = SB_HEADS * SB_DIM
GLA_KW = GLA_HEADS * GLA_DK
GLA_VW = GLA_HEADS * GLA_DV
MEM_W = MEM_HEADS * MEM_DIM
_C_SQ = 0
_C_SK = _C_SQ + SB_W
_C_SV = _C_SK + SB_W
_C_GQ = _C_SV + SB_W
_C_GK = _C_GQ + GLA_KW
_C_GV = _C_GK + GLA_KW
_C_GR = _C_GV + GLA_VW
_C_MQ = _C_GR + GLA_VW
_C_GLR = _C_MQ + MEM_W
_PROJ_W = _C_GLR + LANES


def _proj_kernel(*refs, kv_transposed, key_block):
    if kv_transposed:
        x_ref, g_ref, w_ref, wa2_ref, ba_ref, wkvt_ref = refs[:6]
    else:
        x_ref, g_ref, w_ref, wa2_ref, ba_ref = refs[:5]
    sq_ref, skb_ref, svb_ref, sk_ref, sv_ref, gq_ref, gk_ref, gv_ref, la_ref, gr_ref, mq_ref = refs[-11:]
    xn = _rmsnorm(x_ref[...], g_ref[...]).astype(BF16)

    def seg(start, width):
        return _dot(xn, w_ref[:, start:start + width])

    sq_ref[...] = (seg(_C_SQ, SB_W) * SB_DIM ** -0.5).astype(BF16)
    if kv_transposed:
        tm = xn.shape[0]
        for half, (bf_ref, heads_ref) in enumerate(((skb_ref, sk_ref), (svb_ref, sv_ref))):
            kv_t = _dot_nt(wkvt_ref[half * SB_W:(half + 1) * SB_W, :], xn)
            heads_ref[...] = kv_t.reshape(SB_HEADS, SB_DIM, tm)
            for j in range(tm // key_block):
                bf_ref[j] = kv_t[:, j * key_block:(j + 1) * key_block].astype(BF16)
    else:
        for c0, bf_ref, heads_ref in ((_C_SK, skb_ref, sk_ref), (_C_SV, svb_ref, sv_ref)):
            kv = seg(c0, SB_W)
            bf_ref[...] = kv.astype(BF16)
            for h in range(SB_HEADS):
                heads_ref[:, h, :] = kv[:, h * SB_DIM:(h + 1) * SB_DIM]
    gq_ref[...] = seg(_C_GQ, GLA_KW) * GLA_DK ** -0.5
    gk_ref[...] = seg(_C_GK, GLA_KW)
    gv_ref[...] = seg(_C_GV, GLA_VW).astype(BF16)
    gr_ref[...] = seg(_C_GR, GLA_VW)
    mq_ref[...] = seg(_C_MQ, MEM_W).astype(BF16)
    glr = seg(_C_GLR, LANES).astype(BF16)
    la_ref[...] = _log_sigmoid(_dot(glr, wa2_ref[...]) + ba_ref[...]) * (1.0 / GLA_TAU)


def _proj(x, g, w_proj, wa2, ba, w_kvt, *, tm, seq_len, key_block):
    n, d = x.shape
    kv_transposed = w_kvt is not None
    row = lambda w: pl.BlockSpec((tm, w), lambda i: (i, 0))
    flat = lambda w, dt: (row(w), jax.ShapeDtypeStruct((n, w), dt))
    if kv_transposed:
        assert seq_len % tm == 0 and tm % key_block == 0
        nb, per_b = n // seq_len, seq_len // tm
        kv_bf = (pl.BlockSpec((None, tm // key_block, SB_W, key_block), lambda i: (i // per_b, i % per_b, 0, 0)),
                 jax.ShapeDtypeStruct((nb, seq_len // key_block, SB_W, key_block), BF16))
        kv_f32 = (pl.BlockSpec((None, SB_HEADS, SB_DIM, tm), lambda i: (i // per_b, 0, 0, i % per_b)),
                  jax.ShapeDtypeStruct((nb, SB_HEADS, SB_DIM, seq_len), F32))
    else:
        kv_bf = flat(SB_W, BF16)
        kv_f32 = (pl.BlockSpec((tm, SB_HEADS, SB_DIM), lambda i: (i, 0, 0)),
                  jax.ShapeDtypeStruct((n, SB_HEADS, SB_DIM), F32))
    outs = [flat(SB_W, BF16), kv_bf, kv_bf, kv_f32, kv_f32,
            flat(GLA_KW, F32), flat(GLA_KW, F32), flat(GLA_VW, BF16), flat(GLA_KW, F32), flat(GLA_VW, F32),
            flat(MEM_W, BF16)]
    args = [x, g, w_proj, wa2, ba] + ([w_kvt] if kv_transposed else [])
    return pl.pallas_call(
        functools.partial(_proj_kernel, kv_transposed=kv_transposed, key_block=key_block),
        grid=(n // tm,),
        in_specs=[row(d)] + [_resident(a) for a in args[1:]],
        out_specs=[o[0] for o in outs],
        out_shape=[o[1] for o in outs],
        compiler_params=_params("parallel"),
        name="proj",
    )(*args)


def _memkv_kernel(x_ref, g_ref, w_ref, mk_ref, mv_ref):
    xn = _rmsnorm(x_ref[...], g_ref[...]).astype(BF16)
    half = mk_ref.shape[-1]
    mk_ref[...] = _dot(xn, w_ref[:, :half])
    mv_ref[...] = _dot(xn, w_ref[:, half:])


def _memkv(x, g, w, tm):
    n, d = x.shape
    half = w.shape[1] // 2
    row = lambda wd: pl.BlockSpec((tm, wd), lambda i: (i, 0))
    return pl.pallas_call(
        _memkv_kernel,
        grid=(n // tm,),
        in_specs=[row(d), _resident(g), _resident(w)],
        out_specs=[row(half), row(half)],
        out_shape=[jax.ShapeDtypeStruct((n, half), F32)] * 2,
        compiler_params=_params("parallel"),
        name="memkv",
    )(x, g, w)


def _suffix_matrix(tk):
    j = lax.broadcasted_iota(jnp.int32, (2 * tk, tk), 0)
    s = lax.broadcasted_iota(jnp.int32, (2 * tk, tk), 1)
    return ((j > s) & ((j < tk) | (j > s + tk))).astype(BF16)


def _sb_weights(z, carry, mask):
    lz = jnp.minimum(z, 0.0) - jnp.log(1.0 + jnp.exp(-jnp.abs(z)))
    l1 = lz - z
    if mask is not None:
        l1 = jnp.where(mask, l1, 0.0)
    hi, lo = _split_bf16(l1)
    cum = _dot(jnp.concatenate([hi, lo], axis=1), _suffix_matrix(z.shape[1]))
    p = jnp.exp(lz + cum + carry)
    if mask is not None:
        p = jnp.where(mask, p, 0.0)
    return p.astype(BF16), carry + cum[:, :1] + l1[:, :1]


def _fold_descending(n_blocks, alive, fold_block):
    def cond(state):
        i, live = state
        return (i < n_blocks) & live

    def body(state):
        i, _ = state
        return i + 1, fold_block(n_blocks - 1 - i)

    return lax.while_loop(cond, body, (jnp.int32(0), alive))[1]


def _sb_prompt_kernel(q_ref, k_ref, v_ref, o_ref, acc_ref, car_ref, *, tq, pairs):
    qi = pl.program_id(2)
    heads = 2 * pairs
    lane = lax.broadcasted_iota(jnp.int32, (tq, LANES), 1)
    pair_cols = [slice(j * LANES, (j + 1) * LANES) for j in range(pairs)]
    q_heads = []
    for j in range(pairs):
        q = q_ref[:, pair_cols[j]]
        q_heads += [jnp.where(lane < SB_DIM, q, jnp.zeros_like(q)),
                    jnp.where(lane >= SB_DIM, q, jnp.zeros_like(q))]
    acc_ref[...] = jnp.zeros_like(acc_ref)
    car_ref[...] = jnp.zeros_like(car_ref)

    def block(kb, mask):
        z = jnp.concatenate([_dot(q_heads[h], k_ref[kb, pair_cols[h // 2], :]) for h in range(heads)], axis=0)
        p, carry = _sb_weights(z, car_ref[...], mask)
        car_ref[...] = carry
        for h in range(heads):
            acc_ref[h] += _dot_nt(p[h * tq:(h + 1) * tq], v_ref[kb, pair_cols[h // 2], :])
        return jnp.max(carry) > _LOG_F32_ZERO

    r = lax.broadcasted_iota(jnp.int32, (heads * tq, tq), 0)
    c = lax.broadcasted_iota(jnp.int32, (heads * tq, tq), 1)
    alive = block(qi, c < (r & (tq - 1)))
    _fold_descending(qi, alive, lambda kb: block(kb, None))
    for j in range(pairs):
        o_ref[:, pair_cols[j]] = jnp.where(lane < SB_DIM, acc_ref[2 * j], acc_ref[2 * j + 1]).astype(o_ref.dtype)


def _sb_prompt(q, k_t, v_t, *, tq, pairs):
    b, t, w = q.shape
    assert k_t.shape == (b, t // tq, w, tq) and tq & (tq - 1) == 0
    pw = pairs * LANES
    seq = pl.BlockSpec((None, t // tq, pw, tq), lambda bi, p, qi: (bi, 0, p, 0))
    tile = pl.BlockSpec((None, tq, pw), lambda bi, p, qi: (bi, qi, p))
    return pl.pallas_call(
        functools.partial(_sb_prompt_kernel, tq=tq, pairs=pairs),
        grid=(b, w // pw, t // tq),
        in_specs=[tile, seq, seq],
        out_specs=tile,
        out_shape=jax.ShapeDtypeStruct((b, t, w), BF16),
        scratch_shapes=[pltpu.VMEM((2 * pairs, tq, LANES), F32), pltpu.VMEM((2 * pairs * tq, 1), F32)],
        compiler_params=_params("parallel", "parallel", "arbitrary"),
        name="sb_prompt",
    )(q, k_t, v_t)


def _sb_sample_kernel(q_ref, k_ref, v_ref, pk_ref, pv_ref, o_ref, acc_ref, car_ref, *, tk_past):
    t = q_ref.shape[0]
    past_len = pk_ref.shape[-1]
    acc_ref[...] = jnp.zeros_like(acc_ref)
    car_ref[...] = jnp.zeros_like(car_ref)
    head_cols = [slice(h * SB_DIM, (h + 1) * SB_DIM) for h in range(SB_HEADS)]

    def fold_all_heads(logits, weighted_values, mask):
        z = jnp.concatenate([logits(h, q_ref[:, head_cols[h]]) for h in range(SB_HEADS)], axis=0)
        p, carry = _sb_weights(z, car_ref[...], mask)
        car_ref[...] = carry
        for h in range(SB_HEADS):
            acc_ref[h] += weighted_values(h, p[h * t:(h + 1) * t])
        return jnp.max(carry) > _LOG_F32_ZERO

    r = lax.broadcasted_iota(jnp.int32, (SB_HEADS * t, t), 0)
    c = lax.broadcasted_iota(jnp.int32, (SB_HEADS * t, t), 1)
    alive = fold_all_heads(lambda h, qh: _dot_nt(qh, k_ref[:, head_cols[h]]),
                           lambda h, ph: _dot(ph, v_ref[:, head_cols[h]]),
                           c < (r & (t - 1)))

    def past_block(kb_idx):
        pos = pl.ds(pl.multiple_of(kb_idx * tk_past, tk_past), tk_past)
        return fold_all_heads(lambda h, qh: _dot(qh, pk_ref[h, :, pos].astype(BF16)),
                              lambda h, ph: _dot_nt(ph, pv_ref[h, :, pos].astype(BF16)), None)

    _fold_descending(past_len // tk_past, alive, past_block)
    for h in range(SB_HEADS):
        o_ref[:, head_cols[h]] = acc_ref[h].astype(o_ref.dtype)


def _sb_sample(q, k, v, past_k, past_v, *, tk_past):
    b, t, w = q.shape
    assert t & (t - 1) == 0, "query chunk length must be a power of two"
    cur = pl.BlockSpec((None, t, w), lambda bi: (bi, 0, 0))
    past = pl.BlockSpec((None,) + past_k.shape[1:], lambda bi: (bi, 0, 0, 0))
    return pl.pallas_call(
        functools.partial(_sb_sample_kernel, tk_past=tk_past),
        grid=(b,),
        in_specs=[cur, cur, cur, past, past],
        out_specs=cur,
        out_shape=jax.ShapeDtypeStruct((b, t, w), BF16),
        scratch_shapes=[pltpu.VMEM((SB_HEADS, t, SB_DIM), F32), pltpu.VMEM((SB_HEADS * t, 1), F32)],
        compiler_params=_params("parallel"),
        name="sb_sample",
    )(q, k, v, past_k, past_v)


def _gla_kernel(gq_ref, gk_ref, gv_ref, la_ref, gr_ref, gn_ref, s0_ref, o_ref, st_ref, st_scr, *, n_chunks):
    c_len = GLA_CHUNK
    tt = n_chunks * c_len
    kw = GLA_HEADS * GLA_DK
    t = pl.program_id(1)

    @pl.when(t == 0)
    def _():
        st_scr[...] = s0_ref[...]

    r = lax.broadcasted_iota(jnp.int32, (tt, tt), 0)
    c = lax.broadcasted_iota(jnp.int32, (tt, tt), 1)
    chunk_bits = c_len.bit_length() - 1
    same_chunk = (r >> chunk_bits) == (c >> chunk_bits)
    tril = (same_chunk & (r >= c)).astype(BF16)
    ones = same_chunk.astype(BF16)
    la_hi, la_lo = _split_bf16(la_ref[...])
    b = _dot(tril, la_hi) + _dot(tril, la_lo)
    b_tot = _dot(ones, la_hi) + _dot(ones, la_lo)
    gk = gk_ref[...]
    qd = gq_ref[...] * jnp.exp(b)
    kd = (gk * jnp.exp(-b)).astype(BF16)
    ke = (gk * jnp.exp(b_tot - b)).astype(BF16)
    decay = jnp.exp(b_tot)

    rc = lax.broadcasted_iota(jnp.int32, (GLA_HEADS * c_len, c_len), 0)
    cc = lax.broadcasted_iota(jnp.int32, (GLA_HEADS * c_len, c_len), 1)
    causal = (rc & (c_len - 1)) >= cc
    lane_q = lax.broadcasted_iota(jnp.int32, (c_len, kw), 1)
    lane_s = lax.broadcasted_iota(jnp.int32, (GLA_DV, kw), 1)

    for ci in range(n_chunks):
        rows = slice(ci * c_len, (ci + 1) * c_len)
        qd_c = qd[rows]
        qs = jnp.concatenate(
            [jnp.where((lane_q >= h * GLA_DK) & (lane_q < (h + 1) * GLA_DK), qd_c, 0.0)
             for h in range(GLA_HEADS)], axis=0).astype(BF16)
        att = jnp.where(causal, _dot_nt(qs, kd[rows]), 0.0).astype(BF16)
        st = st_scr[...]
        o_state = _dot_nt(qs, st.astype(BF16))
        v = gv_ref[rows, :]
        upd = lax.dot_general(v, ke[rows], _TN, preferred_element_type=F32)
        st_new = st * decay[ci * c_len:ci * c_len + 1, :]
        for h in range(GLA_HEADS):
            hrows = slice(h * c_len, (h + 1) * c_len)
            cols = slice(h * GLA_DV, (h + 1) * GLA_DV)
            o = _dot(att[hrows], v[:, cols]) + o_state[hrows]
            on = _rmsnorm(o, gn_ref[:, cols])
            gr = gr_ref[rows, cols]
            o_ref[rows, cols] = (on * (gr * jax.nn.sigmoid(gr))).astype(o_ref.dtype)
            in_head = (lane_s >= h * GLA_DK) & (lane_s < (h + 1) * GLA_DK)
            st_new = st_new + jnp.where(in_head, upd[cols], 0.0)
        st_scr[...] = st_new

    @pl.when(t == pl.num_programs(1) - 1)
    def _():
        st_ref[...] = st_scr[...]


def _gla(gq, gk, gv, la, gr, gn, st0, *, tt):
    b, t, kw = gq.shape
    vw = gv.shape[-1]
    tok = lambda w: pl.BlockSpec((None, tt, w), lambda bi, ti: (bi, ti, 0))
    state = pl.BlockSpec((None, GLA_DV, kw), lambda bi, ti: (bi, 0, 0))
    return pl.pallas_call(
        functools.partial(_gla_kernel, n_chunks=tt // GLA_CHUNK),
        grid=(b, t // tt),
        in_specs=[tok(kw), tok(kw), tok(vw), tok(kw), tok(vw), _resident(gn), state],
        out_specs=[tok(vw), state],
        out_shape=[jax.ShapeDtypeStruct((b, t, vw), BF16), jax.ShapeDtypeStruct((b, GLA_DV, kw), F32)],
        scratch_shapes=[pltpu.VMEM((GLA_DV, kw), F32)],
        compiler_params=_params("parallel", "arbitrary"),
        name="gla",
    )(gq, gk, gv, la, gr, gn, st0)


def _mem_kernel(q_ref, mk_ref, mv_ref, o_ref):
    head_major = len(mk_ref.shape) == 3
    for h in range(MEM_HEADS):
        cols = slice(h * MEM_DIM, (h + 1) * MEM_DIM)
        kh = mk_ref[:, h, :] if head_major else mk_ref[:, cols]
        vh = mv_ref[:, h, :] if head_major else mv_ref[:, cols]
        s = _dot_nt(q_ref[:, cols], kh.astype(BF16)) * MEM_DIM ** -0.5
        e = jnp.exp(s - jnp.max(s, axis=-1, keepdims=True))
        p = e * (1.0 / jnp.sum(e, axis=-1, keepdims=True))
        o_ref[:, cols] = _dot(p.astype(BF16), vh.astype(BF16)).astype(o_ref.dtype)


def _mem_attention(q, mk, mv, *, tq):
    b, t, w = q.shape
    tile = pl.BlockSpec((None, tq, w), lambda bi, qi: (bi, qi, 0))
    mem = pl.BlockSpec((None,) + mk.shape[1:], lambda bi, qi: (bi,) + (0,) * (mk.ndim - 1))
    return pl.pallas_call(
        _mem_kernel,
        grid=(b, t // tq),
        in_specs=[tile, mem, mem],
        out_specs=tile,
        out_shape=jax.ShapeDtypeStruct((b, t, w), BF16),
        compiler_params=_params("parallel", "arbitrary"),
        name="mem_attention",
    )(q, mk, mv)


def _merge_kernel(x_ref, osb_ref, ogla_ref, omem_ref, gmix_ref, wgt_ref, wbr_ref, wout_ref,
                  gffn_ref, wr_hi_ref, wr_lo_ref, br_ref, h_ref, ht_ref, idx_ref, wts_ref, cnt_ref, run_ref):
    x = x_ref[...]
    d = x.shape[-1]
    xn = _rmsnorm(x, gmix_ref[...]).astype(BF16)
    mixed = None
    for n, o_ref in enumerate((osb_ref, ogla_ref, omem_ref)):
        gate = jax.nn.sigmoid(_dot(xn, wgt_ref[:, n * d:(n + 1) * d]))
        term = gate * _dot(o_ref[...], wbr_ref[n])
        mixed = term if mixed is None else mixed + term
    h = x + _dot(mixed.astype(BF16), wout_ref[...])
    h_ref[...] = h
    hn = _rmsnorm(h, gffn_ref[...])
    ht_ref[...] = hn

    hn_hi, hn_lo = _split_bf16(hn)
    logits = (_dot(hn_hi, wr_hi_ref[...]) + _dot(hn_lo, wr_hi_ref[...])
              + _dot(hn_hi, wr_lo_ref[...]) + br_ref[...])
    lane = lax.broadcasted_iota(jnp.int32, logits.shape, 1)
    rmax = lambda a: jnp.max(a, axis=-1, keepdims=True)
    rmin = lambda a: jnp.min(a, axis=-1, keepdims=True)
    rsum = lambda a: jnp.sum(a, axis=-1, keepdims=True)

    lc = jnp.where(lane < N_GROUPS, logits, NEG_BIG)
    mc = rmax(lc)
    grp = rmin(jnp.where(lc == mc, lane, LANES))
    p_grp = 1.0 / rsum(jnp.exp(lc - mc))

    lo = N_GROUPS + grp * EXPERTS_PER_GROUP
    in_grp = (lane >= lo) & (lane < lo + EXPERTS_PER_GROUP)
    lf = jnp.where(in_grp, logits, NEG_BIG)
    ef = jnp.exp(lf - rmax(lf))
    pf = jnp.where(in_grp, ef / rsum(ef), -1.0)
    v1 = rmax(pf)
    i1 = rmin(jnp.where(pf == v1, lane, LANES))
    pf2 = jnp.where(lane == i1, -1.0, pf)
    v2 = rmax(pf2)
    i2 = rmin(jnp.where(pf2 == v2, lane, LANES))
    tot = v1 + v2
    e1 = i1 - N_GROUPS
    e2 = i2 - N_GROUPS
    wts_ref[...] = jnp.where(lane == 0, p_grp * (v1 / tot), jnp.where(lane == 1, p_grp * (v2 / tot), 0.0))

    @pl.when(pl.program_id(0) == 0)
    def _():
        run_ref[...] = jnp.zeros_like(run_ref)

    tm = x.shape[0]
    hit1 = lane == e1
    hit2 = lane == e2
    both = (hit1 | hit2).astype(BF16)
    earlier = (lax.broadcasted_iota(jnp.int32, (tm, tm), 1)
               < lax.broadcasted_iota(jnp.int32, (tm, tm), 0)).astype(BF16)
    before = _dot(earlier, both) + run_ref[...]
    r1 = rsum(jnp.where(hit1, before, 0.0)).astype(jnp.int32)
    r2 = rsum(jnp.where(hit2, before, 0.0)).astype(jnp.int32)
    run = before[tm - 1:tm, :] + both[tm - 1:tm, :].astype(F32)
    run_ref[...] = run
    cnt_ref[...] = run
    idx_ref[...] = jnp.where(lane == 0, e1, jnp.where(lane == 1, e2,
                             jnp.where(lane == 2, r1, jnp.where(lane == 3, r2, 0))))


def _merge(x, osb, ogla, omem, gmix, wgt, wbr, wout, gffn, wr_hi, wr_lo, br, *, tm):
    n, d = x.shape
    bw = osb.shape[1]
    row = lambda w: pl.BlockSpec((tm, w), lambda i: (i, 0))
    return pl.pallas_call(
        _merge_kernel,
        grid=(n // tm,),
        in_specs=[row(d), row(bw), row(bw), row(bw), _resident(gmix), _resident(wgt), _resident(wbr),
                  _resident(wout), _resident(gffn), _resident(wr_hi), _resident(wr_lo), _resident(br)],
        out_specs=[row(d), row(d), row(LANES), row(LANES), pl.BlockSpec((1, LANES), lambda i: (0, 0))],
        out_shape=[jax.ShapeDtypeStruct((n, d), F32), jax.ShapeDtypeStruct((n, d), F32),
                   jax.ShapeDtypeStruct((n, LANES), jnp.int32), jax.ShapeDtypeStruct((n, LANES), F32),
                   jax.ShapeDtypeStruct((1, LANES), F32)],
        scratch_shapes=[pltpu.VMEM((1, LANES), F32)],
        compiler_params=_params("arbitrary"),
        name="merge_route",
    )(x, osb, ogla, omem, gmix, wgt, wbr, wout, gffn, wr_hi, wr_lo, br)


def _dispatch_kernel(pend_ref, dest_ref, *refs, tm_e, group_steps):
    x_refs = refs[:len(group_steps)]
    xb_ref, zero_ref, sem = refs[len(group_steps):]
    groups = x_refs[0].shape[0]
    step = pl.program_id(0)

    @pl.when(step == 0)
    def _():
        zero_ref[...] = jnp.zeros_like(zero_ref)

        def segment_end(e):
            end = pend_ref[e]
            prev = jnp.where(e > 0, pend_ref[jnp.maximum(e - 1, 0)], 0)
            return end, end > prev

        def zero_copy(end):
            first = pl.multiple_of(end - tm_e, tm_e)
            return pltpu.make_async_copy(zero_ref, xb_ref.at[pl.ds(first, tm_e), :], sem)

        def start(e, carry):
            end, used = segment_end(e)

            @pl.when(used)
            def _():
                zero_copy(end).start()
            return carry

        def wait(e, carry):
            end, used = segment_end(e)

            @pl.when(used)
            def _():
                zero_copy(end).wait()
            return carry

        lax.fori_loop(0, N_EXPERTS, start, 0)
        lax.fori_loop(0, N_EXPERTS, wait, 0)

        first_free = pend_ref[N_EXPERTS - 1] // tm_e
        n_blocks = xb_ref.shape[0] // tm_e
        lax.fori_loop(first_free, n_blocks, lambda j, c: (zero_copy((j + 1) * tm_e).start(), c)[1], 0)
        lax.fori_loop(first_free, n_blocks, lambda j, c: (zero_copy((j + 1) * tm_e).wait(), c)[1], 0)

    def copy_tile(x_ref):
        def issue(g, carry):
            for u in range(SUBLANES):
                for k in range(TOP_K):
                    slot = dest_ref[(g * SUBLANES + u) * TOP_K + k]
                    pltpu.make_async_copy(x_ref.at[g, pl.ds(u, 1), :], xb_ref.at[pl.ds(slot, 1), :], sem).start()
            return carry

        lax.fori_loop(0, groups, issue, 0)

    first = 0
    for x_ref, n_steps in zip(x_refs, group_steps):
        pl.when((step >= first) & (step < first + n_steps))(functools.partial(copy_tile, x_ref))
        first += n_steps
    tile_rows = xb_ref.at[pl.ds(0, groups * SUBLANES), :]
    for k in range(TOP_K):
        pltpu.make_async_copy(tile_rows, tile_rows, sem).wait()


def _dispatch(xs, dests, pend, n_slots, *, tm, tm_e):
    d = xs[0].shape[1]
    group_steps = [x.shape[0] // tm for x in xs]
    firsts = [sum(group_steps[:j]) for j in range(len(xs))]
    x_specs = [pl.BlockSpec((tm // SUBLANES, SUBLANES, d),
                            functools.partial(lambda i, pe, f, s: (jnp.clip(i - f, 0, s - 1), 0, 0), f=f, s=s))
               for f, s in zip(firsts, group_steps)]
    grid_spec = pltpu.PrefetchScalarGridSpec(
        num_scalar_prefetch=1,
        grid=(sum(group_steps),),
        in_specs=[pl.BlockSpec((tm * TOP_K,), lambda i, pe: (i,), memory_space=pltpu.SMEM)] + x_specs,
        out_specs=pl.BlockSpec(memory_space=pl.ANY),
        scratch_shapes=[pltpu.VMEM((tm_e, d), F32), pltpu.SemaphoreType.DMA(())],
    )
    return pl.pallas_call(
        functools.partial(_dispatch_kernel, tm_e=tm_e, group_steps=tuple(group_steps)),
        grid_spec=grid_spec,
        out_shape=jax.ShapeDtypeStruct((n_slots, d), F32),
        compiler_params=_params("arbitrary"),
        name="dispatch",
    )(pend, jnp.concatenate(dests), *[x.reshape(x.shape[0] // SUBLANES, SUBLANES, d) for x in xs])


def _expert_kernel(blk_e_ref, n_used_ref, x_ref, wg_ref, wu_ref, wd_ref, y_ref, wg_bf, wu_bf, wd_bf):
    i = pl.program_id(0)

    @pl.when((i == 0) | (blk_e_ref[i] != blk_e_ref[jnp.maximum(i - 1, 0)]))
    def _():
        wg_bf[...] = wg_ref[...].astype(BF16)
        wu_bf[...] = wu_ref[...].astype(BF16)
        wd_bf[...] = wd_ref[...].astype(BF16)

    @pl.when(i < n_used_ref[0])
    def _():
        x = x_ref[...].astype(BF16)
        g = _dot(x, wg_bf[...])
        u = _dot(x, wu_bf[...])
        a = (g * jax.nn.sigmoid(g) * u).astype(BF16)
        y_ref[...] = _dot(a, wd_bf[...])

    @pl.when(i >= n_used_ref[0])
    def _():
        y_ref[...] = jnp.zeros_like(y_ref)


def _experts(xb, blk_e, n_used, wg, wu, wd, *, tm):
    p, d = xb.shape
    de = wg.shape[-1]
    grid_spec = pltpu.PrefetchScalarGridSpec(
        num_scalar_prefetch=2,
        grid=(p // tm,),
        in_specs=[pl.BlockSpec((tm, d), lambda i, be, nu: (jnp.minimum(i, nu[0] - 1), 0)),
                  pl.BlockSpec((None, d, de), lambda i, be, nu: (be[i], 0, 0)),
                  pl.BlockSpec((None, d, de), lambda i, be, nu: (be[i], 0, 0)),
                  pl.BlockSpec((None, de, d), lambda i, be, nu: (be[i], 0, 0))],
        out_specs=pl.BlockSpec((tm, d), lambda i, be, nu: (i, 0)),
        scratch_shapes=[pltpu.VMEM((d, de), BF16), pltpu.VMEM((d, de), BF16), pltpu.VMEM((de, d), BF16)],
    )
    return pl.pallas_call(
        _expert_kernel,
        grid_spec=grid_spec,
        out_shape=jax.ShapeDtypeStruct((p, d), F32),
        compiler_params=_params("arbitrary"),
        name="experts",
    )(blk_e, n_used, xb, wg, wu, wd)


def _final_kernel(pos_ref, pos_next_ref, h_ref, wts_ref, g_ref, yb_ref, o_ref, rows_ref, sem, *, n_steps):
    i = pl.program_id(0)
    tm, d = h_ref.shape
    cur = i % 2

    def gather(slots_ref, buf):
        def issue(g, carry):
            for u in range(SUBLANES):
                for k in range(TOP_K):
                    slot = slots_ref[(g * SUBLANES + u) * TOP_K + k]
                    pltpu.make_async_copy(yb_ref.at[pl.ds(slot, 1), :],
                                          rows_ref.at[buf, k, g, pl.ds(u, 1), :], sem.at[buf]).start()
            return carry

        lax.fori_loop(0, tm // SUBLANES, issue, 0)

    @pl.when(i == 0)
    def _():
        gather(pos_ref, 0)

    @pl.when(i + 1 < n_steps)
    def _():
        gather(pos_next_ref, 1 - cur)

    for k in range(TOP_K):
        pltpu.make_async_copy(rows_ref.at[cur, k], rows_ref.at[cur, k], sem.at[cur]).wait()
    w = wts_ref[...]
    y0 = rows_ref[cur, 0].reshape(tm, d)
    y1 = rows_ref[cur, 1].reshape(tm, d)
    o_ref[...] = _rmsnorm(h_ref[...] + (y0 * w[:, 0:1] + y1 * w[:, 1:2]), g_ref[...])


def _final(h, yb, pos, wts, g, *, tm):
    n, d = h.shape
    steps = n // tm
    row = lambda w: pl.BlockSpec((tm, w), lambda i: (i, 0))
    slots = lambda nxt: pl.BlockSpec((tm * TOP_K,), lambda i: (jnp.minimum(i + nxt, steps - 1),),
                                     memory_space=pltpu.SMEM)
    return pl.pallas_call(
        functools.partial(_final_kernel, n_steps=steps),
        grid=(steps,),
        in_specs=[slots(0), slots(1), row(d), row(LANES), _resident(g), pl.BlockSpec(memory_space=pl.ANY)],
        out_specs=row(d),
        out_shape=jax.ShapeDtypeStruct((n, d), F32),
        scratch_shapes=[pltpu.VMEM((2, TOP_K, tm // SUBLANES, SUBLANES, d), F32),
                        pltpu.SemaphoreType.DMA((2,))],
        compiler_params=_params("arbitrary"),
        name="combine_norm",
    )(pos, pos, h, wts, g, yb)


def _dispatch_plan(routes, counts, tm):
    counts = [c[0, :N_EXPERTS].astype(jnp.int32) for c in counts]
    total = sum(counts)
    padded = (total + tm - 1) // tm * tm
    pend = jnp.cumsum(padded)
    experts = jnp.arange(N_EXPERTS, dtype=jnp.int32)
    dests, first, n_assign = [], pend - padded, 0
    for route, cnt in zip(routes, counts):
        a = route.shape[0] * TOP_K
        flat_e = route[:, :TOP_K].reshape(a)
        rank = route[:, TOP_K:2 * TOP_K].reshape(a)
        base = jnp.sum(jnp.where(flat_e[:, None] == experts[None, :], first[None, :], 0), axis=1)
        dests.append((base + rank).astype(jnp.int32))
        first = first + cnt
        n_assign += a
    n_blk = -(-(n_assign + N_EXPERTS * (tm - 1)) // tm)
    blk_start = jnp.arange(n_blk, dtype=jnp.int32) * tm
    blk_e = jnp.minimum(jnp.sum((pend[None, :] <= blk_start[:, None]).astype(jnp.int32), axis=1),
                        N_EXPERTS - 1).astype(jnp.int32)
    n_used = (pend[-1] // tm).astype(jnp.int32).reshape(1)
    return dests, pend.astype(jnp.int32), blk_e, n_used, n_blk


def _moe(groups, wg, wu, wd, g_final, *, tm_e):
    dests, pend, blk_e, n_used, n_blk = _dispatch_plan(
        [g["route"] for g in groups], [g["counts"] for g in groups], tm_e)
    tm_dispatch = 2 * ROW_TILE
    while any(g["ht"].shape[0] % tm_dispatch for g in groups):
        tm_dispatch //= 2
    xb = _dispatch([g["ht"] for g in groups], dests, pend, n_blk * tm_e, tm=tm_dispatch, tm_e=tm_e)
    yb = _experts(xb, blk_e, n_used, wg, wu, wd, tm=tm_e)
    return [_final(g["h"], yb, dest, g["rw"], g_final, tm=g["tm"]) for g, dest in zip(groups, dests)]


def _layer(x, sb_past, gla_state, mem_kv, wts, *, tm, tq_sb, tt_gla, tq_mem):
    b, t, d = x.shape
    n = b * t
    xf = x.reshape(n, d)
    sq, skb, svb, sk, sv, gq, gk, gv, la, gr, mq = _proj(
        xf, wts["g_mix"], wts["w_proj"], wts["wa2"], wts["ba"], wts["w_kvt"] if sb_past is None else None,
        tm=tm, seq_len=t, key_block=tq_sb)
    r3 = lambda a: a.reshape(b, t, a.shape[-1])
    if sb_past is None:
        o_sb = _sb_prompt(r3(sq), skb, svb, tq=tq_sb, pairs=4)
    else:
        o_sb = _sb_sample(r3(sq), r3(skb), r3(svb), sb_past[0], sb_past[1], tk_past=SB_TILE)
    o_gla, st = _gla(r3(gq), r3(gk), r3(gv), r3(la), r3(gr), wts["g_gla"], gla_state, tt=tt_gla)
    o_mem = _mem_attention(r3(mq), mem_kv[0], mem_kv[1], tq=tq_mem)
    h, ht, route, rw, counts = _merge(xf, o_sb.reshape(n, -1), o_gla.reshape(n, -1), o_mem.reshape(n, -1),
                                      wts["g_mix"], wts["w_gt"], wts["w_br"], wts["w_out"], wts["g_ffn"],
                                      wts["wr_hi"], wts["wr_lo"], wts["b_r"], tm=tm)
    return dict(h=h, ht=ht, route=route, rw=rw, counts=counts, tm=tm), sk, sv, st


ROW_TILE = 512
SB_TILE = 256
GLA_TILE = 256


def _tiles(b, t, *, chunked):
    if chunked:
        return dict(tm=min(ROW_TILE, b * t), tq_sb=t, tt_gla=t, tq_mem=t)
    return dict(tm=min(ROW_TILE, b * t), tq_sb=min(SB_TILE, t), tt_gla=min(GLA_TILE, t),
                tq_mem=min(ROW_TILE, t))


def _state_to_t(s):
    b = s.shape[0]
    return s.transpose(0, 3, 1, 2).reshape(b, GLA_DV, GLA_HEADS * GLA_DK)


def _state_from_t(st):
    b = st.shape[0]
    return st.reshape(b, GLA_DV, GLA_HEADS, GLA_DK).transpose(0, 2, 3, 1)


def kernel(x_prompt, x_sample, mem_prompt, cache_sb_k, cache_sb_v, state_gla, cache_mem_k, cache_mem_v,
           norm_mix, w_in, w_gla_a2, b_gla_a, gla_norm, norm_mem, w_mem_kv, w_branch, w_out,
           norm_ffn, w_coarse, b_coarse, w_fine, b_fine, w_e_gate, w_e_up, w_e_down, norm_final):
    assert w_in.shape[0] == 1, "single-layer model"
    d = x_prompt.shape[-1]
    bp, tp, _ = x_prompt.shape
    bs, ts, _ = x_sample.shape
    bw = d // 2
    assert bw == SB_W == GLA_VW == MEM_W

    w = w_in[0]
    offs = [0]
    for s in (SB_W, SB_W, SB_W, GLA_KW, GLA_KW, GLA_VW, GLA_RANK, GLA_VW, MEM_W, N_BRANCH * d):
        offs.append(offs[-1] + s)
    col = lambda i: w[:, offs[i]:offs[i + 1]]
    w_proj = jnp.concatenate(
        [col(0), col(1), col(2), col(3), col(4), col(5), col(7), col(8),
         jnp.pad(col(6), ((0, 0), (0, LANES - GLA_RANK)))], axis=1).astype(BF16)
    assert w_proj.shape[1] == _PROJ_W
    row2 = lambda a: a.reshape(1, -1)
    n_route = N_GROUPS + N_EXPERTS
    w_route = jnp.concatenate([w_coarse[0], w_fine[0].transpose(1, 0, 2).reshape(d, N_EXPERTS)], axis=1)
    w_route = jnp.pad(w_route, ((0, 0), (0, LANES - n_route)))
    wr_hi = w_route.astype(BF16)
    wr_lo = (w_route - wr_hi.astype(F32)).astype(BF16)
    b_route = jnp.pad(jnp.concatenate([b_coarse[0], b_fine[0].reshape(-1)]), (0, LANES - n_route)).reshape(1, LANES)
    wts = dict(
        g_mix=row2(norm_mix[0]), w_proj=w_proj,
        w_kvt=jnp.concatenate([col(1), col(2)], axis=1).T.astype(BF16),
        wa2=jnp.pad(w_gla_a2[0], ((0, LANES - GLA_RANK), (0, 0))).astype(BF16), ba=row2(b_gla_a[0]),
        g_gla=row2(gla_norm[0]),
        w_gt=col(9).astype(BF16), w_br=w_branch[0].astype(BF16), w_out=w_out[0].astype(BF16),
        g_ffn=row2(norm_ffn[0]), wr_hi=wr_hi, wr_lo=wr_lo, b_r=b_route,
        wg=w_e_gate[0], wu=w_e_up[0], wd=w_e_down[0],
        g_final=row2(norm_final),
    )

    m = mem_prompt.shape[1]
    mk, mv = _memkv(mem_prompt.reshape(bp * m, d), row2(norm_mem[0]), w_mem_kv[0].astype(BF16),
                    tm=min(ROW_TILE, bp * m))
    st0 = jnp.zeros((bp, GLA_DV, GLA_HEADS * GLA_DK), F32)
    grp_p, sk_p, sv_p, st_p = _layer(x_prompt, None, st0, (mk.reshape(bp, m, bw), mv.reshape(bp, m, bw)), wts,
                                     **_tiles(bp, tp, chunked=False))

    past = (cache_sb_k[0].transpose(0, 2, 3, 1), cache_sb_v[0].transpose(0, 2, 3, 1))
    grp_s, sk_s, sv_s, st_s = _layer(x_sample, past, _state_to_t(state_gla[0]),
                                     (cache_mem_k[0], cache_mem_v[0]), wts, **_tiles(bs, ts, chunked=True))

    y_p, y_s = _moe([grp_p, grp_s], wts["wg"], wts["wu"], wts["wd"], wts["g_final"], tm_e=ROW_TILE)
    y_p, y_s = y_p.reshape(bp, tp, d), y_s.reshape(bs, ts, d)

    hd = lambda a, bb, tt: a.reshape(1, bb, tt, SB_HEADS, SB_DIM)
    return (y_p, y_s,
            sk_p.transpose(0, 3, 1, 2)[None], sv_p.transpose(0, 3, 1, 2)[None],
            _state_from_t(st_p)[None],
            mk.reshape(1, bp, m, MEM_HEADS, MEM_DIM), mv.reshape(1, bp, m, MEM_HEADS, MEM_DIM),
            hd(sk_s, bs, ts), hd(sv_s, bs, ts),
            _state_from_t(st_s)[None])
```

```python
import functools

import jax
import jax.numpy as jnp
from jax import lax
from jax.experimental import pallas as pl
from jax.experimental.pallas import tpu as pltpu

F32 = jnp.float32
BF16 = jnp.bfloat16
EPS = 1e-6

SB_HEADS = 8
SB_DIM = 64
GLA_HEADS = 4
GLA_DK = 64
GLA_DV = 128
GLA_RANK = 16
GLA_TAU = 16.0
GLA_CHUNK = 64
MEM_HEADS = 4
MEM_DIM = 128
N_BRANCH = 3
N_GROUPS = 4
EXPERTS_PER_GROUP = 8
N_EXPERTS = N_GROUPS * EXPERTS_PER_GROUP
TOP_K = 2

LANES = 128
SUBLANES = 8
VMEM_LIMIT = 56 * 1024 * 1024
NEG_BIG = -1e30
_LOG_F32_ZERO = -110.0

_NT = (((1,), (1,)), ((), ()))
_TN = (((0,), (0,)), ((), ()))


def _dot(a, b):
    return jnp.dot(a, b, preferred_element_type=F32)


def _dot_nt(a, b):
    return lax.dot_general(a, b, _NT, preferred_element_type=F32)


def _split_bf16(x):
    hi = x.astype(BF16)
    lo = (x - hi.astype(F32)).astype(BF16)
    return hi, lo


def _rmsnorm(x, g):
    return x * lax.rsqrt(jnp.mean(x * x, axis=-1, keepdims=True) + EPS) * g


def _log_sigmoid(x):
    return jnp.minimum(x, 0.0) - jnp.log1p(jnp.exp(-jnp.abs(x)))


def _resident(a):
    return pl.BlockSpec(a.shape, lambda *_: (0,) * a.ndim, pipeline_mode=pl.Buffered(1))


def _params(*sem):
    return pltpu.CompilerParams(dimension_semantics=sem, vmem_limit_bytes=VMEM_LIMIT)


SB_W = SB_HEADS * SB_DIM
GLA_KW = GLA_HEADS * GLA_DK
GLA_VW = GLA_HEADS * GLA_DV
MEM_W = MEM_HEADS * MEM_DIM
_C_SQ = 0
_C_SK = _C_SQ + SB_W
_C_SV = _C_SK + SB_W
_C_GQ = _C_SV + SB_W
_C_GK = _C_GQ + GLA_KW
_C_GV = _C_GK + GLA_KW
_C_GR = _C_GV + GLA_VW
_C_MQ = _C_GR + GLA_VW
_C_GLR = _C_MQ + MEM_W
_PROJ_W = _C_GLR + LANES


def _proj_kernel(*refs, kv_transposed, key_block):
    if kv_transposed:
        x_ref, g_ref, w_ref, wa2_ref, ba_ref, wkvt_ref = refs[:6]
    else:
        x_ref, g_ref, w_ref, wa2_ref, ba_ref = refs[:5]
    sq_ref, skb_ref, svb_ref, sk_ref, sv_ref, gq_ref, gk_ref, gv_ref, la_ref, gr_ref, mq_ref = refs[-11:]
    xn = _rmsnorm(x_ref[...], g_ref[...]).astype(BF16)

    def seg(start, width):
        return _dot(xn, w_ref[:, start:start + width])

    sq_ref[...] = (seg(_C_SQ, SB_W) * SB_DIM ** -0.5).astype(BF16)
    if kv_transposed:
        tm = xn.shape[0]
        for half, (bf_ref, heads_ref) in enumerate(((skb_ref, sk_ref), (svb_ref, sv_ref))):
            kv_t = _dot_nt(wkvt_ref[half * SB_W:(half + 1) * SB_W, :], xn)
            heads_ref[...] = kv_t.reshape(SB_HEADS, SB_DIM, tm)
            for j in range(tm // key_block):
                bf_ref[j] = kv_t[:, j * key_block:(j + 1) * key_block].astype(BF16)
    else:
        for c0, bf_ref, heads_ref in ((_C_SK, skb_ref, sk_ref), (_C_SV, svb_ref, sv_ref)):
            kv = seg(c0, SB_W)
            bf_ref[...] = kv.astype(BF16)
            for h in range(SB_HEADS):
                heads_ref[:, h, :] = kv[:, h * SB_DIM:(h + 1) * SB_DIM]
    gq_ref[...] = seg(_C_GQ, GLA_KW) * GLA_DK ** -0.5
    gk_ref[...] = seg(_C_GK, GLA_KW)
    gv_ref[...] = seg(_C_GV, GLA_VW).astype(BF16)
    gr_ref[...] = seg(_C_GR, GLA_VW)
    mq_ref[...] = seg(_C_MQ, MEM_W).astype(BF16)
    glr = seg(_C_GLR, LANES).astype(BF16)
    la_ref[...] = _log_sigmoid(_dot(glr, wa2_ref[...]) + ba_ref[...]) * (1.0 / GLA_TAU)


def _proj(x, g, w_proj, wa2, ba, w_kvt, *, tm, seq_len, key_block):
    n, d = x.shape
    kv_transposed = w_kvt is not None
    row = lambda w: pl.BlockSpec((tm, w), lambda i: (i, 0))
    flat = lambda w, dt: (row(w), jax.ShapeDtypeStruct((n, w), dt))
    if kv_transposed:
        assert seq_len % tm == 0 and tm % key_block == 0
        nb, per_b = n // seq_len, seq_len // tm
        kv_bf = (pl.BlockSpec((None, tm // key_block, SB_W, key_block), lambda i: (i // per_b, i % per_b, 0, 0)),
                 jax.ShapeDtypeStruct((nb, seq_len // key_block, SB_W, key_block), BF16))
        kv_f32 = (pl.BlockSpec((None, SB_HEADS, SB_DIM, tm), lambda i: (i // per_b, 0, 0, i % per_b)),
                  jax.ShapeDtypeStruct((nb, SB_HEADS, SB_DIM, seq_len), F32))
    else:
        kv_bf = flat(SB_W, BF16)
        kv_f32 = (pl.BlockSpec((tm, SB_HEADS, SB_DIM), lambda i: (i, 0, 0)),
                  jax.ShapeDtypeStruct((n, SB_HEADS, SB_DIM), F32))
    outs = [flat(SB_W, BF16), kv_bf, kv_bf, kv_f32, kv_f32,
            flat(GLA_KW, F32), flat(GLA_KW, F32), flat(GLA_VW, BF16), flat(GLA_KW, F32), flat(GLA_VW, F32),
            flat(MEM_W, BF16)]
    args = [x, g, w_proj, wa2, ba] + ([w_kvt] if kv_transposed else [])
    return pl.pallas_call(
        functools.partial(_proj_kernel, kv_transposed=kv_transposed, key_block=key_block),
        grid=(n // tm,),
        in_specs=[row(d)] + [_resident(a) for a in args[1:]],
        out_specs=[o[0] for o in outs],
        out_shape=[o[1] for o in outs],
        compiler_params=_params("parallel"),
        name="proj",
    )(*args)


def _memkv_kernel(x_ref, g_ref, w_ref, mk_ref, mv_ref):
    xn = _rmsnorm(x_ref[...], g_ref[...]).astype(BF16)
    half = mk_ref.shape[-1]
    mk_ref[...] = _dot(xn, w_ref[:, :half])
    mv_ref[...] = _dot(xn, w_ref[:, half:])


def _memkv(x, g, w, tm):
    n, d = x.shape
    half = w.shape[1] // 2
    row = lambda wd: pl.BlockSpec((tm, wd), lambda i: (i, 0))
    return pl.pallas_call(
        _memkv_kernel,
        grid=(n // tm,),
        in_specs=[row(d), _resident(g), _resident(w)],
        out_specs=[row(half), row(half)],
        out_shape=[jax.ShapeDtypeStruct((n, half), F32)] * 2,
        compiler_params=_params("parallel"),
        name="memkv",
    )(x, g, w)


def _suffix_matrix(tk):
    j = lax.broadcasted_iota(jnp.int32, (2 * tk, tk), 0)
    s = lax.broadcasted_iota(jnp.int32, (2 * tk, tk), 1)
    return ((j > s) & ((j < tk) | (j > s + tk))).astype(BF16)


def _sb_weights(z, carry, mask):
    lz = jnp.minimum(z, 0.0) - jnp.log(1.0 + jnp.exp(-jnp.abs(z)))
    l1 = lz - z
    if mask is not None:
        l1 = jnp.where(mask, l1, 0.0)
    hi, lo = _split_bf16(l1)
    cum = _dot(jnp.concatenate([hi, lo], axis=1), _suffix_matrix(z.shape[1]))
    p = jnp.exp(lz + cum + carry)
    if mask is not None:
        p = jnp.where(mask, p, 0.0)
    return p.astype(BF16), carry + cum[:, :1] + l1[:, :1]


def _fold_descending(n_blocks, alive, fold_block):
    def cond(state):
        i, live = state
        return (i < n_blocks) & live

    def body(state):
        i, _ = state
        return i + 1, fold_block(n_blocks - 1 - i)

    return lax.while_loop(cond, body, (jnp.int32(0), alive))[1]


def _sb_prompt_kernel(q_ref, k_ref, v_ref, o_ref, acc_ref, car_ref, *, tq, pairs):
    qi = pl.program_id(2)
    heads = 2 * pairs
    lane = lax.broadcasted_iota(jnp.int32, (tq, LANES), 1)
    pair_cols = [slice(j * LANES, (j + 1) * LANES) for j in range(pairs)]
    q_heads = []
    for j in range(pairs):
        q = q_ref[:, pair_cols[j]]
        q_heads += [jnp.where(lane < SB_DIM, q, jnp.zeros_like(q)),
                    jnp.where(lane >= SB_DIM, q, jnp.zeros_like(q))]
    acc_ref[...] = jnp.zeros_like(acc_ref)
    car_ref[...] = jnp.zeros_like(car_ref)

    def block(kb, mask):
        z = jnp.concatenate([_dot(q_heads[h], k_ref[kb, pair_cols[h // 2], :]) for h in range(heads)], axis=0)
        p, carry = _sb_weights(z, car_ref[...], mask)
        car_ref[...] = carry
        for h in range(heads):
            acc_ref[h] += _dot_nt(p[h * tq:(h + 1) * tq], v_ref[kb, pair_cols[h // 2], :])
        return jnp.max(carry) > _LOG_F32_ZERO

    r = lax.broadcasted_iota(jnp.int32, (heads * tq, tq), 0)
    c = lax.broadcasted_iota(jnp.int32, (heads * tq, tq), 1)
    alive = block(qi, c < (r & (tq - 1)))
    _fold_descending(qi, alive, lambda kb: block(kb, None))
    for j in range(pairs):
        o_ref[:, pair_cols[j]] = jnp.where(lane < SB_DIM, acc_ref[2 * j], acc_ref[2 * j + 1]).astype(o_ref.dtype)


def _sb_prompt(q, k_t, v_t, *, tq, pairs):
    b, t, w = q.shape
    assert k_t.shape == (b, t // tq, w, tq) and tq & (tq - 1) == 0
    pw = pairs * LANES
    seq = pl.BlockSpec((None, t // tq, pw, tq), lambda bi, p, qi: (bi, 0, p, 0))
    tile = pl.BlockSpec((None, tq, pw), lambda bi, p, qi: (bi, qi, p))
    return pl.pallas_call(
        functools.partial(_sb_prompt_kernel, tq=tq, pairs=pairs),
        grid=(b, w // pw, t // tq),
        in_specs=[tile, seq, seq],
        out_specs=tile,
        out_shape=jax.ShapeDtypeStruct((b, t, w), BF16),
        scratch_shapes=[pltpu.VMEM((2 * pairs, tq, LANES), F32), pltpu.VMEM((2 * pairs * tq, 1), F32)],
        compiler_params=_params("parallel", "parallel", "arbitrary"),
        name="sb_prompt",
    )(q, k_t, v_t)


def _sb_sample_kernel(q_ref, k_ref, v_ref, pk_ref, pv_ref, o_ref, acc_ref, car_ref, *, tk_past):
    t = q_ref.shape[0]
    past_len = pk_ref.shape[-1]
    acc_ref[...] = jnp.zeros_like(acc_ref)
    car_ref[...] = jnp.zeros_like(car_ref)
    head_cols = [slice(h * SB_DIM, (h + 1) * SB_DIM) for h in range(SB_HEADS)]

    def fold_all_heads(logits, weighted_values, mask):
        z = jnp.concatenate([logits(h, q_ref[:, head_cols[h]]) for h in range(SB_HEADS)], axis=0)
        p, carry = _sb_weights(z, car_ref[...], mask)
        car_ref[...] = carry
        for h in range(SB_HEADS):
            acc_ref[h] += weighted_values(h, p[h * t:(h + 1) * t])
        return jnp.max(carry) > _LOG_F32_ZERO

    r = lax.broadcasted_iota(jnp.int32, (SB_HEADS * t, t), 0)
    c = lax.broadcasted_iota(jnp.int32, (SB_HEADS * t, t), 1)
    alive = fold_all_heads(lambda h, qh: _dot_nt(qh, k_ref[:, head_cols[h]]),
                           lambda h, ph: _dot(ph, v_ref[:, head_cols[h]]),
                           c < (r & (t - 1)))

    def past_block(kb_idx):
        pos = pl.ds(pl.multiple_of(kb_idx * tk_past, tk_past), tk_past)
        return fold_all_heads(lambda h, qh: _dot(qh, pk_ref[h, :, pos].astype(BF16)),
                              lambda h, ph: _dot_nt(ph, pv_ref[h, :, pos].astype(BF16)), None)

    _fold_descending(past_len // tk_past, alive, past_block)
    for h in range(SB_HEADS):
        o_ref[:, head_cols[h]] = acc_ref[h].astype(o_ref.dtype)


def _sb_sample(q, k, v, past_k, past_v, *, tk_past):
    b, t, w = q.shape
    assert t & (t - 1) == 0, "query chunk length must be a power of two"
    cur = pl.BlockSpec((None, t, w), lambda bi: (bi, 0, 0))
    past = pl.BlockSpec((None,) + past_k.shape[1:], lambda bi: (bi, 0, 0, 0))
    return pl.pallas_call(
        functools.partial(_sb_sample_kernel, tk_past=tk_past),
        grid=(b,),
        in_specs=[cur, cur, cur, past, past],
        out_specs=cur,
        out_shape=jax.ShapeDtypeStruct((b, t, w), BF16),
        scratch_shapes=[pltpu.VMEM((SB_HEADS, t, SB_DIM), F32), pltpu.VMEM((SB_HEADS * t, 1), F32)],
        compiler_params=_params("parallel"),
        name="sb_sample",
    )(q, k, v, past_k, past_v)


def _gla_kernel(gq_ref, gk_ref, gv_ref, la_ref, gr_ref, gn_ref, s0_ref, o_ref, st_ref, st_scr, *, n_chunks):
    c_len = GLA_CHUNK
    tt = n_chunks * c_len
    kw = GLA_HEADS * GLA_DK
    t = pl.program_id(1)

    @pl.when(t == 0)
    def _():
        st_scr[...] = s0_ref[...]

    r = lax.broadcasted_iota(jnp.int32, (tt, tt), 0)
    c = lax.broadcasted_iota(jnp.int32, (tt, tt), 1)
    chunk_bits = c_len.bit_length() - 1
    same_chunk = (r >> chunk_bits) == (c >> chunk_bits)
    tril = (same_chunk & (r >= c)).astype(BF16)
    ones = same_chunk.astype(BF16)
    la_hi, la_lo = _split_bf16(la_ref[...])
    b = _dot(tril, la_hi) + _dot(tril, la_lo)
    b_tot = _dot(ones, la_hi) + _dot(ones, la_lo)
    gk = gk_ref[...]
    qd = gq_ref[...] * jnp.exp(b)
    kd = (gk * jnp.exp(-b)).astype(BF16)
    ke = (gk * jnp.exp(b_tot - b)).astype(BF16)
    decay = jnp.exp(b_tot)

    rc = lax.broadcasted_iota(jnp.int32, (GLA_HEADS * c_len, c_len), 0)
    cc = lax.broadcasted_iota(jnp.int32, (GLA_HEADS * c_len, c_len), 1)
    causal = (rc & (c_len - 1)) >= cc
    lane_q = lax.broadcasted_iota(jnp.int32, (c_len, kw), 1)
    lane_s = lax.broadcasted_iota(jnp.int32, (GLA_DV, kw), 1)

    for ci in range(n_chunks):
        rows = slice(ci * c_len, (ci + 1) * c_len)
        qd_c = qd[rows]
        qs = jnp.concatenate(
            [jnp.where((lane_q >= h * GLA_DK) & (lane_q < (h + 1) * GLA_DK), qd_c, 0.0)
             for h in range(GLA_HEADS)], axis=0).astype(BF16)
        att = jnp.where(causal, _dot_nt(qs, kd[rows]), 0.0).astype(BF16)
        st = st_scr[...]
        o_state = _dot_nt(qs, st.astype(BF16))
        v = gv_ref[rows, :]
        upd = lax.dot_general(v, ke[rows], _TN, preferred_element_type=F32)
        st_new = st * decay[ci * c_len:ci * c_len + 1, :]
        for h in range(GLA_HEADS):
            hrows = slice(h * c_len, (h + 1) * c_len)
            cols = slice(h * GLA_DV, (h + 1) * GLA_DV)
            o = _dot(att[hrows], v[:, cols]) + o_state[hrows]
            on = _rmsnorm(o, gn_ref[:, cols])
            gr = gr_ref[rows, cols]
            o_ref[rows, cols] = (on * (gr * jax.nn.sigmoid(gr))).astype(o_ref.dtype)
            in_head = (lane_s >= h * GLA_DK) & (lane_s < (h + 1) * GLA_DK)
            st_new = st_new + jnp.where(in_head, upd[cols], 0.0)
        st_scr[...] = st_new

    @pl.when(t == pl.num_programs(1) - 1)
    def _():
        st_ref[...] = st_scr[...]


def _gla(gq, gk, gv, la, gr, gn, st0, *, tt):
    b, t, kw = gq.shape
    vw = gv.shape[-1]
    tok = lambda w: pl.BlockSpec((None, tt, w), lambda bi, ti: (bi, ti, 0))
    state = pl.BlockSpec((None, GLA_DV, kw), lambda bi, ti: (bi, 0, 0))
    return pl.pallas_call(
        functools.partial(_gla_kernel, n_chunks=tt // GLA_CHUNK),
        grid=(b, t // tt),
        in_specs=[tok(kw), tok(kw), tok(vw), tok(kw), tok(vw), _resident(gn), state],
        out_specs=[tok(vw), state],
        out_shape=[jax.ShapeDtypeStruct((b, t, vw), BF16), jax.ShapeDtypeStruct((b, GLA_DV, kw), F32)],
        scratch_shapes=[pltpu.VMEM((GLA_DV, kw), F32)],
        compiler_params=_params("parallel", "arbitrary"),
        name="gla",
    )(gq, gk, gv, la, gr, gn, st0)


def _mem_kernel(q_ref, mk_ref, mv_ref, o_ref):
    head_major = len(mk_ref.shape) == 3
    for h in range(MEM_HEADS):
        cols = slice(h * MEM_DIM, (h + 1) * MEM_DIM)
        kh = mk_ref[:, h, :] if head_major else mk_ref[:, cols]
        vh = mv_ref[:, h, :] if head_major else mv_ref[:, cols]
        s = _dot_nt(q_ref[:, cols], kh.astype(BF16)) * MEM_DIM ** -0.5
        e = jnp.exp(s - jnp.max(s, axis=-1, keepdims=True))
        p = e * (1.0 / jnp.sum(e, axis=-1, keepdims=True))
        o_ref[:, cols] = _dot(p.astype(BF16), vh.astype(BF16)).astype(o_ref.dtype)


def _mem_attention(q, mk, mv, *, tq):
    b, t, w = q.shape
    tile = pl.BlockSpec((None, tq, w), lambda bi, qi: (bi, qi, 0))
    mem = pl.BlockSpec((None,) + mk.shape[1:], lambda bi, qi: (bi,) + (0,) * (mk.ndim - 1))
    return pl.pallas_call(
        _mem_kernel,
        grid=(b, t // tq),
        in_specs=[tile, mem, mem],
        out_specs=tile,
        out_shape=jax.ShapeDtypeStruct((b, t, w), BF16),
        compiler_params=_params("parallel", "arbitrary"),
        name="mem_attention",
    )(q, mk, mv)


def _merge_kernel(x_ref, osb_ref, ogla_ref, omem_ref, gmix_ref, wgt_ref, wbr_ref, wout_ref,
                  gffn_ref, wr_hi_ref, wr_lo_ref, br_ref, h_ref, ht_ref, idx_ref, wts_ref, cnt_ref, run_ref):
    x = x_ref[...]
    d = x.shape[-1]
    xn = _rmsnorm(x, gmix_ref[...]).astype(BF16)
    mixed = None
    for n, o_ref in enumerate((osb_ref, ogla_ref, omem_ref)):
        gate = jax.nn.sigmoid(_dot(xn, wgt_ref[:, n * d:(n + 1) * d]))
        term = gate * _dot(o_ref[...], wbr_ref[n])
        mixed = term if mixed is None else mixed + term
    h = x + _dot(mixed.astype(BF16), wout_ref[...])
    h_ref[...] = h
    hn = _rmsnorm(h, gffn_ref[...])
    ht_ref[...] = hn

    hn_hi, hn_lo = _split_bf16(hn)
    logits = (_dot(hn_hi, wr_hi_ref[...]) + _dot(hn_lo, wr_hi_ref[...])
              + _dot(hn_hi, wr_lo_ref[...]) + br_ref[...])
    lane = lax.broadcasted_iota(jnp.int32, logits.shape, 1)
    rmax = lambda a: jnp.max(a, axis=-1, keepdims=True)
    rmin = lambda a: jnp.min(a, axis=-1, keepdims=True)
    rsum = lambda a: jnp.sum(a, axis=-1, keepdims=True)

    lc = jnp.where(lane < N_GROUPS, logits, NEG_BIG)
    mc = rmax(lc)
    grp = rmin(jnp.where(lc == mc, lane, LANES))
    p_grp = 1.0 / rsum(jnp.exp(lc - mc))

    lo = N_GROUPS + grp * EXPERTS_PER_GROUP
    in_grp = (lane >= lo) & (lane < lo + EXPERTS_PER_GROUP)
    lf = jnp.where(in_grp, logits, NEG_BIG)
    ef = jnp.exp(lf - rmax(lf))
    pf = jnp.where(in_grp, ef / rsum(ef), -1.0)
    v1 = rmax(pf)
    i1 = rmin(jnp.where(pf == v1, lane, LANES))
    pf2 = jnp.where(lane == i1, -1.0, pf)
    v2 = rmax(pf2)
    i2 = rmin(jnp.where(pf2 == v2, lane, LANES))
    tot = v1 + v2
    e1 = i1 - N_GROUPS
    e2 = i2 - N_GROUPS
    wts_ref[...] = jnp.where(lane == 0, p_grp * (v1 / tot), jnp.where(lane == 1, p_grp * (v2 / tot), 0.0))

    @pl.when(pl.program_id(0) == 0)
    def _():
        run_ref[...] = jnp.zeros_like(run_ref)

    tm = x.shape[0]
    hit1 = lane == e1
    hit2 = lane == e2
    both = (hit1 | hit2).astype(BF16)
    earlier = (lax.broadcasted_iota(jnp.int32, (tm, tm), 1)
               < lax.broadcasted_iota(jnp.int32, (tm, tm), 0)).astype(BF16)
    before = _dot(earlier, both) + run_ref[...]
    r1 = rsum(jnp.where(hit1, before, 0.0)).astype(jnp.int32)
    r2 = rsum(jnp.where(hit2, before, 0.0)).astype(jnp.int32)
    run = before[tm - 1:tm, :] + both[tm - 1:tm, :].astype(F32)
    run_ref[...] = run
    cnt_ref[...] = run
    idx_ref[...] = jnp.where(lane == 0, e1, jnp.where(lane == 1, e2,
                             jnp.where(lane == 2, r1, jnp.where(lane == 3, r2, 0))))


def _merge(x, osb, ogla, omem, gmix, wgt, wbr, wout, gffn, wr_hi, wr_lo, br, *, tm):
    n, d = x.shape
    bw = osb.shape[1]
    row = lambda w: pl.BlockSpec((tm, w), lambda i: (i, 0))
    return pl.pallas_call(
        _merge_kernel,
        grid=(n // tm,),
        in_specs=[row(d), row(bw), row(bw), row(bw), _resident(gmix), _resident(wgt), _resident(wbr),
                  _resident(wout), _resident(gffn), _resident(wr_hi), _resident(wr_lo), _resident(br)],
        out_specs=[row(d), row(d), row(LANES), row(LANES), pl.BlockSpec((1, LANES), lambda i: (0, 0))],
        out_shape=[jax.ShapeDtypeStruct((n, d), F32), jax.ShapeDtypeStruct((n, d), F32),
                   jax.ShapeDtypeStruct((n, LANES), jnp.int32), jax.ShapeDtypeStruct((n, LANES), F32),
                   jax.ShapeDtypeStruct((1, LANES), F32)],
        scratch_shapes=[pltpu.VMEM((1, LANES), F32)],
        compiler_params=_params("arbitrary"),
        name="merge_route",
    )(x, osb, ogla, omem, gmix, wgt, wbr, wout, gffn, wr_hi, wr_lo, br)


def _dispatch_kernel(pend_ref, dest_ref, *refs, tm_e, group_steps):
    x_refs = refs[:len(group_steps)]
    xb_ref, zero_ref, sem = refs[len(group_steps):]
    groups = x_refs[0].shape[0]
    step = pl.program_id(0)

    @pl.when(step == 0)
    def _():
        zero_ref[...] = jnp.zeros_like(zero_ref)

        def segment_end(e):
            end = pend_ref[e]
            prev = jnp.where(e > 0, pend_ref[jnp.maximum(e - 1, 0)], 0)
            return end, end > prev

        def zero_copy(end):
            first = pl.multiple_of(end - tm_e, tm_e)
            return pltpu.make_async_copy(zero_ref, xb_ref.at[pl.ds(first, tm_e), :], sem)

        def start(e, carry):
            end, used = segment_end(e)

            @pl.when(used)
            def _():
                zero_copy(end).start()
            return carry

        def wait(e, carry):
            end, used = segment_end(e)

            @pl.when(used)
            def _():
                zero_copy(end).wait()
            return carry

        lax.fori_loop(0, N_EXPERTS, start, 0)
        lax.fori_loop(0, N_EXPERTS, wait, 0)

        first_free = pend_ref[N_EXPERTS - 1] // tm_e
        n_blocks = xb_ref.shape[0] // tm_e
        lax.fori_loop(first_free, n_blocks, lambda j, c: (zero_copy((j + 1) * tm_e).start(), c)[1], 0)
        lax.fori_loop(first_free, n_blocks, lambda j, c: (zero_copy((j + 1) * tm_e).wait(), c)[1], 0)

    def copy_tile(x_ref):
        def issue(g, carry):
            for u in range(SUBLANES):
                for k in range(TOP_K):
                    slot = dest_ref[(g * SUBLANES + u) * TOP_K + k]
                    pltpu.make_async_copy(x_ref.at[g, pl.ds(u, 1), :], xb_ref.at[pl.ds(slot, 1), :], sem).start()
            return carry

        lax.fori_loop(0, groups, issue, 0)

    first = 0
    for x_ref, n_steps in zip(x_refs, group_steps):
        pl.when((step >= first) & (step < first + n_steps))(functools.partial(copy_tile, x_ref))
        first += n_steps
    tile_rows = xb_ref.at[pl.ds(0, groups * SUBLANES), :]
    for k in range(TOP_K):
        pltpu.make_async_copy(tile_rows, tile_rows, sem).wait()


def _dispatch(xs, dests, pend, n_slots, *, tm, tm_e):
    d = xs[0].shape[1]
    group_steps = [x.shape[0] // tm for x in xs]
    firsts = [sum(group_steps[:j]) for j in range(len(xs))]
    x_specs = [pl.BlockSpec((tm // SUBLANES, SUBLANES, d),
                            functools.partial(lambda i, pe, f, s: (jnp.clip(i - f, 0, s - 1), 0, 0), f=f, s=s))
               for f, s in zip(firsts, group_steps)]
    grid_spec = pltpu.PrefetchScalarGridSpec(
        num_scalar_prefetch=1,
        grid=(sum(group_steps),),
        in_specs=[pl.BlockSpec((tm * TOP_K,), lambda i, pe: (i,), memory_space=pltpu.SMEM)] + x_specs,
        out_specs=pl.BlockSpec(memory_space=pl.ANY),
        scratch_shapes=[pltpu.VMEM((tm_e, d), F32), pltpu.SemaphoreType.DMA(())],
    )
    return pl.pallas_call(
        functools.partial(_dispatch_kernel, tm_e=tm_e, group_steps=tuple(group_steps)),
        grid_spec=grid_spec,
        out_shape=jax.ShapeDtypeStruct((n_slots, d), F32),
        compiler_params=_params("arbitrary"),
        name="dispatch",
    )(pend, jnp.concatenate(dests), *[x.reshape(x.shape[0] // SUBLANES, SUBLANES, d) for x in xs])


def _expert_kernel(blk_e_ref, n_used_ref, x_ref, wg_ref, wu_ref, wd_ref, y_ref, wg_bf, wu_bf, wd_bf):
    i = pl.program_id(0)

    @pl.when((i == 0) | (blk_e_ref[i] != blk_e_ref[jnp.maximum(i - 1, 0)]))
    def _():
        wg_bf[...] = wg_ref[...].astype(BF16)
        wu_bf[...] = wu_ref[...].astype(BF16)
        wd_bf[...] = wd_ref[...].astype(BF16)

    @pl.when(i < n_used_ref[0])
    def _():
        x = x_ref[...].astype(BF16)
        g = _dot(x, wg_bf[...])
        u = _dot(x, wu_bf[...])
        a = (g * jax.nn.sigmoid(g) * u).astype(BF16)
        y_ref[...] = _dot(a, wd_bf[...])

    @pl.when(i >= n_used_ref[0])
    def _():
        y_ref[...] = jnp.zeros_like(y_ref)


def _experts(xb, blk_e, n_used, wg, wu, wd, *, tm):
    p, d = xb.shape
    de = wg.shape[-1]
    grid_spec = pltpu.PrefetchScalarGridSpec(
        num_scalar_prefetch=2,
        grid=(p // tm,),
        in_specs=[pl.BlockSpec((tm, d), lambda i, be, nu: (jnp.minimum(i, nu[0] - 1), 0)),
                  pl.BlockSpec((None, d, de), lambda i, be, nu: (be[i], 0, 0)),
                  pl.BlockSpec((None, d, de), lambda i, be, nu: (be[i], 0, 0)),
                  pl.BlockSpec((None, de, d), lambda i, be, nu: (be[i], 0, 0))],
        out_specs=pl.BlockSpec((tm, d), lambda i, be, nu: (i, 0)),
        scratch_shapes=[pltpu.VMEM((d, de), BF16), pltpu.VMEM((d, de), BF16), pltpu.VMEM((de, d), BF16)],
    )
    return pl.pallas_call(
        _expert_kernel,
        grid_spec=grid_spec,
        out_shape=jax.ShapeDtypeStruct((p, d), F32),
        compiler_params=_params("arbitrary"),
        name="experts",
    )(blk_e, n_used, xb, wg, wu, wd)


def _final_kernel(pos_ref, pos_next_ref, h_ref, wts_ref, g_ref, yb_ref, o_ref, rows_ref, sem, *, n_steps):
    i = pl.program_id(0)
    tm, d = h_ref.shape
    cur = i % 2

    def gather(slots_ref, buf):
        def issue(g, carry):
            for u in range(SUBLANES):
                for k in range(TOP_K):
                    slot = slots_ref[(g * SUBLANES + u) * TOP_K + k]
                    pltpu.make_async_copy(yb_ref.at[pl.ds(slot, 1), :],
                                          rows_ref.at[buf, k, g, pl.ds(u, 1), :], sem.at[buf]).start()
            return carry

        lax.fori_loop(0, tm // SUBLANES, issue, 0)

    @pl.when(i == 0)
    def _():
        gather(pos_ref, 0)

    @pl.when(i + 1 < n_steps)
    def _():
        gather(pos_next_ref, 1 - cur)

    for k in range(TOP_K):
        pltpu.make_async_copy(rows_ref.at[cur, k], rows_ref.at[cur, k], sem.at[cur]).wait()
    w = wts_ref[...]
    y0 = rows_ref[cur, 0].reshape(tm, d)
    y1 = rows_ref[cur, 1].reshape(tm, d)
    o_ref[...] = _rmsnorm(h_ref[...] + (y0 * w[:, 0:1] + y1 * w[:, 1:2]), g_ref[...])


def _final(h, yb, pos, wts, g, *, tm):
    n, d = h.shape
    steps = n // tm
    row = lambda w: pl.BlockSpec((tm, w), lambda i: (i, 0))
    slots = lambda nxt: pl.BlockSpec((tm * TOP_K,), lambda i: (jnp.minimum(i + nxt, steps - 1),),
                                     memory_space=pltpu.SMEM)
    return pl.pallas_call(
        functools.partial(_final_kernel, n_steps=steps),
        grid=(steps,),
        in_specs=[slots(0), slots(1), row(d), row(LANES), _resident(g), pl.BlockSpec(memory_space=pl.ANY)],
        out_specs=row(d),
        out_shape=jax.ShapeDtypeStruct((n, d), F32),
        scratch_shapes=[pltpu.VMEM((2, TOP_K, tm // SUBLANES, SUBLANES, d), F32),
                        pltpu.SemaphoreType.DMA((2,))],
        compiler_params=_params("arbitrary"),
        name="combine_norm",
    )(pos, pos, h, wts, g, yb)


def _dispatch_plan(routes, counts, tm):
    counts = [c[0, :N_EXPERTS].astype(jnp.int32) for c in counts]
    total = sum(counts)
    padded = (total + tm - 1) // tm * tm
    pend = jnp.cumsum(padded)
    experts = jnp.arange(N_EXPERTS, dtype=jnp.int32)
    dests, first, n_assign = [], pend - padded, 0
    for route, cnt in zip(routes, counts):
        a = route.shape[0] * TOP_K
        flat_e = route[:, :TOP_K].reshape(a)
        rank = route[:, TOP_K:2 * TOP_K].reshape(a)
        base = jnp.sum(jnp.where(flat_e[:, None] == experts[None, :], first[None, :], 0), axis=1)
        dests.append((base + rank).astype(jnp.int32))
        first = first + cnt
        n_assign += a
    n_blk = -(-(n_assign + N_EXPERTS * (tm - 1)) // tm)
    blk_start = jnp.arange(n_blk, dtype=jnp.int32) * tm
    blk_e = jnp.minimum(jnp.sum((pend[None, :] <= blk_start[:, None]).astype(jnp.int32), axis=1),
                        N_EXPERTS - 1).astype(jnp.int32)
    n_used = (pend[-1] // tm).astype(jnp.int32).reshape(1)
    return dests, pend.astype(jnp.int32), blk_e, n_used, n_blk


def _moe(groups, wg, wu, wd, g_final, *, tm_e):
    dests, pend, blk_e, n_used, n_blk = _dispatch_plan(
        [g["route"] for g in groups], [g["counts"] for g in groups], tm_e)
    tm_dispatch = 2 * ROW_TILE
    while any(g["ht"].shape[0] % tm_dispatch for g in groups):
        tm_dispatch //= 2
    xb = _dispatch([g["ht"] for g in groups], dests, pend, n_blk * tm_e, tm=tm_dispatch, tm_e=tm_e)
    yb = _experts(xb, blk_e, n_used, wg, wu, wd, tm=tm_e)
    return [_final(g["h"], yb, dest, g["rw"], g_final, tm=tm_dispatch) for g, dest in zip(groups, dests)]


def _layer(x, sb_past, gla_state, mem_kv, wts, *, tm, tq_sb, tt_gla, tq_mem):
    b, t, d = x.shape
    n = b * t
    xf = x.reshape(n, d)
    sq, skb, svb, sk, sv, gq, gk, gv, la, gr, mq = _proj(
        xf, wts["g_mix"], wts["w_proj"], wts["wa2"], wts["ba"], wts["w_kvt"] if sb_past is None else None,
        tm=tm, seq_len=t, key_block=tq_sb)
    r3 = lambda a: a.reshape(b, t, a.shape[-1])
    if sb_past is None:
        o_sb = _sb_prompt(r3(sq), skb, svb, tq=tq_sb, pairs=4)
    else:
        o_sb = _sb_sample(r3(sq), r3(skb), r3(svb), sb_past[0], sb_past[1], tk_past=SB_TILE)
    o_gla, st = _gla(r3(gq), r3(gk), r3(gv), r3(la), r3(gr), wts["g_gla"], gla_state, tt=tt_gla)
    o_mem = _mem_attention(r3(mq), mem_kv[0], mem_kv[1], tq=tq_mem)
    h, ht, route, rw, counts = _merge(xf, o_sb.reshape(n, -1), o_gla.reshape(n, -1), o_mem.reshape(n, -1),
                                      wts["g_mix"], wts["w_gt"], wts["w_br"], wts["w_out"], wts["g_ffn"],
                                      wts["wr_hi"], wts["wr_lo"], wts["b_r"], tm=tm)
    return dict(h=h, ht=ht, route=route, rw=rw, counts=counts, tm=tm), sk, sv, st


ROW_TILE = 512
SB_TILE = 256
GLA_TILE = 256


def _tiles(b, t, *, chunked):
    if chunked:
        return dict(tm=min(ROW_TILE, b * t), tq_sb=t, tt_gla=t, tq_mem=t)
    return dict(tm=min(ROW_TILE, b * t), tq_sb=min(SB_TILE, t), tt_gla=min(GLA_TILE, t),
                tq_mem=min(ROW_TILE, t))


def _state_to_t(s):
    b = s.shape[0]
    return s.transpose(0, 3, 1, 2).reshape(b, GLA_DV, GLA_HEADS * GLA_DK)


def _state_from_t(st):
    b = st.shape[0]
    return st.reshape(b, GLA_DV, GLA_HEADS, GLA_DK).transpose(0, 2, 3, 1)


def kernel(x_prompt, x_sample, mem_prompt, cache_sb_k, cache_sb_v, state_gla, cache_mem_k, cache_mem_v,
           norm_mix, w_in, w_gla_a2, b_gla_a, gla_norm, norm_mem, w_mem_kv, w_branch, w_out,
           norm_ffn, w_coarse, b_coarse, w_fine, b_fine, w_e_gate, w_e_up, w_e_down, norm_final):
    assert w_in.shape[0] == 1, "single-layer model"
    d = x_prompt.shape[-1]
    bp, tp, _ = x_prompt.shape
    bs, ts, _ = x_sample.shape
    bw = d // 2
    assert bw == SB_W == GLA_VW == MEM_W

    w = w_in[0]
    offs = [0]
    for s in (SB_W, SB_W, SB_W, GLA_KW, GLA_KW, GLA_VW, GLA_RANK, GLA_VW, MEM_W, N_BRANCH * d):
        offs.append(offs[-1] + s)
    col = lambda i: w[:, offs[i]:offs[i + 1]]
    w_proj = jnp.concatenate(
        [col(0), col(1), col(2), col(3), col(4), col(5), col(7), col(8),
         jnp.pad(col(6), ((0, 0), (0, LANES - GLA_RANK)))], axis=1).astype(BF16)
    assert w_proj.shape[1] == _PROJ_W
    row2 = lambda a: a.reshape(1, -1)
    n_route = N_GROUPS + N_EXPERTS
    w_route = jnp.concatenate([w_coarse[0], w_fine[0].transpose(1, 0, 2).reshape(d, N_EXPERTS)], axis=1)
    w_route = jnp.pad(w_route, ((0, 0), (0, LANES - n_route)))
    wr_hi = w_route.astype(BF16)
    wr_lo = (w_route - wr_hi.astype(F32)).astype(BF16)
    b_route = jnp.pad(jnp.concatenate([b_coarse[0], b_fine[0].reshape(-1)]), (0, LANES - n_route)).reshape(1, LANES)
    wts = dict(
        g_mix=row2(norm_mix[0]), w_proj=w_proj,
        w_kvt=jnp.concatenate([col(1), col(2)], axis=1).T.astype(BF16),
        wa2=jnp.pad(w_gla_a2[0], ((0, LANES - GLA_RANK), (0, 0))).astype(BF16), ba=row2(b_gla_a[0]),
        g_gla=row2(gla_norm[0]),
        w_gt=col(9).astype(BF16), w_br=w_branch[0].astype(BF16), w_out=w_out[0].astype(BF16),
        g_ffn=row2(norm_ffn[0]), wr_hi=wr_hi, wr_lo=wr_lo, b_r=b_route,
        wg=w_e_gate[0], wu=w_e_up[0], wd=w_e_down[0],
        g_final=row2(norm_final),
    )

    m = mem_prompt.shape[1]
    mk, mv = _memkv(mem_prompt.reshape(bp * m, d), row2(norm_mem[0]), w_mem_kv[0].astype(BF16),
                    tm=min(ROW_TILE, bp * m))
    st0 = jnp.zeros((bp, GLA_DV, GLA_HEADS * GLA_DK), F32)
    grp_p, sk_p, sv_p, st_p = _layer(x_prompt, None, st0, (mk.reshape(bp, m, bw), mv.reshape(bp, m, bw)), wts,
                                     **_tiles(bp, tp, chunked=False))

    past = (cache_sb_k[0].transpose(0, 2, 3, 1), cache_sb_v[0].transpose(0, 2, 3, 1))
    grp_s, sk_s, sv_s, st_s = _layer(x_sample, past, _state_to_t(state_gla[0]),
                                     (cache_mem_k[0], cache_mem_v[0]), wts, **_tiles(bs, ts, chunked=True))

    y_p, y_s = _moe([grp_p, grp_s], wts["wg"], wts["wu"], wts["wd"], wts["g_final"], tm_e=ROW_TILE)
    y_p, y_s = y_p.reshape(bp, tp, d), y_s.reshape(bs, ts, d)

    hd = lambda a, bb, tt: a.reshape(1, bb, tt, SB_HEADS, SB_DIM)
    return (y_p, y_s,
            sk_p.transpose(0, 3, 1, 2)[None], sv_p.transpose(0, 3, 1, 2)[None],
            _state_from_t(st_p)[None],
            mk.reshape(1, bp, m, MEM_HEADS, MEM_DIM), mv.reshape(1, bp, m, MEM_HEADS, MEM_DIM),
            hd(sk_s, bs, ts), hd(sv_s, bs, ts),
            _state_from_t(st_s)[None])
```
